```python
import jax, jax.numpy as jnp
from jax import lax
import numpy as np

D_MODEL = 1024
BATCH = 2
SEQ = 8192
DEPTH = 4

HEAD_DIM = 64
A_HEADS = 8
B_HEADS = 8
M_HEADS = 4
M_HEAD_DIM = 128
BRANCH_WIDTH = 512
N_BRANCH = 3
MOBA_BLOCK = 256
MOBA_TOPK = 3
IDX_HEADS = 8
IDX_DIM = 64
IDX_TOPK_MAX = 256
N_MEM = 256
ROPE_THETA = 500000.0
ROT_FRACTION = 4
Q_BLOCK = 128
RMS_EPS = 1e-6
IN_WIDTHS = (
    BRANCH_WIDTH, BRANCH_WIDTH, BRANCH_WIDTH, BRANCH_WIDTH,
    BRANCH_WIDTH, BRANCH_WIDTH, BRANCH_WIDTH, BRANCH_WIDTH,
    IDX_HEADS * IDX_DIM, IDX_DIM, IDX_HEADS,
    BRANCH_WIDTH, BRANCH_WIDTH,
    N_BRANCH * D_MODEL,
)
IN_COLS = sum(IN_WIDTHS)

kernel_name = 'gated_parallel_moba_dsa_memory_trunk'


def rmsnorm(x, g):
    xf = x.astype(jnp.float32)
    y = xf * lax.rsqrt(jnp.mean(xf * xf, axis=-1, keepdims=True) + RMS_EPS)
    return (y * g.astype(jnp.float32)).astype(x.dtype)


def partial_rope(x, pos):
    d = x.shape[-1]
    rot = d // ROT_FRACTION
    half = rot // 2
    inv_freq = ROPE_THETA ** (-jnp.arange(half, dtype=jnp.float32) / half)
    ang = pos.astype(jnp.float32)[:, None] * inv_freq[None, :]
    cos = jnp.cos(ang)[:, None, :].astype(x.dtype)
    sin = jnp.sin(ang)[:, None, :].astype(x.dtype)
    x1 = x[..., :half]
    x2 = x[..., half:rot]
    return jnp.concatenate([x1 * cos - x2 * sin, x2 * cos + x1 * sin, x[..., rot:]], axis=-1)


def masked_softmax(s, mask):
    s = jnp.where(mask, s.astype(jnp.float32), -jnp.inf)
    return jax.nn.softmax(s, axis=-1)


def moba_attention(q, k, v):
    b, s, h, d = q.shape
    nb = -(-s // MOBA_BLOCK)
    pad = nb * MOBA_BLOCK - s
    padw = ((0, 0), (0, pad), (0, 0), (0, 0))
    kb = jnp.pad(k, padw).reshape(b, nb, MOBA_BLOCK, h, d).transpose(0, 3, 1, 2, 4)
    vb = jnp.pad(v, padw).reshape(b, nb, MOBA_BLOCK, h, d).transpose(0, 3, 1, 2, 4)
    kmean = jnp.mean(kb.astype(jnp.float32), axis=3).astype(q.dtype)
    kk = min(MOBA_TOPK, nb - 1)
    scale = d ** -0.5
    b_idx = jnp.arange(b)[:, None, None, None]
    h_idx = jnp.arange(h)[None, :, None, None]
    blk_ids = jnp.arange(nb)

    def one_block(qi):
        start = qi * Q_BLOCK
        own = start // MOBA_BLOCK
        qpos = start + jnp.arange(Q_BLOCK)
        qb = lax.dynamic_slice_in_dim(q, start, Q_BLOCK, axis=1).transpose(0, 2, 1, 3)
        k_own = lax.dynamic_index_in_dim(kb, own, axis=2, keepdims=False)
        v_own = lax.dynamic_index_in_dim(vb, own, axis=2, keepdims=False)
        kpos_own = own * MOBA_BLOCK + jnp.arange(MOBA_BLOCK)
        s_own = jnp.einsum('bhqd,bhkd->bhqk', qb, k_own) * scale
        m_own = jnp.broadcast_to(kpos_own[None, :] <= qpos[:, None], s_own.shape)
        if kk == 0:
            p = masked_softmax(s_own, m_own).astype(v.dtype)
            o = jnp.einsum('bhqk,bhkd->bhqd', p, v_own)
        else:
            gate = jnp.einsum('bhqd,bhnd->bhqn', qb, kmean).astype(jnp.float32)
            gate = jnp.where(blk_ids < own, gate, -jnp.inf)
            _, sel = lax.top_k(gate, kk)
            valid = sel < own
            kg = kb[b_idx, h_idx, sel]
            vg = vb[b_idx, h_idx, sel]
            s_past = jnp.einsum('bhqd,bhqjkd->bhqjk', qb, kg) * scale
            m_past = jnp.broadcast_to(valid[..., None], s_past.shape)
            n_past = kk * MOBA_BLOCK
            s_all = jnp.concatenate([s_past.reshape(b, h, Q_BLOCK, n_past), s_own], axis=-1)
            m_all = jnp.concatenate([m_past.reshape(b, h, Q_BLOCK, n_past), m_own], axis=-1)
            p = masked_softmax(s_all, m_all).astype(v.dtype)
            p_past = p[..., :n_past].reshape(b, h, Q_BLOCK, kk, MOBA_BLOCK)
            o = (jnp.einsum('bhqjk,bhqjkd->bhqd', p_past, vg)
                 + jnp.einsum('bhqk,bhkd->bhqd', p[..., n_past:], v_own))
        return o.transpose(0, 2, 1, 3)

    out = lax.map(one_block, jnp.arange(s // Q_BLOCK))
    return out.transpose(1, 0, 2, 3, 4).reshape(b, s, h, d)


def dsa_attention(q, k, v, qi, ki, wi):
    b, s, h, d = q.shape
    n_sel = min(IDX_TOPK_MAX, s // 4)
    scale = d ** -0.5
    kpos = jnp.arange(s)
    b_idx = jnp.arange(b)[:, None, None]

    def one_block(blk):
        start = blk * Q_BLOCK
        qpos = start + jnp.arange(Q_BLOCK)
        q_b = lax.dynamic_slice_in_dim(q, start, Q_BLOCK, axis=1)
        qi_b = lax.dynamic_slice_in_dim(qi, start, Q_BLOCK, axis=1)
        wi_b = lax.dynamic_slice_in_dim(wi, start, Q_BLOCK, axis=1)
        rel = jax.nn.relu(jnp.einsum('bqhd,bsd->bqhs', qi_b, ki))
        idx_score = jnp.einsum('bqhs,bqh->bqs', rel, wi_b).astype(jnp.float32)
        idx_score = jnp.where(kpos[None, :] <= qpos[:, None], idx_score, -jnp.inf)
        _, sel = lax.top_k(idx_score, n_sel)
        valid = sel <= qpos[None, :, None]
        kg = k[b_idx, sel]
        vg = v[b_idx, sel]
        sc = jnp.einsum('bqhd,bqnhd->bqhn', q_b, kg) * scale
        p = masked_softmax(sc, valid[:, :, None, :]).astype(v.dtype)
        return jnp.einsum('bqhn,bqnhd->bqhd', p, vg)

    out = lax.map(one_block, jnp.arange(s // Q_BLOCK))
    return out.transpose(1, 0, 2, 3, 4).reshape(b, s, h, d)


def memory_attention(qm, km, vm):
    sc = jnp.einsum('bshd,bmhd->bhsm', qm, km) * (qm.shape[-1] ** -0.5)
    p = jax.nn.softmax(sc.astype(jnp.float32), axis=-1).astype(vm.dtype)
    return jnp.einsum('bhsm,bmhd->bshd', p, vm)


def hybrid_layer(x, mem, pos, norm_g, w_in, mem_norm_g, w_mem_kv, w_branch, w_out):
    b, s, _ = x.shape
    h = rmsnorm(x, norm_g)
    proj = jnp.einsum('bsd,dc->bsc', h, w_in)
    offs = np.cumsum(IN_WIDTHS)[:-1].tolist()
    (a_q, a_k, a_v, a_g, b_q, b_k, b_v, b_g,
     i_q, i_k, i_w, m_q, m_g, mix) = jnp.split(proj, offs, axis=-1)

    qa = partial_rope(a_q.reshape(b, s, A_HEADS, HEAD_DIM), pos)
    ka = partial_rope(a_k.reshape(b, s, A_HEADS, HEAD_DIM), pos)
    va = a_v.reshape(b, s, A_HEADS, HEAD_DIM)
    ya = moba_attention(qa, ka, va).reshape(b, s, BRANCH_WIDTH) * jax.nn.silu(a_g)

    qb = partial_rope(b_q.reshape(b, s, B_HEADS, HEAD_DIM), pos)
    kb = partial_rope(b_k.reshape(b, s, B_HEADS, HEAD_DIM), pos)
    vb = b_v.reshape(b, s, B_HEADS, HEAD_DIM)
    qi = partial_rope(i_q.reshape(b, s, IDX_HEADS, IDX_DIM), pos) * (IDX_DIM ** -0.5)
    ki = partial_rope(i_k[:, :, None, :], pos)[:, :, 0, :]
    wi = i_w * (IDX_HEADS ** -0.5)
    yb = dsa_attention(qb, kb, vb, qi, ki, wi).reshape(b, s, BRANCH_WIDTH) * jax.nn.silu(b_g)

    mn = rmsnorm(mem, mem_norm_g)
    km, vm = jnp.split(jnp.einsum('bmd,dc->bmc', mn, w_mem_kv), 2, axis=-1)
    km = km.reshape(b, -1, M_HEADS, M_HEAD_DIM)
    vm = vm.reshape(b, -1, M_HEADS, M_HEAD_DIM)
    ym = memory_attention(m_q.reshape(b, s, M_HEADS, M_HEAD_DIM), km, vm)
    ym = ym.reshape(b, s, BRANCH_WIDTH) * jax.nn.silu(m_g)

    ys = jnp.stack([ya, yb, ym], axis=2)
    up = jnp.einsum('bsnw,nwd->bsnd', ys, w_branch)
    gates = jax.nn.sigmoid(mix.reshape(b, s, N_BRANCH, D_MODEL))
    merged = jnp.sum(gates * up, axis=2)
    return x + jnp.einsum('bsd,de->bse', merged, w_out)


def setup_inputs(seed: int = 0) -> dict:
    key = jax.random.key(seed)
    ks = jax.random.split(key, 9)
    f32 = jnp.float32
    x = jax.random.normal(ks[0], (BATCH, SEQ, D_MODEL), f32)
    mem = jax.random.normal(ks[1], (BATCH, N_MEM, D_MODEL), f32)
    norm_g = 1.0 + 0.02 * jax.random.normal(ks[2], (DEPTH, D_MODEL), f32)
    w_in = jax.random.normal(ks[3], (DEPTH, D_MODEL, IN_COLS), f32) * (D_MODEL ** -0.5)
    mem_norm_g = 1.0 + 0.02 * jax.random.normal(ks[4], (DEPTH, D_MODEL), f32)
    w_mem_kv = jax.random.normal(ks[5], (DEPTH, D_MODEL, 2 * BRANCH_WIDTH), f32) * (D_MODEL ** -0.5)
    w_branch = jax.random.normal(ks[6], (DEPTH, N_BRANCH, BRANCH_WIDTH, D_MODEL), f32) * (BRANCH_WIDTH ** -0.5)
    w_out = jax.random.normal(ks[7], (DEPTH, D_MODEL, D_MODEL), f32) * (D_MODEL ** -0.5)
    final_g = 1.0 + 0.02 * jax.random.normal(ks[8], (D_MODEL,), f32)
    return {'x': x, 'mem': mem, 'norm_g': norm_g, 'w_in': w_in, 'mem_norm_g': mem_norm_g,
            'w_mem_kv': w_mem_kv, 'w_branch': w_branch, 'w_out': w_out, 'final_g': final_g}


def reference(x, mem, norm_g, w_in, mem_norm_g, w_mem_kv, w_branch, w_out, final_g):
    pos = jnp.arange(x.shape[1], dtype=jnp.int32)
    for l in range(DEPTH):
        x = hybrid_layer(x, mem, pos, norm_g[l], w_in[l], mem_norm_g[l], w_mem_kv[l],
                         w_branch[l], w_out[l])
    return rmsnorm(x, final_g)
```

```python
import functools

import jax
import jax.numpy as jnp
import numpy as np
from jax import lax
from jax.experimental import pallas as pl
from jax.experimental.pallas import tpu as pltpu

F32 = jnp.float32
BF16 = jnp.bfloat16

HEAD_DIM = 64
N_HEADS = 8
BRANCH_WIDTH = 512
M_HEADS = 4
M_HEAD_DIM = 128
MOBA_BLOCK = 256
MOBA_TOPK = 3
IDX_TOPK_MAX = 256
ROPE_THETA = 500000.0
ROT_HALF = HEAD_DIM // 4 // 2
RMS_EPS = 1e-6

LANES = 128
SUBLANES = 8
VMEM_LIMIT = 56 * 1024 * 1024
NEG_BIG = -1e30
INT_MIN = -(2 ** 31)


def _cparams(n_axes):
    return pltpu.CompilerParams(dimension_semantics=("arbitrary",) * n_axes,
                                vmem_limit_bytes=VMEM_LIMIT)


def _rmsnorm_kernel(x_ref, g_ref, o_ref):
    xf = x_ref[...]
    y = xf * lax.rsqrt(jnp.mean(xf * xf, axis=-1, keepdims=True) + RMS_EPS)
    o_ref[...] = (y * g_ref[...]).astype(o_ref.dtype)


def _rmsnorm(x2d, g, out_dtype, tm):
    m, d = x2d.shape
    return pl.pallas_call(
        _rmsnorm_kernel,
        grid=(m // tm,),
        in_specs=[pl.BlockSpec((tm, d), lambda i: (i, 0)),
                  pl.BlockSpec((1, d), lambda i: (0, 0))],
        out_specs=pl.BlockSpec((tm, d), lambda i: (i, 0)),
        out_shape=jax.ShapeDtypeStruct((m, d), out_dtype),
        compiler_params=_cparams(1),
        name="rmsnorm",
    )(x2d, g.reshape(1, d))


def _mm_tok_kernel(h_ref, w_ref, *rest, epilogue):
    o_ref = rest[-1]
    acc = jnp.dot(h_ref[...], w_ref[...], preferred_element_type=F32)
    if epilogue == "silu":
        acc = acc * jax.nn.sigmoid(acc)
    elif epilogue == "sigmoid":
        acc = jax.nn.sigmoid(acc)
    if epilogue == "rope":
        c_ref, s1_ref, s2_ref = rest[:3]
        c, s1, s2 = c_ref[...], s1_ref[...], s2_ref[...]
        for j in range(acc.shape[1] // LANES):
            piece = acc[:, j * LANES:(j + 1) * LANES]
            up = pltpu.roll(piece, LANES - ROT_HALF, 1)
            down = pltpu.roll(piece, ROT_HALF, 1)
            o_ref[:, j * LANES:(j + 1) * LANES] = (piece * c + up * s1 + down * s2).astype(o_ref.dtype)
    else:
        o_ref[...] = acc.astype(o_ref.dtype)


def _mm_tok(h, w, epilogue, out_dtype, tm, tn, rope_tabs=None, seq=None):
    m, k = h.shape
    n = w.shape[1]
    in_specs = [pl.BlockSpec((tm, k), lambda i, j: (i, 0)),
                pl.BlockSpec((k, tn), lambda i, j: (0, j))]
    args = [h, w]
    if epilogue == "rope":
        nt = seq // tm
        for t in rope_tabs:
            in_specs.append(pl.BlockSpec((tm, LANES), lambda i, j: (i % nt, 0)))
            args.append(t)
    return pl.pallas_call(
        functools.partial(_mm_tok_kernel, epilogue=epilogue),
        grid=(m // tm, n // tn),
        in_specs=in_specs,
        out_specs=pl.BlockSpec((tm, tn), lambda i, j: (i, j)),
        out_shape=jax.ShapeDtypeStruct((m, n), out_dtype),
        compiler_params=_cparams(2),
        name="proj_tok_" + epilogue,
    )(*args)


def _mm_feat_kernel(wt_ref, h_ref, *rest, epilogue):
    o_ref = rest[-1]
    acc = lax.dot_general(wt_ref[...], h_ref[...], (((1,), (1,)), ((), ())),
                          preferred_element_type=F32)
    if epilogue == "rope":
        cos_ref, sin_ref = rest[:2]
        c, s = cos_ref[...], sin_ref[...]
        pieces = []
        for hh in range(acc.shape[0] // HEAD_DIM):
            base = hh * HEAD_DIM
            x1 = acc[base:base + ROT_HALF]
            x2 = acc[base + ROT_HALF:base + 2 * ROT_HALF]
            pieces += [x1 * c - x2 * s, x2 * c + x1 * s, acc[base + 2 * ROT_HALF:base + HEAD_DIM]]
        acc = jnp.concatenate(pieces, axis=0)
    o_ref[...] = acc.astype(o_ref.dtype)


def _mm_feat(wt, h, epilogue, out_dtype, tm, tn, rope_tabs=None, seq=None):
    n, k = wt.shape
    m = h.shape[0]
    in_specs = [pl.BlockSpec((tn, k), lambda i, j: (j, 0)),
                pl.BlockSpec((tm, k), lambda i, j: (i, 0))]
    args = [wt, h]
    if epilogue == "rope":
        nt = seq // tm
        for t in rope_tabs:
            in_specs.append(pl.BlockSpec((SUBLANES, tm), lambda i, j: (0, i % nt)))
            args.append(t)
    return pl.pallas_call(
        functools.partial(_mm_feat_kernel, epilogue=epilogue),
        grid=(m // tm, n // tn),
        in_specs=in_specs,
        out_specs=pl.BlockSpec((tn, tm), lambda i, j: (j, i)),
        out_shape=jax.ShapeDtypeStruct((n, m), out_dtype),
        compiler_params=_cparams(2),
        name="proj_feat_" + epilogue,
    )(*args)


def _head_operand(qt_ref, head):
    pair, sub = divmod(head, 2)
    qp = qt_ref[pair * LANES:(pair + 1) * LANES, :].astype(F32)
    row = lax.broadcasted_iota(jnp.int32, qp.shape, 0)
    keep = (row < HEAD_DIM) if sub == 0 else (row >= HEAD_DIM)
    return jnp.where(keep, qp, 0.0).astype(BF16)


def _store_head(ot_ref, head, val):
    pair, sub = divmod(head, 2)
    lo = pair * LANES + sub * HEAD_DIM
    ot_ref[lo:lo + HEAD_DIM, :] = val[sub * HEAD_DIM:(sub + 1) * HEAD_DIM, :]


def _moba_kernel(qt_ref, k_ref, vt_ref, o_ref, kmean_ref, sel_ref, ot_ref, *, seq, topk):
    i = pl.program_id(1)
    blk = MOBA_BLOCK
    nb = seq // blk
    tq = blk

    @pl.when(i == 0)
    def _():
        r = lax.broadcasted_iota(jnp.int32, (nb, seq), 0)
        c = lax.broadcasted_iota(jnp.int32, (nb, seq), 1)
        member = jnp.where((c >= r * blk) & (c < (r + 1) * blk), 1.0, 0.0).astype(BF16)
        ksum = jnp.dot(member, k_ref[...], preferred_element_type=F32)
        kmean_ref[...] = (ksum * (1.0 / blk)).astype(BF16)

    blk_id = lax.broadcasted_iota(jnp.int32, (nb, tq), 0)
    past = blk_id < i
    krow = lax.broadcasted_iota(jnp.int32, (blk, tq), 0)
    qcol = lax.broadcasted_iota(jnp.int32, (blk, tq), 1)
    causal = krow <= qcol
    own_start = pl.multiple_of(i * blk, blk)

    for head in range(N_HEADS):
        pair = head // 2
        lanes = slice(pair * LANES, (pair + 1) * LANES)
        qtm = _head_operand(qt_ref, head)

        gate = jnp.dot(kmean_ref[:, lanes], qtm, preferred_element_type=F32)
        gate = jnp.where(past, gate, -jnp.inf)
        sel = jnp.zeros((nb, tq), jnp.bool_)
        for _ in range(topk):
            mx = jnp.max(gate, axis=0, keepdims=True)
            first = jnp.min(jnp.where(gate == mx, blk_id, nb), axis=0, keepdims=True)
            hit = blk_id == first
            sel = sel | hit
            gate = jnp.where(hit, -jnp.inf, gate)
        sel_ref[...] = jnp.where(sel & past, 1.0, 0.0)

        s = jnp.dot(k_ref[pl.ds(own_start, blk), lanes], qtm, preferred_element_type=F32)
        s = jnp.where(causal, s, -jnp.inf)
        m0 = jnp.max(s, axis=0, keepdims=True)
        p0 = jnp.exp(s - m0)
        l0 = jnp.sum(p0, axis=0, keepdims=True)
        acc0 = jnp.dot(vt_ref[lanes, pl.ds(own_start, blk)], p0.astype(BF16),
                       preferred_element_type=F32)

        def body(j, carry, lanes=lanes, qtm=qtm):
            m, l, acc = carry
            start = pl.multiple_of(j * blk, blk)
            sj = jnp.dot(k_ref[pl.ds(start, blk), lanes], qtm, preferred_element_type=F32)
            sj = jnp.where(sel_ref[pl.ds(j, 1), :] > 0.5, sj, -jnp.inf)
            m_new = jnp.maximum(m, jnp.max(sj, axis=0, keepdims=True))
            alpha = jnp.exp(m - m_new)
            pj = jnp.exp(sj - m_new)
            l = alpha * l + jnp.sum(pj, axis=0, keepdims=True)
            acc = alpha * acc + jnp.dot(vt_ref[lanes, pl.ds(start, blk)], pj.astype(BF16),
                                        preferred_element_type=F32)
            return m_new, l, acc

        _, l, acc = lax.fori_loop(0, i, body, (m0, l0, acc0))
        _store_head(ot_ref, head, acc / l)

    o_ref[...] = ot_ref[...].T


def _moba(qt_all, k_all, vt_all, batch, seq, q_row_blk, k_col_blk, v_row_blk):
    tq = MOBA_BLOCK
    nq = seq // tq
    nb = seq // MOBA_BLOCK
    topk = min(MOBA_TOPK, nb - 1)
    w = BRANCH_WIDTH
    return pl.pallas_call(
        functools.partial(_moba_kernel, seq=seq, topk=topk),
        grid=(batch, nq),
        in_specs=[
            pl.BlockSpec((w, tq), lambda b, i: (q_row_blk, b * nq + i)),
            pl.BlockSpec((seq, w), lambda b, i: (b, k_col_blk), pipeline_mode=pl.Buffered(1)),
            pl.BlockSpec((w, seq), lambda b, i: (v_row_blk, b), pipeline_mode=pl.Buffered(1)),
        ],
        out_specs=pl.BlockSpec((tq, w), lambda b, i: (b * nq + i, 0)),
        out_shape=jax.ShapeDtypeStruct((batch * seq, w), F32),
        scratch_shapes=[pltpu.VMEM((nb, w), BF16),
                        pltpu.VMEM((nb, tq), F32),
                        pltpu.VMEM((w, tq), F32)],
        compiler_params=_cparams(2),
        name="moba",
    )(qt_all, k_all, vt_all)


def _dsa_kernel(qit_ref, ki_ref, wt_ref, qt_ref, k_ref, vt_ref, o_ref,
                score_ref, bias_ref, ot_ref, *, tq, kc, n_sel):
    i = pl.program_id(1)
    nch = (i * tq + tq + kc - 1) // kc
    krow = lax.broadcasted_iota(jnp.int32, (kc, tq), 0)
    qpos = i * tq + lax.broadcasted_iota(jnp.int32, (kc, tq), 1)
    n_sel_f = float(n_sel)

    qi_ops = [_head_operand(qit_ref, h) for h in range(N_HEADS)]
    w_rows = [wt_ref[h:h + 1, :] * (N_HEADS ** -0.5) for h in range(N_HEADS)]

    def score_body(c, carry):
        start = pl.multiple_of(c * kc, kc)
        kic = ki_ref[pl.ds(start, kc), :]
        acc = jnp.zeros((kc, tq), F32)
        for h in range(N_HEADS):
            r = jnp.dot(kic, qi_ops[h], preferred_element_type=F32)
            acc = acc + jnp.maximum(r, 0.0) * w_rows[h]
        score_ref[pl.ds(start, kc), :] = jnp.where(start + krow <= qpos, acc, -jnp.inf)
        return carry

    lax.fori_loop(0, nch, score_body, 0)

    def count(pred):
        def body(c, cnt):
            sc = score_ref[pl.ds(pl.multiple_of(c * kc, kc), kc), :]
            ones = jnp.where(pred(sc), 1.0, 0.0)
            return cnt + jnp.sum(ones.reshape(kc // SUBLANES, SUBLANES, tq), axis=0)
        part = lax.fori_loop(0, nch, body, jnp.zeros((SUBLANES, tq), F32))
        return jnp.sum(part, axis=0, keepdims=True)

    def decode(key):
        bits = jnp.where(key >= 0, key, key ^ jnp.int32(0x7FFFFFFF))
        return lax.bitcast_convert_type(bits, F32)

    def bit_body(it, key):
        cand = key + jnp.left_shift(jnp.int32(1), 31 - it)
        cf = decode(cand)
        cnt = count(lambda sc: sc >= cf)
        return jnp.where(cnt >= n_sel_f, cand, key)

    key = lax.fori_loop(0, 32, bit_body, jnp.full((1, tq), INT_MIN, jnp.int32))
    short = key == INT_MIN
    thr = jnp.where(short, -jnp.inf, decode(key))
    n_gt = count(lambda sc: sc > thr)
    n_ge = count(lambda sc: sc >= thr)
    need = n_sel_f - n_gt
    has_tie = jnp.max(jnp.where(short, 0.0, n_ge - n_sel_f)) > 0.0

    @pl.when(jnp.logical_not(has_tie))
    def _():
        def body(c, carry):
            start = pl.multiple_of(c * kc, kc)
            sc = score_ref[pl.ds(start, kc), :]
            keep = (sc >= thr) & (start + krow <= qpos)
            bias_ref[pl.ds(start, kc), :] = jnp.where(keep, 0.0, NEG_BIG)
            return carry
        lax.fori_loop(0, nch, body, 0)

    @pl.when(has_tie)
    def _():
        lower = (lax.broadcasted_iota(jnp.int32, (kc, kc), 1)
                 < lax.broadcasted_iota(jnp.int32, (kc, kc), 0))
        lower = jnp.where(lower, 1.0, 0.0).astype(BF16)

        def body(c, seen):
            start = pl.multiple_of(c * kc, kc)
            sc = score_ref[pl.ds(start, kc), :]
            eq = jnp.where(sc == thr, 1.0, 0.0)
            rank = seen + jnp.dot(lower, eq.astype(BF16), preferred_element_type=F32)
            keep = (sc > thr) | ((sc == thr) & (rank < need))
            keep = keep & (start + krow <= qpos)
            bias_ref[pl.ds(start, kc), :] = jnp.where(keep, 0.0, NEG_BIG)
            return seen + jnp.sum(eq, axis=0, keepdims=True)
        lax.fori_loop(0, nch, body, jnp.zeros((1, tq), F32))

    for head in range(N_HEADS):
        pair = head // 2
        lanes = slice(pair * LANES, (pair + 1) * LANES)
        qtm = _head_operand(qt_ref, head)

        def body(c, carry, lanes=lanes, qtm=qtm):
            m, l, acc = carry
            start = pl.multiple_of(c * kc, kc)
            s = jnp.dot(k_ref[pl.ds(start, kc), lanes], qtm, preferred_element_type=F32)
            s = s + bias_ref[pl.ds(start, kc), :]
            m_new = jnp.maximum(m, jnp.max(s, axis=0, keepdims=True))
            alpha = jnp.exp(m - m_new)
            p = jnp.exp(s - m_new)
            l = alpha * l + jnp.sum(p, axis=0, keepdims=True)
            acc = alpha * acc + jnp.dot(vt_ref[lanes, pl.ds(start, kc)], p.astype(BF16),
                                        preferred_element_type=F32)
            return m_new, l, acc

        init = (jnp.full((1, tq), NEG_BIG, F32), jnp.zeros((1, tq), F32),
                jnp.zeros((LANES, tq), F32))
        _, l, acc = lax.fori_loop(0, nch, body, init)
        _store_head(ot_ref, head, acc / l)

    o_ref[...] = ot_ref[...].T


def _dsa(qt_all, k_all, vt_all, wt, batch, seq, tq, kc):
    nq = seq // tq
    w = BRANCH_WIDTH
    n_sel = min(IDX_TOPK_MAX, seq // 4)
    ki_col_blk = (2 * w) // LANES
    return pl.pallas_call(
        functools.partial(_dsa_kernel, tq=tq, kc=kc, n_sel=n_sel),
        grid=(batch, nq),
        in_specs=[
            pl.BlockSpec((w, tq), lambda b, i: (2, b * nq + i)),
            pl.BlockSpec((seq, LANES), lambda b, i: (b, ki_col_blk), pipeline_mode=pl.Buffered(1)),
            pl.BlockSpec((2 * SUBLANES, tq), lambda b, i: (0, b * nq + i)),
            pl.BlockSpec((w, tq), lambda b, i: (1, b * nq + i)),
            pl.BlockSpec((seq, w), lambda b, i: (b, 1), pipeline_mode=pl.Buffered(1)),
            pl.BlockSpec((w, seq), lambda b, i: (1, b), pipeline_mode=pl.Buffered(1)),
        ],
        out_specs=pl.BlockSpec((tq, w), lambda b, i: (b * nq + i, 0)),
        out_shape=jax.ShapeDtypeStruct((batch * seq, w), F32),
        scratch_shapes=[pltpu.VMEM((seq, tq), F32),
                        pltpu.VMEM((seq, tq), F32),
                        pltpu.VMEM((w, tq), F32)],
        compiler_params=_cparams(2),
        name="dsa",
    )(qt_all, k_all, wt, qt_all, k_all, vt_all)


def _merge_kernel(ya_ref, yb_ref, mq_ref, kv_ref, g_ref, mix_ref, wb_ref, wo_ref, x_ref, o_ref):
    w = BRANCH_WIDTH
    d = x_ref.shape[1]
    mq = mq_ref[...]
    ym_heads = []
    for h in range(M_HEADS):
        cols = slice(h * M_HEAD_DIM, (h + 1) * M_HEAD_DIM)
        km = kv_ref[:, cols]
        vm = kv_ref[:, w + h * M_HEAD_DIM:w + (h + 1) * M_HEAD_DIM]
        sc = lax.dot_general(mq[:, cols], km, (((1,), (1,)), ((), ())),
                             preferred_element_type=F32) * (M_HEAD_DIM ** -0.5)
        sc = sc - jnp.max(sc, axis=-1, keepdims=True)
        e = jnp.exp(sc)
        p = e / jnp.sum(e, axis=-1, keepdims=True)
        ym_heads.append(jnp.dot(p.astype(BF16), vm, preferred_element_type=F32))
    ym = jnp.concatenate(ym_heads, axis=-1)
    ys = (ya_ref[...], yb_ref[...], ym)
    merged = None
    for n in range(3):
        y = (ys[n] * g_ref[:, n * w:(n + 1) * w]).astype(BF16)
        up = jnp.dot(y, wb_ref[n], preferred_element_type=F32)
        term = mix_ref[:, n * d:(n + 1) * d] * up
        merged = term if merged is None else merged + term
    o_ref[...] = x_ref[...] + jnp.dot(merged.astype(BF16), wo_ref[...], preferred_element_type=F32)


def _merge(ya, yb, mq, kv, gates, mix, wb, wo, x2d, seq, n_mem, tm):
    m, d = x2d.shape
    w = BRANCH_WIDTH
    nt = seq // tm
    return pl.pallas_call(
        _merge_kernel,
        grid=(m // tm,),
        in_specs=[
            pl.BlockSpec((tm, w), lambda i: (i, 0)),
            pl.BlockSpec((tm, w), lambda i: (i, 0)),
            pl.BlockSpec((tm, w), lambda i: (i, 0)),
            pl.BlockSpec((n_mem, 2 * w), lambda i: (i // nt, 0)),
            pl.BlockSpec((tm, 3 * w), lambda i: (i, 0)),
            pl.BlockSpec((tm, 3 * d), lambda i: (i, 0)),
            pl.BlockSpec((3, w, d), lambda i: (0, 0, 0)),
            pl.BlockSpec((d, d), lambda i: (0, 0)),
            pl.BlockSpec((tm, d), lambda i: (i, 0)),
        ],
        out_specs=pl.BlockSpec((tm, d), lambda i: (i, 0)),
        out_shape=jax.ShapeDtypeStruct((m, d), F32),
        compiler_params=_cparams(1),
        name="merge",
    )(ya, yb, mq, kv, gates, mix, wb, wo, x2d)


def _rope_tables(seq):
    inv_freq = ROPE_THETA ** (-jnp.arange(ROT_HALF, dtype=F32) / ROT_HALF)
    ang = jnp.arange(seq, dtype=jnp.int32).astype(F32)[:, None] * inv_freq[None, :]
    cos, sin = jnp.cos(ang), jnp.sin(ang)
    r = np.arange(LANES) % HEAD_DIM
    f = r % ROT_HALF
    first = jnp.asarray(r < ROT_HALF)
    second = jnp.asarray((r >= ROT_HALF) & (r < 2 * ROT_HALF))
    rot = jnp.asarray(r < 2 * ROT_HALF)
    c_tok = jnp.where(rot[None, :], cos[:, f], 1.0)
    s1_tok = jnp.where(first[None, :], -sin[:, f], 0.0)
    s2_tok = jnp.where(second[None, :], sin[:, f], 0.0)
    return (c_tok, s1_tok, s2_tok), (cos.T, sin.T)


def kernel(x, mem, norm_g, w_in, mem_norm_g, w_mem_kv, w_branch, w_out, final_g):
    batch, seq, d = x.shape
    n_mem = mem.shape[1]
    depth = norm_g.shape[0]
    w = BRANCH_WIDTH
    m = batch * seq
    assert seq % MOBA_BLOCK == 0 and d % LANES == 0

    tm = min(1024, seq)
    tm_merge = min(512, seq)
    dsa_tq = 128
    dsa_kc = min(512, seq)

    tok_tabs, feat_tabs = _rope_tables(seq)
    offs = np.cumsum([0, w, w, w, w, w, w, w, w, N_HEADS * HEAD_DIM, HEAD_DIM, N_HEADS, w, w, 3 * d])
    (o_aq, o_ak, o_av, o_ag, o_bq, o_bk, o_bv, o_bg, o_iq, o_ik, o_iw, o_mq, o_mg, o_mix, o_end) = offs
    attn_scale = HEAD_DIM ** -0.5
    idx_scale = HEAD_DIM ** -0.5

    x2d = x.reshape(m, d)
    mem2d = mem.reshape(batch * n_mem, d)
    for l in range(depth):
        wl = w_in[l]
        col = lambda a, b: wl[:, a:b]
        w_k = jnp.concatenate([col(o_ak, o_av), col(o_bk, o_bv), col(o_ik, o_iw), col(o_ik, o_iw)],
                              axis=1).astype(BF16)
        w_qt = jnp.concatenate([col(o_aq, o_ak) * attn_scale, col(o_bq, o_bk) * attn_scale,
                                col(o_iq, o_ik) * idx_scale], axis=1).T.astype(BF16)
        w_vt = jnp.concatenate([col(o_av, o_ag), col(o_bv, o_bg)], axis=1).T.astype(BF16)
        w_iwt = jnp.pad(col(o_iw, o_mq).T, ((0, 2 * SUBLANES - N_HEADS), (0, 0))).astype(BF16)
        w_mq = col(o_mq, o_mg).astype(BF16)
        w_g = jnp.concatenate([col(o_ag, o_bq), col(o_bg, o_iq), col(o_mg, o_mix)], axis=1).astype(BF16)
        w_mix = col(o_mix, o_end).astype(BF16)

        h = _rmsnorm(x2d, norm_g[l], BF16, tm)
        k_all = _mm_tok(h, w_k, "rope", BF16, tm, w_k.shape[1], tok_tabs, seq)
        qt_all = _mm_feat(w_qt, h, "rope", BF16, tm, w, feat_tabs, seq)
        vt_all = _mm_feat(w_vt, h, "none", BF16, tm, w)
        iwt = _mm_feat(w_iwt, h, "none", F32, tm, 2 * SUBLANES)
        mq = _mm_tok(h, w_mq, "none", BF16, tm, w)
        gates = _mm_tok(h, w_g, "silu", F32, tm, w)
        mix = _mm_tok(h, w_mix, "sigmoid", F32, tm, 2 * w)

        ya = _moba(qt_all, k_all, vt_all, batch, seq, 0, 0, 0)
        yb = _dsa(qt_all, k_all, vt_all, iwt, batch, seq, dsa_tq, dsa_kc)

        mn = _rmsnorm(mem2d, mem_norm_g[l], BF16, n_mem)
        kv = _mm_tok(mn, w_mem_kv[l].astype(BF16), "none", BF16, n_mem, 2 * w)

        x2d = _merge(ya, yb, mq, kv, gates, mix, w_branch[l].astype(BF16), w_out[l].astype(BF16),
                     x2d, seq, n_mem, tm_merge)

    out = _rmsnorm(x2d, final_g, F32, tm)
    return out.reshape(batch, seq, d)
```

```python
import functools

import jax
import jax.numpy as jnp
import numpy as np
from jax import lax
from jax.experimental import pallas as pl
from jax.experimental.pallas import tpu as pltpu

F32 = jnp.float32
BF16 = jnp.bfloat16

HEAD_DIM = 64
N_HEADS = 8
BRANCH_WIDTH = 512
M_HEADS = 4
M_HEAD_DIM = 128
MOBA_BLOCK = 256
MOBA_TOPK = 3
IDX_TOPK_MAX = 256
ROPE_THETA = 500000.0
ROT_HALF = HEAD_DIM // 4 // 2
RMS_EPS = 1e-6

LANES = 128
SUBLANES = 8
VMEM_LIMIT = 56 * 1024 * 1024
NEG_BIG = -1e30
INT_MIN = -(2 ** 31)


def _cparams(n_axes):
    return pltpu.CompilerParams(dimension_semantics=("arbitrary",) * n_axes,
                                vmem_limit_bytes=VMEM_LIMIT)


def _rmsnorm_kernel(x_ref, g_ref, o_ref):
    xf = x_ref[...]
    y = xf * lax.rsqrt(jnp.mean(xf * xf, axis=-1, keepdims=True) + RMS_EPS)
    o_ref[...] = (y * g_ref[...]).astype(o_ref.dtype)


def _rmsnorm(x2d, g, out_dtype, tm):
    m, d = x2d.shape
    return pl.pallas_call(
        _rmsnorm_kernel,
        grid=(m // tm,),
        in_specs=[pl.BlockSpec((tm, d), lambda i: (i, 0)),
                  pl.BlockSpec((1, d), lambda i: (0, 0))],
        out_specs=pl.BlockSpec((tm, d), lambda i: (i, 0)),
        out_shape=jax.ShapeDtypeStruct((m, d), out_dtype),
        compiler_params=_cparams(1),
        name="rmsnorm",
    )(x2d, g.reshape(1, d))


def _mm_tok_kernel(h_ref, w_ref, *rest, epilogue):
    o_ref = rest[-1]
    acc = jnp.dot(h_ref[...], w_ref[...], preferred_element_type=F32)
    if epilogue == "silu":
        acc = acc * jax.nn.sigmoid(acc)
    elif epilogue == "sigmoid":
        acc = jax.nn.sigmoid(acc)
    if epilogue == "rope":
        c_ref, s1_ref, s2_ref = rest[:3]
        c, s1, s2 = c_ref[...], s1_ref[...], s2_ref[...]
        for j in range(acc.shape[1] // LANES):
            piece = acc[:, j * LANES:(j + 1) * LANES]
            up = pltpu.roll(piece, LANES - ROT_HALF, 1)
            down = pltpu.roll(piece, ROT_HALF, 1)
            o_ref[:, j * LANES:(j + 1) * LANES] = (piece * c + up * s1 + down * s2).astype(o_ref.dtype)
    else:
        o_ref[...] = acc.astype(o_ref.dtype)


def _mm_tok(h, w, epilogue, out_dtype, tm, tn, rope_tabs=None, seq=None):
    m, k = h.shape
    n = w.shape[1]
    in_specs = [pl.BlockSpec((tm, k), lambda i, j: (i, 0)),
                pl.BlockSpec((k, tn), lambda i, j: (0, j))]
    args = [h, w]
    if epilogue == "rope":
        nt = seq // tm
        for t in rope_tabs:
            in_specs.append(pl.BlockSpec((tm, LANES), lambda i, j: (i % nt, 0)))
            args.append(t)
    return pl.pallas_call(
        functools.partial(_mm_tok_kernel, epilogue=epilogue),
        grid=(m // tm, n // tn),
        in_specs=in_specs,
        out_specs=pl.BlockSpec((tm, tn), lambda i, j: (i, j)),
        out_shape=jax.ShapeDtypeStruct((m, n), out_dtype),
        compiler_params=_cparams(2),
        name="proj_tok_" + epilogue,
    )(*args)


def _mm_feat_kernel(wt_ref, h_ref, *rest, epilogue):
    o_ref = rest[-1]
    acc = lax.dot_general(wt_ref[...], h_ref[...], (((1,), (1,)), ((), ())),
                          preferred_element_type=F32)
    if epilogue == "rope":
        cos_ref, sin_ref = rest[:2]
        c, s = cos_ref[...], sin_ref[...]
        pieces = []
        for hh in range(acc.shape[0] // HEAD_DIM):
            base = hh * HEAD_DIM
            x1 = acc[base:base + ROT_HALF]
            x2 = acc[base + ROT_HALF:base + 2 * ROT_HALF]
            pieces += [x1 * c - x2 * s, x2 * c + x1 * s, acc[base + 2 * ROT_HALF:base + HEAD_DIM]]
        acc = jnp.concatenate(pieces, axis=0)
    o_ref[...] = acc.astype(o_ref.dtype)


def _mm_feat(wt, h, epilogue, out_dtype, tm, tn, rope_tabs=None, seq=None):
    n, k = wt.shape
    m = h.shape[0]
    in_specs = [pl.BlockSpec((tn, k), lambda i, j: (j, 0)),
                pl.BlockSpec((tm, k), lambda i, j: (i, 0))]
    args = [wt, h]
    if epilogue == "rope":
        nt = seq // tm
        for t in rope_tabs:
            in_specs.append(pl.BlockSpec((SUBLANES, tm), lambda i, j: (0, i % nt)))
            args.append(t)
    return pl.pallas_call(
        functools.partial(_mm_feat_kernel, epilogue=epilogue),
        grid=(m // tm, n // tn),
        in_specs=in_specs,
        out_specs=pl.BlockSpec((tn, tm), lambda i, j: (j, i)),
        out_shape=jax.ShapeDtypeStruct((n, m), out_dtype),
        compiler_params=_cparams(2),
        name="proj_feat_" + epilogue,
    )(*args)


N_PAIRS = N_HEADS // 2


def _pair_operand(qt_ref, pair):
    qp = qt_ref[pair * LANES:(pair + 1) * LANES, :].astype(F32)
    row = lax.broadcasted_iota(jnp.int32, qp.shape, 0)
    lo = jnp.where(row < HEAD_DIM, qp, 0.0)
    hi = jnp.where(row >= HEAD_DIM, qp, 0.0)
    return jnp.concatenate([lo, hi], axis=1).astype(BF16)


ONES_ROWS = 16
QK_LOOKAHEAD = 4


def _flash_init(m_ref, acc_ref):
    m_ref[...] = jnp.full(m_ref.shape, NEG_BIG, F32)
    acc_ref[...] = jnp.zeros(acc_ref.shape, F32)


def _flash_update(s, vt_tile, m_ref, acc_ref, p):
    keys = vt_tile.shape[1]
    m = m_ref[p]
    m_new = jnp.maximum(m, jnp.max(s, axis=0, keepdims=True))
    alpha = jnp.exp2(m - m_new)
    pm = jnp.exp2(s - m_new).astype(BF16)
    vt_ext = jnp.concatenate([vt_tile, jnp.ones((ONES_ROWS, keys), BF16)], axis=0)
    acc_ref[p] = alpha * acc_ref[p] + jnp.dot(vt_ext, pm, preferred_element_type=F32)
    m_ref[p] = m_new


def _flash_tiles(tiles, qk, mask, vt_tile, m_ref, acc_ref):
    ss = [qk(*t) for t in tiles[:QK_LOOKAHEAD]]
    for n, (u, p) in enumerate(tiles):
        if n + QK_LOOKAHEAD < len(tiles):
            ss.append(qk(*tiles[n + QK_LOOKAHEAD]))
        _flash_update(mask(ss[n], u, p), vt_tile(u, p), m_ref, acc_ref, p)
        ss[n] = None


def _flash_finish(o_ref, ot_ref, acc_ref, tq):
    for p in range(N_PAIRS):
        o = acc_ref[p, :LANES] / acc_ref[p, LANES:LANES + 1]
        ot_ref[p * LANES:p * LANES + HEAD_DIM, :] = o[:HEAD_DIM, :tq]
        ot_ref[p * LANES + HEAD_DIM:(p + 1) * LANES, :] = o[HEAD_DIM:, tq:]
    o_ref[...] = ot_ref[...].T


MOBA_UNROLL = 2
DSA_UNROLL = 2


def _moba_kernel(qt_ref, k_ref, vt_ref, o_ref, kmean_ref, qop_ref, sel_ref, m_ref, acc_ref,
                 ot_ref, *, seq, topk):
    i = pl.program_id(1)
    blk = MOBA_BLOCK
    nb = seq // blk
    tq = blk

    @pl.when(i == 0)
    def _():
        r = lax.broadcasted_iota(jnp.int32, (nb, seq), 0)
        c = lax.broadcasted_iota(jnp.int32, (nb, seq), 1)
        member = jnp.where((c >= r * blk) & (c < (r + 1) * blk), 1.0, 0.0).astype(BF16)
        ksum = jnp.dot(member, k_ref[...], preferred_element_type=F32)
        kmean_ref[...] = (ksum * (1.0 / blk)).astype(BF16)

    blk_id = lax.broadcasted_iota(jnp.int32, (nb, 2 * tq), 0)
    past = blk_id < i
    own_start = pl.multiple_of(i * blk, blk)

    for p in range(N_PAIRS):
        lanes = slice(p * LANES, (p + 1) * LANES)
        qop = _pair_operand(qt_ref, p)
        qop_ref[p] = qop
        gate = jnp.dot(kmean_ref[:, lanes], qop, preferred_element_type=F32)
        gate = jnp.where(past, gate, -jnp.inf)
        sel = jnp.zeros((nb, 2 * tq), jnp.bool_)
        for _ in range(topk):
            mx = jnp.max(gate, axis=0, keepdims=True)
            first = jnp.min(jnp.where(gate == mx, blk_id, nb), axis=0, keepdims=True)
            hit = blk_id == first
            sel = sel | hit
            gate = jnp.where(hit, -jnp.inf, gate)
        sel_ref[p] = jnp.where(sel & past, 1.0, 0.0)

    _flash_init(m_ref, acc_ref)

    def lanes(p):
        return slice(p * LANES, (p + 1) * LANES)

    krow = lax.broadcasted_iota(jnp.int32, (blk, 2 * tq), 0)
    qcol = lax.broadcasted_iota(jnp.int32, (blk, 2 * tq), 1)
    causal = krow <= jnp.where(qcol < tq, qcol, qcol - tq)
    _flash_tiles(
        [(0, p) for p in range(N_PAIRS)],
        lambda u, p: jnp.dot(k_ref[pl.ds(own_start, blk), lanes(p)], qop_ref[p], preferred_element_type=F32),
        lambda s, u, p: jnp.where(causal, s, -jnp.inf),
        lambda u, p: vt_ref[lanes(p), pl.ds(own_start, blk)],
        m_ref, acc_ref)

    def body(c, carry):
        js = [c * MOBA_UNROLL + u for u in range(MOBA_UNROLL)]
        starts = [pl.multiple_of(j * blk, blk) for j in js]
        _flash_tiles(
            [(u, p) for u in range(MOBA_UNROLL) for p in range(N_PAIRS)],
            lambda u, p: jnp.dot(k_ref[pl.ds(starts[u], blk), lanes(p)], qop_ref[p],
                                 preferred_element_type=F32),
            lambda s, u, p: jnp.where(sel_ref[p, pl.ds(js[u], 1), :] > 0.5, s, -jnp.inf),
            lambda u, p: vt_ref[lanes(p), pl.ds(starts[u], blk)],
            m_ref, acc_ref)
        return carry

    lax.fori_loop(0, (i + MOBA_UNROLL - 1) // MOBA_UNROLL, body, 0)
    _flash_finish(o_ref, ot_ref, acc_ref, tq)


def _moba(qt_all, k_all, vt_all, batch, seq, q_row_blk, k_col_blk, v_row_blk):
    tq = MOBA_BLOCK
    nq = seq // tq
    nb = seq // MOBA_BLOCK
    topk = min(MOBA_TOPK, nb - 1)
    w = BRANCH_WIDTH
    return pl.pallas_call(
        functools.partial(_moba_kernel, seq=seq, topk=topk),
        grid=(batch, nq),
        in_specs=[
            pl.BlockSpec((w, tq), lambda b, i: (q_row_blk, b * nq + i)),
            pl.BlockSpec((seq, w), lambda b, i: (b, k_col_blk), pipeline_mode=pl.Buffered(1)),
            pl.BlockSpec((w, seq), lambda b, i: (v_row_blk, b), pipeline_mode=pl.Buffered(1)),
        ],
        out_specs=pl.BlockSpec((tq, w), lambda b, i: (b * nq + i, 0)),
        out_shape=jax.ShapeDtypeStruct((batch * seq, w), F32),
        scratch_shapes=[pltpu.VMEM((nb, w), BF16),
                        pltpu.VMEM((N_PAIRS, LANES, 2 * tq), BF16),
                        pltpu.VMEM((N_PAIRS, nb, 2 * tq), F32),
                        pltpu.VMEM((N_PAIRS, 1, 2 * tq), F32),
                        pltpu.VMEM((N_PAIRS, LANES + ONES_ROWS, 2 * tq), F32),
                        pltpu.VMEM((w, tq), F32)],
        compiler_params=_cparams(2),
        name="moba",
    )(qt_all, k_all, vt_all)


def _tree_sum(parts):
    while len(parts) > 1:
        parts = [parts[a] + parts[a + 1] for a in range(0, len(parts) - 1, 2)] + (
            [parts[-1]] if len(parts) % 2 else [])
    return parts[0]


def _dsa_kernel(qit_ref, ki_ref, wt_ref, qt_ref, k_ref, vt_ref, o_ref,
                score_ref, bias_ref, qop_ref, m_ref, acc_ref, ot_ref, *, tq, kc, kcb, n_sel):
    i = pl.program_id(1)
    nch = (i * tq + tq) // kc
    nchb = (i * tq + tq + kcb - 1) // kcb
    ntrip = (nch + DSA_UNROLL - 1) // DSA_UNROLL
    krow = lax.broadcasted_iota(jnp.int32, (kc, tq), 0)
    qpos = i * tq + lax.broadcasted_iota(jnp.int32, (kc, tq), 1)
    n_sel_f = float(n_sel)

    for p in range(N_PAIRS):
        qop_ref[p] = _pair_operand(qit_ref, p)
    w_rows = [wt_ref[h:h + 1, :] * (N_HEADS ** -0.5) for h in range(N_HEADS)]

    def score_body(c, carry):
        start = pl.multiple_of(c * kc, kc)
        kic = ki_ref[pl.ds(start, kc), :]
        acc = jnp.zeros((kc, tq), F32)
        rs = [jnp.dot(kic, qop_ref[p], preferred_element_type=F32) for p in range(N_PAIRS)]
        for p in range(N_PAIRS):
            acc = acc + jnp.maximum(rs[p][:, :tq], 0.0) * w_rows[2 * p]
            acc = acc + jnp.maximum(rs[p][:, tq:], 0.0) * w_rows[2 * p + 1]
        score_ref[pl.ds(start, kc), :] = jnp.where(start + krow <= qpos, acc, -jnp.inf)
        return carry

    lax.fori_loop(0, nch, score_body, 0)

    @pl.when(nch * kc < nchb * kcb)
    def _():
        score_ref[pl.ds(pl.multiple_of(nch * kc, kc), kcb - kc), :] = jnp.full((kcb - kc, tq), -jnp.inf, F32)

    def count(pred):
        n_acc = 4

        def body(c, accs):
            sc = score_ref[pl.ds(pl.multiple_of(c * kcb, kcb), kcb), :]
            accs = list(accs)
            for r in range(kcb // SUBLANES):
                x = sc[r * SUBLANES:(r + 1) * SUBLANES]
                accs[r % n_acc] = accs[r % n_acc] + jnp.where(pred(x), 1.0, 0.0)
            return tuple(accs)
        accs = lax.fori_loop(0, nchb, body, tuple(jnp.zeros((SUBLANES, tq), F32) for _ in range(n_acc)))
        return jnp.sum(_tree_sum(list(accs)), axis=0, keepdims=True)

    def decode(key):
        bits = jnp.where(key >= 0, key, key ^ jnp.int32(0x7FFFFFFF))
        return lax.bitcast_convert_type(bits, F32)

    def bit_body(it, key):
        cand = key + jnp.left_shift(jnp.int32(1), 31 - it)
        cf = decode(cand)
        cnt = count(lambda sc: sc >= cf)
        return jnp.where(cnt >= n_sel_f, cand, key)

    key = lax.fori_loop(0, 32, bit_body, jnp.full((1, tq), INT_MIN, jnp.int32))
    short = key == INT_MIN
    thr = jnp.where(short, -jnp.inf, decode(key))
    n_gt = count(lambda sc: sc > thr)
    n_ge = count(lambda sc: sc >= thr)
    need = n_sel_f - n_gt
    has_tie = jnp.max(jnp.where(short, 0.0, n_ge - n_sel_f)) > 0.0

    @pl.when(jnp.logical_not(has_tie))
    def _():
        def body(c, carry):
            start = pl.multiple_of(c * kc, kc)
            sc = score_ref[pl.ds(start, kc), :]
            keep = (sc >= thr) & (start + krow <= qpos)
            bias_ref[pl.ds(start, kc), :] = jnp.where(keep, 0.0, NEG_BIG)
            return carry
        lax.fori_loop(0, nch, body, 0)

    @pl.when(has_tie)
    def _():
        lower = (lax.broadcasted_iota(jnp.int32, (kc, kc), 1)
                 < lax.broadcasted_iota(jnp.int32, (kc, kc), 0))
        lower = jnp.where(lower, 1.0, 0.0).astype(BF16)

        def body(c, seen):
            start = pl.multiple_of(c * kc, kc)
            sc = score_ref[pl.ds(start, kc), :]
            eq = jnp.where(sc == thr, 1.0, 0.0)
            rank = seen + jnp.dot(lower, eq.astype(BF16), preferred_element_type=F32)
            keep = (sc > thr) | ((sc == thr) & (rank < need))
            keep = keep & (start + krow <= qpos)
            bias_ref[pl.ds(start, kc), :] = jnp.where(keep, 0.0, NEG_BIG)
            return seen + jnp.sum(eq, axis=0, keepdims=True)
        lax.fori_loop(0, nch, body, jnp.zeros((1, tq), F32))

    for p in range(N_PAIRS):
        qop_ref[p] = _pair_operand(qt_ref, p)
    _flash_init(m_ref, acc_ref)

    @pl.when(nch < ntrip * DSA_UNROLL)
    def _():
        pad = (DSA_UNROLL - 1) * kc
        bias_ref[pl.ds(pl.multiple_of(nch * kc, kc), pad), :] = jnp.full((pad, tq), NEG_BIG, F32)

    def lanes(p):
        return slice(p * LANES, (p + 1) * LANES)

    def att_body(c, carry):
        starts = [pl.multiple_of((c * DSA_UNROLL + u) * kc, kc) for u in range(DSA_UNROLL)]

        def add_bias(s, u, p):
            bias = bias_ref[pl.ds(starts[u], kc), :]
            return jnp.concatenate([s[:, :tq] + bias, s[:, tq:] + bias], axis=1)

        _flash_tiles(
            [(u, p) for u in range(DSA_UNROLL) for p in range(N_PAIRS)],
            lambda u, p: jnp.dot(k_ref[pl.ds(starts[u], kc), lanes(p)], qop_ref[p],
                                 preferred_element_type=F32),
            add_bias,
            lambda u, p: vt_ref[lanes(p), pl.ds(starts[u], kc)],
            m_ref, acc_ref)
        return carry

    lax.fori_loop(0, ntrip, att_body, 0)
    _flash_finish(o_ref, ot_ref, acc_ref, tq)


def _dsa(qt_all, k_all, vt_all, wt, batch, seq, tq, kc, kcb):
    nq = seq // tq
    w = BRANCH_WIDTH
    n_sel = min(IDX_TOPK_MAX, seq // 4)
    assert tq % kc == 0 and kcb % kc == 0 and kcb > kc and seq % kcb == 0
    ki_col_blk = (2 * w) // LANES
    return pl.pallas_call(
        functools.partial(_dsa_kernel, tq=tq, kc=kc, kcb=kcb, n_sel=n_sel),
        grid=(batch, nq),
        in_specs=[
            pl.BlockSpec((w, tq), lambda b, i: (2, b * nq + i)),
            pl.BlockSpec((seq, LANES), lambda b, i: (b, ki_col_blk), pipeline_mode=pl.Buffered(1)),
            pl.BlockSpec((2 * SUBLANES, tq), lambda b, i: (0, b * nq + i)),
            pl.BlockSpec((w, tq), lambda b, i: (1, b * nq + i)),
            pl.BlockSpec((seq, w), lambda b, i: (b, 1), pipeline_mode=pl.Buffered(1)),
            pl.BlockSpec((w, seq), lambda b, i: (1, b), pipeline_mode=pl.Buffered(1)),
        ],
        out_specs=pl.BlockSpec((tq, w), lambda b, i: (b * nq + i, 0)),
        out_shape=jax.ShapeDtypeStruct((batch * seq, w), F32),
        scratch_shapes=[pltpu.VMEM((seq, tq), F32),
                        pltpu.VMEM((seq, tq), F32),
                        pltpu.VMEM((N_PAIRS, LANES, 2 * tq), BF16),
                        pltpu.VMEM((N_PAIRS, 1, 2 * tq), F32),
                        pltpu.VMEM((N_PAIRS, LANES + ONES_ROWS, 2 * tq), F32),
                        pltpu.VMEM((w, tq), F32)],
        compiler_params=_cparams(2),
        name="dsa",
    )(qt_all, k_all, wt, qt_all, k_all, vt_all)


def _merge_kernel(ya_ref, yb_ref, mq_ref, kv_ref, g_ref, mix_ref, wb_ref, wo_ref, x_ref, o_ref):
    w = BRANCH_WIDTH
    d = x_ref.shape[1]
    mq = mq_ref[...]
    ym_heads = []
    for h in range(M_HEADS):
        cols = slice(h * M_HEAD_DIM, (h + 1) * M_HEAD_DIM)
        km = kv_ref[:, cols]
        vm = kv_ref[:, w + h * M_HEAD_DIM:w + (h + 1) * M_HEAD_DIM]
        sc = lax.dot_general(mq[:, cols], km, (((1,), (1,)), ((), ())),
                             preferred_element_type=F32) * (M_HEAD_DIM ** -0.5)
        sc = sc - jnp.max(sc, axis=-1, keepdims=True)
        e = jnp.exp(sc)
        p = e / jnp.sum(e, axis=-1, keepdims=True)
        ym_heads.append(jnp.dot(p.astype(BF16), vm, preferred_element_type=F32))
    ym = jnp.concatenate(ym_heads, axis=-1)
    ys = (ya_ref[...], yb_ref[...], ym)
    merged = None
    for n in range(3):
        y = (ys[n] * g_ref[:, n * w:(n + 1) * w]).astype(BF16)
        up = jnp.dot(y, wb_ref[n], preferred_element_type=F32)
        term = mix_ref[:, n * d:(n + 1) * d] * up
        merged = term if merged is None else merged + term
    o_ref[...] = x_ref[...] + jnp.dot(merged.astype(BF16), wo_ref[...], preferred_element_type=F32)


def _merge(ya, yb, mq, kv, gates, mix, wb, wo, x2d, seq, n_mem, tm):
    m, d = x2d.shape
    w = BRANCH_WIDTH
    nt = seq // tm
    return pl.pallas_call(
        _merge_kernel,
        grid=(m // tm,),
        in_specs=[
            pl.BlockSpec((tm, w), lambda i: (i, 0)),
            pl.BlockSpec((tm, w), lambda i: (i, 0)),
            pl.BlockSpec((tm, w), lambda i: (i, 0)),
            pl.BlockSpec((n_mem, 2 * w), lambda i: (i // nt, 0)),
            pl.BlockSpec((tm, 3 * w), lambda i: (i, 0)),
            pl.BlockSpec((tm, 3 * d), lambda i: (i, 0)),
            pl.BlockSpec((3, w, d), lambda i: (0, 0, 0)),
            pl.BlockSpec((d, d), lambda i: (0, 0)),
            pl.BlockSpec((tm, d), lambda i: (i, 0)),
        ],
        out_specs=pl.BlockSpec((tm, d), lambda i: (i, 0)),
        out_shape=jax.ShapeDtypeStruct((m, d), F32),
        compiler_params=_cparams(1),
        name="merge",
    )(ya, yb, mq, kv, gates, mix, wb, wo, x2d)


def _rope_tables(seq):
    inv_freq = ROPE_THETA ** (-jnp.arange(ROT_HALF, dtype=F32) / ROT_HALF)
    ang = jnp.arange(seq, dtype=jnp.int32).astype(F32)[:, None] * inv_freq[None, :]
    cos, sin = jnp.cos(ang), jnp.sin(ang)
    r = np.arange(LANES) % HEAD_DIM
    f = r % ROT_HALF
    first = jnp.asarray(r < ROT_HALF)
    second = jnp.asarray((r >= ROT_HALF) & (r < 2 * ROT_HALF))
    rot = jnp.asarray(r < 2 * ROT_HALF)
    c_tok = jnp.where(rot[None, :], cos[:, f], 1.0)
    s1_tok = jnp.where(first[None, :], -sin[:, f], 0.0)
    s2_tok = jnp.where(second[None, :], sin[:, f], 0.0)
    return (c_tok, s1_tok, s2_tok), (cos.T, sin.T)


def kernel(x, mem, norm_g, w_in, mem_norm_g, w_mem_kv, w_branch, w_out, final_g):
    batch, seq, d = x.shape
    n_mem = mem.shape[1]
    depth = norm_g.shape[0]
    w = BRANCH_WIDTH
    m = batch * seq
    assert seq % MOBA_BLOCK == 0 and d % LANES == 0

    tm = min(1024, seq)
    tm_merge = min(512, seq)
    dsa_tq = 256
    dsa_kc = 256
    dsa_kcb = min(512, seq)

    tok_tabs, feat_tabs = _rope_tables(seq)
    offs = np.cumsum([0, w, w, w, w, w, w, w, w, N_HEADS * HEAD_DIM, HEAD_DIM, N_HEADS, w, w, 3 * d])
    (o_aq, o_ak, o_av, o_ag, o_bq, o_bk, o_bv, o_bg, o_iq, o_ik, o_iw, o_mq, o_mg, o_mix, o_end) = offs
    attn_scale = HEAD_DIM ** -0.5 * float(np.log2(np.e))
    idx_scale = HEAD_DIM ** -0.5

    x2d = x.reshape(m, d)
    mem2d = mem.reshape(batch * n_mem, d)
    for l in range(depth):
        wl = w_in[l]
        col = lambda a, b: wl[:, a:b]
        w_k = jnp.concatenate([col(o_ak, o_av), col(o_bk, o_bv), col(o_ik, o_iw), col(o_ik, o_iw)],
                              axis=1).astype(BF16)
        w_qt = jnp.concatenate([col(o_aq, o_ak) * attn_scale, col(o_bq, o_bk) * attn_scale,
                                col(o_iq, o_ik) * idx_scale], axis=1).T.astype(BF16)
        w_vt = jnp.concatenate([col(o_av, o_ag), col(o_bv, o_bg)], axis=1).T.astype(BF16)
        w_iwt = jnp.pad(col(o_iw, o_mq).T, ((0, 2 * SUBLANES - N_HEADS), (0, 0))).astype(BF16)
        w_mq = col(o_mq, o_mg).astype(BF16)
        w_g = jnp.concatenate([col(o_ag, o_bq), col(o_bg, o_iq), col(o_mg, o_mix)], axis=1).astype(BF16)
        w_mix = col(o_mix, o_end).astype(BF16)

        h = _rmsnorm(x2d, norm_g[l], BF16, tm)
        k_all = _mm_tok(h, w_k, "rope", BF16, tm, w_k.shape[1], tok_tabs, seq)
        qt_all = _mm_feat(w_qt, h, "rope", BF16, tm, w, feat_tabs, seq)
        vt_all = _mm_feat(w_vt, h, "none", BF16, tm, w)
        iwt = _mm_feat(w_iwt, h, "none", F32, tm, 2 * SUBLANES)
        mq = _mm_tok(h, w_mq, "none", BF16, tm, w)
        gates = _mm_tok(h, w_g, "silu", F32, tm, w)
        mix = _mm_tok(h, w_mix, "sigmoid", F32, tm, 2 * w)

        ya = _moba(qt_all, k_all, vt_all, batch, seq, 0, 0, 0)
        yb = _dsa(qt_all, k_all, vt_all, iwt, batch, seq, dsa_tq, dsa_kc, dsa_kcb)

        mn = _rmsnorm(mem2d, mem_norm_g[l], BF16, n_mem)
        kv = _mm_tok(mn, w_mem_kv[l].astype(BF16), "none", BF16, n_mem, 2 * w)

        x2d = _merge(ya, yb, mq, kv, gates, mix, w_branch[l].astype(BF16), w_out[l].astype(BF16),
                     x2d, seq, n_mem, tm_merge)

    out = _rmsnorm(x2d, final_g, F32, tm)
    return out.reshape(batch, seq, d)
```

```python
import functools

import jax
import jax.numpy as jnp
import numpy as np
from jax import lax
from jax.experimental import pallas as pl
from jax.experimental.pallas import tpu as pltpu

F32 = jnp.float32
BF16 = jnp.bfloat16

HEAD_DIM = 64
N_HEADS = 8
BRANCH_WIDTH = 512
M_HEADS = 4
M_HEAD_DIM = 128
MOBA_BLOCK = 256
MOBA_TOPK = 3
IDX_TOPK_MAX = 256
ROPE_THETA = 500000.0
ROT_HALF = HEAD_DIM // 4 // 2
RMS_EPS = 1e-6

LANES = 128
SUBLANES = 8
VMEM_LIMIT = 56 * 1024 * 1024
NEG_BIG = -1e30
INT_MIN = -(2 ** 31)
I16_MIN = -(2 ** 15)
KEY_NEG_INF = -0x7F800000


def _cparams(n_axes):
    return pltpu.CompilerParams(dimension_semantics=("arbitrary",) * n_axes,
                                vmem_limit_bytes=VMEM_LIMIT)


def _rmsnorm_kernel(x_ref, g_ref, o_ref):
    xf = x_ref[...]
    y = xf * lax.rsqrt(jnp.mean(xf * xf, axis=-1, keepdims=True) + RMS_EPS)
    o_ref[...] = (y * g_ref[...]).astype(o_ref.dtype)


def _rmsnorm(x2d, g, out_dtype, tm):
    m, d = x2d.shape
    return pl.pallas_call(
        _rmsnorm_kernel,
        grid=(m // tm,),
        in_specs=[pl.BlockSpec((tm, d), lambda i: (i, 0)),
                  pl.BlockSpec((1, d), lambda i: (0, 0))],
        out_specs=pl.BlockSpec((tm, d), lambda i: (i, 0)),
        out_shape=jax.ShapeDtypeStruct((m, d), out_dtype),
        compiler_params=_cparams(1),
        name="rmsnorm",
    )(x2d, g.reshape(1, d))


def _mm_tok_kernel(h_ref, w_ref, *rest, epilogue):
    o_ref = rest[-1]
    acc = jnp.dot(h_ref[...], w_ref[...], preferred_element_type=F32)
    if epilogue == "silu":
        acc = acc * jax.nn.sigmoid(acc)
    elif epilogue == "sigmoid":
        acc = jax.nn.sigmoid(acc)
    if epilogue == "rope":
        c_ref, s1_ref, s2_ref = rest[:3]
        c, s1, s2 = c_ref[...], s1_ref[...], s2_ref[...]
        for j in range(acc.shape[1] // LANES):
            piece = acc[:, j * LANES:(j + 1) * LANES]
            up = pltpu.roll(piece, LANES - ROT_HALF, 1)
            down = pltpu.roll(piece, ROT_HALF, 1)
            o_ref[:, j * LANES:(j + 1) * LANES] = (piece * c + up * s1 + down * s2).astype(o_ref.dtype)
    else:
        o_ref[...] = acc.astype(o_ref.dtype)


def _mm_tok(h, w, epilogue, out_dtype, tm, tn, rope_tabs=None, seq=None):
    m, k = h.shape
    n = w.shape[1]
    in_specs = [pl.BlockSpec((tm, k), lambda i, j: (i, 0)),
                pl.BlockSpec((k, tn), lambda i, j: (0, j))]
    args = [h, w]
    if epilogue == "rope":
        nt = seq // tm
        for t in rope_tabs:
            in_specs.append(pl.BlockSpec((tm, LANES), lambda i, j: (i % nt, 0)))
            args.append(t)
    return pl.pallas_call(
        functools.partial(_mm_tok_kernel, epilogue=epilogue),
        grid=(m // tm, n // tn),
        in_specs=in_specs,
        out_specs=pl.BlockSpec((tm, tn), lambda i, j: (i, j)),
        out_shape=jax.ShapeDtypeStruct((m, n), out_dtype),
        compiler_params=_cparams(2),
        name="proj_tok_" + epilogue,
    )(*args)


def _mm_feat_kernel(wt_ref, h_ref, *rest, epilogue):
    o_ref = rest[-1]
    acc = lax.dot_general(wt_ref[...], h_ref[...], (((1,), (1,)), ((), ())),
                          preferred_element_type=F32)
    if epilogue == "rope":
        cos_ref, sin_ref = rest[:2]
        c, s = cos_ref[...], sin_ref[...]
        pieces = []
        for hh in range(acc.shape[0] // HEAD_DIM):
            base = hh * HEAD_DIM
            x1 = acc[base:base + ROT_HALF]
            x2 = acc[base + ROT_HALF:base + 2 * ROT_HALF]
            pieces += [x1 * c - x2 * s, x2 * c + x1 * s, acc[base + 2 * ROT_HALF:base + HEAD_DIM]]
        acc = jnp.concatenate(pieces, axis=0)
    o_ref[...] = acc.astype(o_ref.dtype)


def _mm_feat(wt, h, epilogue, out_dtype, tm, tn, rope_tabs=None, seq=None):
    n, k = wt.shape
    m = h.shape[0]
    in_specs = [pl.BlockSpec((tn, k), lambda i, j: (j, 0)),
                pl.BlockSpec((tm, k), lambda i, j: (i, 0))]
    args = [wt, h]
    if epilogue == "rope":
        nt = seq // tm
        for t in rope_tabs:
            in_specs.append(pl.BlockSpec((SUBLANES, tm), lambda i, j: (0, i % nt)))
            args.append(t)
    return pl.pallas_call(
        functools.partial(_mm_feat_kernel, epilogue=epilogue),
        grid=(m // tm, n // tn),
        in_specs=in_specs,
        out_specs=pl.BlockSpec((tn, tm), lambda i, j: (j, i)),
        out_shape=jax.ShapeDtypeStruct((n, m), out_dtype),
        compiler_params=_cparams(2),
        name="proj_feat_" + epilogue,
    )(*args)


N_PAIRS = N_HEADS // 2


def _pair_operand(qt_ref, pair):
    qp = qt_ref[pair * LANES:(pair + 1) * LANES, :].astype(F32)
    row = lax.broadcasted_iota(jnp.int32, qp.shape, 0)
    lo = jnp.where(row < HEAD_DIM, qp, 0.0)
    hi = jnp.where(row >= HEAD_DIM, qp, 0.0)
    return jnp.concatenate([lo, hi], axis=1).astype(BF16)


ONES_ROWS = 16
QK_LOOKAHEAD = 4


def _flash_init(m_ref, acc_ref):
    m_ref[...] = jnp.full(m_ref.shape, NEG_BIG, F32)
    acc_ref[...] = jnp.zeros(acc_ref.shape, F32)


def _flash_update(s, vt_tile, m_ref, acc_ref, p):
    keys = vt_tile.shape[1]
    m = m_ref[p]
    m_new = jnp.maximum(m, jnp.max(s, axis=0, keepdims=True))
    alpha = jnp.exp2(m - m_new)
    pm = jnp.exp2(s - m_new).astype(BF16)
    vt_ext = jnp.concatenate([vt_tile, jnp.ones((ONES_ROWS, keys), BF16)], axis=0)
    acc_ref[p] = alpha * acc_ref[p] + jnp.dot(vt_ext, pm, preferred_element_type=F32)
    m_ref[p] = m_new


def _flash_tiles(tiles, qk, mask, vt_tile, m_ref, acc_ref):
    ss = [qk(*t) for t in tiles[:QK_LOOKAHEAD]]
    for n, (u, p) in enumerate(tiles):
        if n + QK_LOOKAHEAD < len(tiles):
            ss.append(qk(*tiles[n + QK_LOOKAHEAD]))
        _flash_update(mask(ss[n], u, p), vt_tile(u, p), m_ref, acc_ref, p)
        ss[n] = None


def _flash_finish(o_ref, ot_ref, acc_ref, tq):
    for p in range(N_PAIRS):
        o = acc_ref[p, :LANES] / acc_ref[p, LANES:LANES + 1]
        ot_ref[p * LANES:p * LANES + HEAD_DIM, :] = o[:HEAD_DIM, :tq]
        ot_ref[p * LANES + HEAD_DIM:(p + 1) * LANES, :] = o[HEAD_DIM:, tq:]
    o_ref[...] = ot_ref[...].T


MOBA_UNROLL = 2
DSA_UNROLL = 2
SCORE_UNROLL = 2


def _moba_kernel(qt_ref, k_ref, vt_ref, o_ref, kmean_ref, qop_ref, sel_ref, m_ref, acc_ref,
                 ot_ref, *, seq, topk):
    i = pl.program_id(1)
    blk = MOBA_BLOCK
    nb = seq // blk
    tq = blk

    @pl.when(i == 0)
    def _():
        r = lax.broadcasted_iota(jnp.int32, (nb, seq), 0)
        c = lax.broadcasted_iota(jnp.int32, (nb, seq), 1)
        member = jnp.where((c >= r * blk) & (c < (r + 1) * blk), 1.0, 0.0).astype(BF16)
        ksum = jnp.dot(member, k_ref[...], preferred_element_type=F32)
        kmean_ref[...] = (ksum * (1.0 / blk)).astype(BF16)

    blk_id = lax.broadcasted_iota(jnp.int32, (nb, 2 * tq), 0)
    past = blk_id < i
    own_start = pl.multiple_of(i * blk, blk)

    for p in range(N_PAIRS):
        lanes = slice(p * LANES, (p + 1) * LANES)
        qop = _pair_operand(qt_ref, p)
        qop_ref[p] = qop
        gate = jnp.dot(kmean_ref[:, lanes], qop, preferred_element_type=F32)
        gate = jnp.where(past, gate, -jnp.inf)
        sel = jnp.zeros((nb, 2 * tq), jnp.bool_)
        for _ in range(topk):
            mx = jnp.max(gate, axis=0, keepdims=True)
            first = jnp.min(jnp.where(gate == mx, blk_id, nb), axis=0, keepdims=True)
            hit = blk_id == first
            sel = sel | hit
            gate = jnp.where(hit, -jnp.inf, gate)
        sel_ref[p] = jnp.where(sel & past, 1.0, 0.0)

    _flash_init(m_ref, acc_ref)

    def lanes(p):
        return slice(p * LANES, (p + 1) * LANES)

    krow = lax.broadcasted_iota(jnp.int32, (blk, 2 * tq), 0)
    qcol = lax.broadcasted_iota(jnp.int32, (blk, 2 * tq), 1)
    causal = krow <= jnp.where(qcol < tq, qcol, qcol - tq)
    _flash_tiles(
        [(0, p) for p in range(N_PAIRS)],
        lambda u, p: jnp.dot(k_ref[pl.ds(own_start, blk), lanes(p)], qop_ref[p], preferred_element_type=F32),
        lambda s, u, p: jnp.where(causal, s, -jnp.inf),
        lambda u, p: vt_ref[lanes(p), pl.ds(own_start, blk)],
        m_ref, acc_ref)

    def body(c, carry):
        js = [c * MOBA_UNROLL + u for u in range(MOBA_UNROLL)]
        starts = [pl.multiple_of(j * blk, blk) for j in js]
        _flash_tiles(
            [(u, p) for u in range(MOBA_UNROLL) for p in range(N_PAIRS)],
            lambda u, p: jnp.dot(k_ref[pl.ds(starts[u], blk), lanes(p)], qop_ref[p],
                                 preferred_element_type=F32),
            lambda s, u, p: jnp.where(sel_ref[p, pl.ds(js[u], 1), :] > 0.5, s, -jnp.inf),
            lambda u, p: vt_ref[lanes(p), pl.ds(starts[u], blk)],
            m_ref, acc_ref)
        return carry

    lax.fori_loop(0, (i + MOBA_UNROLL - 1) // MOBA_UNROLL, body, 0)
    _flash_finish(o_ref, ot_ref, acc_ref, tq)


def _moba(qt_all, k_all, vt_all, batch, seq, q_row_blk, k_col_blk, v_row_blk):
    tq = MOBA_BLOCK
    nq = seq // tq
    nb = seq // MOBA_BLOCK
    topk = min(MOBA_TOPK, nb - 1)
    w = BRANCH_WIDTH
    return pl.pallas_call(
        functools.partial(_moba_kernel, seq=seq, topk=topk),
        grid=(batch, nq),
        in_specs=[
            pl.BlockSpec((w, tq), lambda b, i: (q_row_blk, b * nq + i)),
            pl.BlockSpec((seq, w), lambda b, i: (b, k_col_blk), pipeline_mode=pl.Buffered(1)),
            pl.BlockSpec((w, seq), lambda b, i: (v_row_blk, b), pipeline_mode=pl.Buffered(1)),
        ],
        out_specs=pl.BlockSpec((tq, w), lambda b, i: (b * nq + i, 0)),
        out_shape=jax.ShapeDtypeStruct((batch * seq, w), F32),
        scratch_shapes=[pltpu.VMEM((nb, w), BF16),
                        pltpu.VMEM((N_PAIRS, LANES, 2 * tq), BF16),
                        pltpu.VMEM((N_PAIRS, nb, 2 * tq), F32),
                        pltpu.VMEM((N_PAIRS, 1, 2 * tq), F32),
                        pltpu.VMEM((N_PAIRS, LANES + ONES_ROWS, 2 * tq), F32),
                        pltpu.VMEM((w, tq), F32)],
        compiler_params=_cparams(2),
        name="moba",
    )(qt_all, k_all, vt_all)


def _ordered_key(bits):
    return jnp.where(bits < 0, jnp.int32(INT_MIN) - bits, bits)


def _tree_sum(parts):
    while len(parts) > 1:
        parts = [parts[a] + parts[a + 1] for a in range(0, len(parts) - 1, 2)] + (
            [parts[-1]] if len(parts) % 2 else [])
    return parts[0]


def _dsa_kernel(qit_ref, ki_ref, wt_ref, qt_ref, k_ref, vt_ref, o_ref,
                score_ref, hi_ref, lo_ref, bias_ref, qop_ref, m_ref, acc_ref, ot_ref, *, tq, kc, kcb, n_sel):
    i = pl.program_id(1)
    nch = (i * tq + tq) // kc
    nchb = (i * tq + tq + kcb - 1) // kcb
    ntrip = (nch + DSA_UNROLL - 1) // DSA_UNROLL
    krow = lax.broadcasted_iota(jnp.int32, (kc, tq), 0)
    qpos = i * tq + lax.broadcasted_iota(jnp.int32, (kc, tq), 1)
    n_sel_f = float(n_sel)

    for p in range(N_PAIRS):
        qop_ref[p] = _pair_operand(qit_ref, p)
    w_rows = [wt_ref[h:h + 1, :] * (N_HEADS ** -0.5) for h in range(N_HEADS)]

    def score_body(c, carry, masked, unroll):
        starts = [pl.multiple_of((c * unroll + u) * kc, kc) for u in range(unroll)]
        rs = [[jnp.dot(ki_ref[pl.ds(st, kc), :], qop_ref[p], preferred_element_type=F32)
               for p in range(N_PAIRS)] for st in starts]
        for u, start in enumerate(starts):
            acc = jnp.zeros((kc, tq), F32)
            for p in range(N_PAIRS):
                acc = acc + jnp.maximum(rs[u][p][:, :tq], 0.0) * w_rows[2 * p]
                acc = acc + jnp.maximum(rs[u][p][:, tq:], 0.0) * w_rows[2 * p + 1]
            sc = jnp.where(start + krow <= qpos, acc, -jnp.inf) if masked else acc
            score_ref[pl.ds(start, kc), :] = sc
            key = _ordered_key(lax.bitcast_convert_type(sc, jnp.int32))
            hi_ref[pl.ds(start, kc), :] = jnp.right_shift(key, 16).astype(jnp.int16)
            lo_ref[pl.ds(start, kc), :] = (key ^ jnp.int32(0x8000)).astype(jnp.int16)
        return carry

    n_fast = (i * tq + 1) // (kc * SCORE_UNROLL)
    lax.fori_loop(0, n_fast, functools.partial(score_body, masked=False, unroll=SCORE_UNROLL), 0)
    lax.fori_loop(n_fast * SCORE_UNROLL, nch, functools.partial(score_body, masked=True, unroll=1), 0)

    @pl.when(nch * kc < nchb * kcb)
    def _():
        pad_rows = pl.ds(pl.multiple_of(nch * kc, kc), kcb - kc)
        score_ref[pad_rows, :] = jnp.full((kcb - kc, tq), -jnp.inf, F32)
        hi_ref[pad_rows, :] = jnp.full((kcb - kc, tq), I16_MIN, jnp.int16)
        lo_ref[pad_rows, :] = jnp.full((kcb - kc, tq), I16_MIN, jnp.int16)

    def count16(ref, pred):
        n_acc = 4
        rows = 2 * SUBLANES

        def body(c, accs):
            x = ref[pl.ds(pl.multiple_of(c * kcb, kcb), kcb), :]
            accs = list(accs)
            for r in range(kcb // rows):
                hit = jnp.where(pred(x[r * rows:(r + 1) * rows]), jnp.int16(1), jnp.int16(0))
                accs[r % n_acc] = accs[r % n_acc] + hit
            return tuple(accs)
        accs = lax.fori_loop(0, nchb, body, tuple(jnp.zeros((rows, tq), jnp.int16) for _ in range(n_acc)))
        tot = _tree_sum(list(accs)).astype(jnp.int32).astype(F32)
        return jnp.sum(tot, axis=0, keepdims=True)

    def bisect16(ref, target, count_at_min):
        def body(it, carry):
            t, c_ge = carry
            cand = t + jnp.left_shift(jnp.int32(1), 15 - it)
            cand16 = cand.astype(jnp.int16)
            cnt = count16(ref, lambda x: x >= cand16)
            ok = cnt >= target
            return jnp.where(ok, cand, t), jnp.where(ok, cnt, c_ge)
        return lax.fori_loop(0, 16, body, (jnp.full((1, tq), I16_MIN, jnp.int32), count_at_min))

    rows_seen = jnp.full((1, tq), (nchb * kcb).astype(F32))
    t_hi, c_ge_hi = bisect16(hi_ref, n_sel_f, rows_seen)
    t_hi16 = t_hi.astype(jnp.int16)
    c_gt_hi = count16(hi_ref, lambda x: x > t_hi16)

    def low_body(c, carry):
        rows = pl.ds(pl.multiple_of(c * kcb, kcb), kcb)
        lo_ref[rows, :] = jnp.where(hi_ref[rows, :] == t_hi16, lo_ref[rows, :], jnp.int16(I16_MIN))
        return carry

    lax.fori_loop(0, nchb, low_body, 0)
    t_lo, c_ge_lo = bisect16(lo_ref, n_sel_f - c_gt_hi, c_ge_hi - c_gt_hi)
    n_ge = c_gt_hi + c_ge_lo
    key_t = t_hi * 65536 + (t_lo + 32768)
    short = key_t <= KEY_NEG_INF
    thr = jnp.where(short, -jnp.inf, lax.bitcast_convert_type(_ordered_key(key_t), F32))
    has_tie = jnp.max(jnp.where(short, 0.0, n_ge - n_sel_f)) > 0.0

    @pl.when(jnp.logical_not(has_tie))
    def _():
        def body(c, carry):
            start = pl.multiple_of(c * kc, kc)
            sc = score_ref[pl.ds(start, kc), :]
            keep = (sc >= thr) & (start + krow <= qpos)
            bias_ref[pl.ds(start, kc), :] = jnp.where(keep, 0.0, NEG_BIG)
            return carry
        lax.fori_loop(0, nch, body, 0)

    @pl.when(has_tie)
    def _():
        lower = (lax.broadcasted_iota(jnp.int32, (kc, kc), 1)
                 < lax.broadcasted_iota(jnp.int32, (kc, kc), 0))
        lower = jnp.where(lower, 1.0, 0.0).astype(BF16)

        def gt_body(c, cnt):
            sc = score_ref[pl.ds(pl.multiple_of(c * kc, kc), kc), :]
            return cnt + jnp.sum(jnp.where(sc > thr, 1.0, 0.0), axis=0, keepdims=True)
        n_gt = lax.fori_loop(0, nch, gt_body, jnp.zeros((1, tq), F32))
        need = n_sel_f - n_gt

        def body(c, seen):
            start = pl.multiple_of(c * kc, kc)
            sc = score_ref[pl.ds(start, kc), :]
            eq = jnp.where(sc == thr, 1.0, 0.0)
            rank = seen + jnp.dot(lower, eq.astype(BF16), preferred_element_type=F32)
            keep = (sc > thr) | ((sc == thr) & (rank < need))
            keep = keep & (start + krow <= qpos)
            bias_ref[pl.ds(start, kc), :] = jnp.where(keep, 0.0, NEG_BIG)
            return seen + jnp.sum(eq, axis=0, keepdims=True)
        lax.fori_loop(0, nch, body, jnp.zeros((1, tq), F32))

    for p in range(N_PAIRS):
        qop_ref[p] = _pair_operand(qt_ref, p)
    _flash_init(m_ref, acc_ref)

    @pl.when(nch < ntrip * DSA_UNROLL)
    def _():
        pad = (DSA_UNROLL - 1) * kc
        bias_ref[pl.ds(pl.multiple_of(nch * kc, kc), pad), :] = jnp.full((pad, tq), NEG_BIG, F32)

    def lanes(p):
        return slice(p * LANES, (p + 1) * LANES)

    def att_body(c, carry):
        starts = [pl.multiple_of((c * DSA_UNROLL + u) * kc, kc) for u in range(DSA_UNROLL)]

        def add_bias(s, u, p):
            bias = bias_ref[pl.ds(starts[u], kc), :]
            return jnp.concatenate([s[:, :tq] + bias, s[:, tq:] + bias], axis=1)

        _flash_tiles(
            [(u, p) for u in range(DSA_UNROLL) for p in range(N_PAIRS)],
            lambda u, p: jnp.dot(k_ref[pl.ds(starts[u], kc), lanes(p)], qop_ref[p],
                                 preferred_element_type=F32),
            add_bias,
            lambda u, p: vt_ref[lanes(p), pl.ds(starts[u], kc)],
            m_ref, acc_ref)
        return carry

    lax.fori_loop(0, ntrip, att_body, 0)
    _flash_finish(o_ref, ot_ref, acc_ref, tq)


def _dsa(qt_all, k_all, vt_all, wt, batch, seq, tq, kc, kcb):
    nq = seq // tq
    w = BRANCH_WIDTH
    n_sel = min(IDX_TOPK_MAX, seq // 4)
    assert tq % kc == 0 and kcb % kc == 0 and kcb > kc and seq % kcb == 0
    ki_col_blk = (2 * w) // LANES
    return pl.pallas_call(
        functools.partial(_dsa_kernel, tq=tq, kc=kc, kcb=kcb, n_sel=n_sel),
        grid=(batch, nq),
        in_specs=[
            pl.BlockSpec((w, tq), lambda b, i: (2, b * nq + i)),
            pl.BlockSpec((seq, LANES), lambda b, i: (b, ki_col_blk), pipeline_mode=pl.Buffered(1)),
            pl.BlockSpec((2 * SUBLANES, tq), lambda b, i: (0, b * nq + i)),
            pl.BlockSpec((w, tq), lambda b, i: (1, b * nq + i)),
            pl.BlockSpec((seq, w), lambda b, i: (b, 1), pipeline_mode=pl.Buffered(1)),
            pl.BlockSpec((w, seq), lambda b, i: (1, b), pipeline_mode=pl.Buffered(1)),
        ],
        out_specs=pl.BlockSpec((tq, w), lambda b, i: (b * nq + i, 0)),
        out_shape=jax.ShapeDtypeStruct((batch * seq, w), F32),
        scratch_shapes=[pltpu.VMEM((seq, tq), F32),
                        pltpu.VMEM((seq, tq), jnp.int16),
                        pltpu.VMEM((seq, tq), jnp.int16),
                        pltpu.VMEM((seq, tq), F32),
                        pltpu.VMEM((N_PAIRS, LANES, 2 * tq), BF16),
                        pltpu.VMEM((N_PAIRS, 1, 2 * tq), F32),
                        pltpu.VMEM((N_PAIRS, LANES + ONES_ROWS, 2 * tq), F32),
                        pltpu.VMEM((w, tq), F32)],
        compiler_params=_cparams(2),
        name="dsa",
    )(qt_all, k_all, wt, qt_all, k_all, vt_all)


def _merge_kernel(ya_ref, yb_ref, mq_ref, kv_ref, g_ref, mix_ref, wb_ref, wo_ref, x_ref, o_ref):
    w = BRANCH_WIDTH
    d = x_ref.shape[1]
    mq = mq_ref[...]
    ym_heads = []
    for h in range(M_HEADS):
        cols = slice(h * M_HEAD_DIM, (h + 1) * M_HEAD_DIM)
        km = kv_ref[:, cols]
        vm = kv_ref[:, w + h * M_HEAD_DIM:w + (h + 1) * M_HEAD_DIM]
        sc = lax.dot_general(mq[:, cols], km, (((1,), (1,)), ((), ())),
                             preferred_element_type=F32) * (M_HEAD_DIM ** -0.5)
        sc = sc - jnp.max(sc, axis=-1, keepdims=True)
        e = jnp.exp(sc)
        p = e / jnp.sum(e, axis=-1, keepdims=True)
        ym_heads.append(jnp.dot(p.astype(BF16), vm, preferred_element_type=F32))
    ym = jnp.concatenate(ym_heads, axis=-1)
    ys = (ya_ref[...], yb_ref[...], ym)
    merged = None
    for n in range(3):
        y = (ys[n] * g_ref[:, n * w:(n + 1) * w]).astype(BF16)
        up = jnp.dot(y, wb_ref[n], preferred_element_type=F32)
        term = mix_ref[:, n * d:(n + 1) * d] * up
        merged = term if merged is None else merged + term
    o_ref[...] = x_ref[...] + jnp.dot(merged.astype(BF16), wo_ref[...], preferred_element_type=F32)


def _merge(ya, yb, mq, kv, gates, mix, wb, wo, x2d, seq, n_mem, tm):
    m, d = x2d.shape
    w = BRANCH_WIDTH
    nt = seq // tm
    return pl.pallas_call(
        _merge_kernel,
        grid=(m // tm,),
        in_specs=[
            pl.BlockSpec((tm, w), lambda i: (i, 0)),
            pl.BlockSpec((tm, w), lambda i: (i, 0)),
            pl.BlockSpec((tm, w), lambda i: (i, 0)),
            pl.BlockSpec((n_mem, 2 * w), lambda i: (i // nt, 0)),
            pl.BlockSpec((tm, 3 * w), lambda i: (i, 0)),
            pl.BlockSpec((tm, 3 * d), lambda i: (i, 0)),
            pl.BlockSpec((3, w, d), lambda i: (0, 0, 0)),
            pl.BlockSpec((d, d), lambda i: (0, 0)),
            pl.BlockSpec((tm, d), lambda i: (i, 0)),
        ],
        out_specs=pl.BlockSpec((tm, d), lambda i: (i, 0)),
        out_shape=jax.ShapeDtypeStruct((m, d), F32),
        compiler_params=_cparams(1),
        name="merge",
    )(ya, yb, mq, kv, gates, mix, wb, wo, x2d)


def _rope_tables(seq):
    inv_freq = ROPE_THETA ** (-jnp.arange(ROT_HALF, dtype=F32) / ROT_HALF)
    ang = jnp.arange(seq, dtype=jnp.int32).astype(F32)[:, None] * inv_freq[None, :]
    cos, sin = jnp.cos(ang), jnp.sin(ang)
    r = np.arange(LANES) % HEAD_DIM
    f = r % ROT_HALF
    first = jnp.asarray(r < ROT_HALF)
    second = jnp.asarray((r >= ROT_HALF) & (r < 2 * ROT_HALF))
    rot = jnp.asarray(r < 2 * ROT_HALF)
    c_tok = jnp.where(rot[None, :], cos[:, f], 1.0)
    s1_tok = jnp.where(first[None, :], -sin[:, f], 0.0)
    s2_tok = jnp.where(second[None, :], sin[:, f], 0.0)
    return (c_tok, s1_tok, s2_tok), (cos.T, sin.T)


def kernel(x, mem, norm_g, w_in, mem_norm_g, w_mem_kv, w_branch, w_out, final_g):
    batch, seq, d = x.shape
    n_mem = mem.shape[1]
    depth = norm_g.shape[0]
    w = BRANCH_WIDTH
    m = batch * seq
    assert seq % MOBA_BLOCK == 0 and d % LANES == 0

    tm = min(1024, seq)
    tm_merge = min(512, seq)
    dsa_tq = 256
    dsa_kc = 256
    dsa_kcb = min(512, seq)

    tok_tabs, feat_tabs = _rope_tables(seq)
    offs = np.cumsum([0, w, w, w, w, w, w, w, w, N_HEADS * HEAD_DIM, HEAD_DIM, N_HEADS, w, w, 3 * d])
    (o_aq, o_ak, o_av, o_ag, o_bq, o_bk, o_bv, o_bg, o_iq, o_ik, o_iw, o_mq, o_mg, o_mix, o_end) = offs
    attn_scale = HEAD_DIM ** -0.5 * float(np.log2(np.e))
    idx_scale = HEAD_DIM ** -0.5

    x2d = x.reshape(m, d)
    mem2d = mem.reshape(batch * n_mem, d)
    for l in range(depth):
        wl = w_in[l]
        col = lambda a, b: wl[:, a:b]
        w_k = jnp.concatenate([col(o_ak, o_av), col(o_bk, o_bv), col(o_ik, o_iw), col(o_ik, o_iw)],
                              axis=1).astype(BF16)
        w_qt = jnp.concatenate([col(o_aq, o_ak) * attn_scale, col(o_bq, o_bk) * attn_scale,
                                col(o_iq, o_ik) * idx_scale], axis=1).T.astype(BF16)
        w_vt = jnp.concatenate([col(o_av, o_ag), col(o_bv, o_bg)], axis=1).T.astype(BF16)
        w_iwt = jnp.pad(col(o_iw, o_mq).T, ((0, 2 * SUBLANES - N_HEADS), (0, 0))).astype(BF16)
        w_mq = col(o_mq, o_mg).astype(BF16)
        w_g = jnp.concatenate([col(o_ag, o_bq), col(o_bg, o_iq), col(o_mg, o_mix)], axis=1).astype(BF16)
        w_mix = col(o_mix, o_end).astype(BF16)

        h = _rmsnorm(x2d, norm_g[l], BF16, tm)
        k_all = _mm_tok(h, w_k, "rope", BF16, tm, w_k.shape[1], tok_tabs, seq)
        qt_all = _mm_feat(w_qt, h, "rope", BF16, tm, w, feat_tabs, seq)
        vt_all = _mm_feat(w_vt, h, "none", BF16, tm, w)
        iwt = _mm_feat(w_iwt, h, "none", F32, tm, 2 * SUBLANES)
        mq = _mm_tok(h, w_mq, "none", BF16, tm, w)
        gates = _mm_tok(h, w_g, "silu", F32, tm, w)
        mix = _mm_tok(h, w_mix, "sigmoid", F32, tm, 2 * w)

        ya = _moba(qt_all, k_all, vt_all, batch, seq, 0, 0, 0)
        yb = _dsa(qt_all, k_all, vt_all, iwt, batch, seq, dsa_tq, dsa_kc, dsa_kcb)

        mn = _rmsnorm(mem2d, mem_norm_g[l], BF16, n_mem)
        kv = _mm_tok(mn, w_mem_kv[l].astype(BF16), "none", BF16, n_mem, 2 * w)

        x2d = _merge(ya, yb, mq, kv, gates, mix, w_branch[l].astype(BF16), w_out[l].astype(BF16),
                     x2d, seq, n_mem, tm_merge)

    out = _rmsnorm(x2d, final_g, F32, tm)
    return out.reshape(batch, seq, d)
```

```python
import functools

import jax
import jax.numpy as jnp
import numpy as np
from jax import lax
from jax.experimental import pallas as pl
from jax.experimental.pallas import tpu as pltpu

F32 = jnp.float32
BF16 = jnp.bfloat16

HEAD_DIM = 64
N_HEADS = 8
BRANCH_WIDTH = 512
M_HEADS = 4
M_HEAD_DIM = 128
MOBA_BLOCK = 256
MOBA_TOPK = 3
IDX_TOPK_MAX = 256
ROPE_THETA = 500000.0
ROT_HALF = HEAD_DIM // 4 // 2
RMS_EPS = 1e-6

LANES = 128
SUBLANES = 8
VMEM_LIMIT = 56 * 1024 * 1024
NEG_BIG = -1e30
INT_MIN = -(2 ** 31)
I16_MIN = -(2 ** 15)
KEY_NEG_INF = -0x7F800000


def _cparams(n_axes):
    return pltpu.CompilerParams(dimension_semantics=("arbitrary",) * n_axes,
                                vmem_limit_bytes=VMEM_LIMIT)


def _rmsnorm_kernel(x_ref, g_ref, o_ref):
    xf = x_ref[...]
    y = xf * lax.rsqrt(jnp.mean(xf * xf, axis=-1, keepdims=True) + RMS_EPS)
    o_ref[...] = (y * g_ref[...]).astype(o_ref.dtype)


def _rmsnorm(x2d, g, out_dtype, tm):
    m, d = x2d.shape
    return pl.pallas_call(
        _rmsnorm_kernel,
        grid=(m // tm,),
        in_specs=[pl.BlockSpec((tm, d), lambda i: (i, 0)),
                  pl.BlockSpec((1, d), lambda i: (0, 0))],
        out_specs=pl.BlockSpec((tm, d), lambda i: (i, 0)),
        out_shape=jax.ShapeDtypeStruct((m, d), out_dtype),
        compiler_params=_cparams(1),
        name="rmsnorm",
    )(x2d, g.reshape(1, d))


def _mm_tok_kernel(h_ref, w_ref, *rest, epilogue):
    o_ref = rest[-1]
    acc = jnp.dot(h_ref[...], w_ref[...], preferred_element_type=F32)
    if epilogue == "rope":
        c_ref, s1_ref, s2_ref = rest[:3]
        c, s1, s2 = c_ref[...], s1_ref[...], s2_ref[...]
        for j in range(acc.shape[1] // LANES):
            piece = acc[:, j * LANES:(j + 1) * LANES]
            up = pltpu.roll(piece, LANES - ROT_HALF, 1)
            down = pltpu.roll(piece, ROT_HALF, 1)
            o_ref[:, j * LANES:(j + 1) * LANES] = (piece * c + up * s1 + down * s2).astype(o_ref.dtype)
    else:
        o_ref[...] = acc.astype(o_ref.dtype)


def _mm_tok(h, w, epilogue, out_dtype, tm, tn, rope_tabs=None, seq=None):
    m, k = h.shape
    n = w.shape[1]
    in_specs = [pl.BlockSpec((tm, k), lambda i, j: (i, 0)),
                pl.BlockSpec((k, tn), lambda i, j: (0, j))]
    args = [h, w]
    if epilogue == "rope":
        nt = seq // tm
        for t in rope_tabs:
            in_specs.append(pl.BlockSpec((tm, LANES), lambda i, j: (i % nt, 0)))
            args.append(t)
    return pl.pallas_call(
        functools.partial(_mm_tok_kernel, epilogue=epilogue),
        grid=(m // tm, n // tn),
        in_specs=in_specs,
        out_specs=pl.BlockSpec((tm, tn), lambda i, j: (i, j)),
        out_shape=jax.ShapeDtypeStruct((m, n), out_dtype),
        compiler_params=_cparams(2),
        name="proj_tok_" + epilogue,
    )(*args)


def _mm_feat_kernel(wt_ref, h_ref, *rest, epilogue):
    o_ref = rest[-1]
    acc = lax.dot_general(wt_ref[...], h_ref[...], (((1,), (1,)), ((), ())),
                          preferred_element_type=F32)
    if epilogue == "rope":
        cos_ref, sin_ref = rest[:2]
        c, s = cos_ref[...], sin_ref[...]
        pieces = []
        for hh in range(acc.shape[0] // HEAD_DIM):
            base = hh * HEAD_DIM
            x1 = acc[base:base + ROT_HALF]
            x2 = acc[base + ROT_HALF:base + 2 * ROT_HALF]
            pieces += [x1 * c - x2 * s, x2 * c + x1 * s, acc[base + 2 * ROT_HALF:base + HEAD_DIM]]
        acc = jnp.concatenate(pieces, axis=0)
    o_ref[...] = acc.astype(o_ref.dtype)


def _mm_feat(wt, h, epilogue, out_dtype, tm, tn, rope_tabs=None, seq=None):
    n, k = wt.shape
    m = h.shape[0]
    in_specs = [pl.BlockSpec((tn, k), lambda i, j: (j, 0)),
                pl.BlockSpec((tm, k), lambda i, j: (i, 0))]
    args = [wt, h]
    if epilogue == "rope":
        nt = seq // tm
        for t in rope_tabs:
            in_specs.append(pl.BlockSpec((SUBLANES, tm), lambda i, j: (0, i % nt)))
            args.append(t)
    return pl.pallas_call(
        functools.partial(_mm_feat_kernel, epilogue=epilogue),
        grid=(m // tm, n // tn),
        in_specs=in_specs,
        out_specs=pl.BlockSpec((tn, tm), lambda i, j: (j, i)),
        out_shape=jax.ShapeDtypeStruct((n, m), out_dtype),
        compiler_params=_cparams(2),
        name="proj_feat_" + epilogue,
    )(*args)


N_PAIRS = N_HEADS // 2


def _pair_operand(qt_ref, pair):
    qp = qt_ref[pair * LANES:(pair + 1) * LANES, :].astype(F32)
    row = lax.broadcasted_iota(jnp.int32, qp.shape, 0)
    lo = jnp.where(row < HEAD_DIM, qp, 0.0)
    hi = jnp.where(row >= HEAD_DIM, qp, 0.0)
    return jnp.concatenate([lo, hi], axis=1).astype(BF16)


ONES_ROWS = 16
QK_LOOKAHEAD = 4


def _flash_init(m_ref, acc_ref):
    m_ref[...] = jnp.full(m_ref.shape, NEG_BIG, F32)
    acc_ref[...] = jnp.zeros(acc_ref.shape, F32)


def _flash_update(s, vt_tile, m_ref, acc_ref, p):
    keys = vt_tile.shape[1]
    tq = s.shape[1] // 2
    m = m_ref[p]
    m_new = jnp.maximum(m, jnp.max(s, axis=0, keepdims=True))
    alpha = jnp.exp2(m - m_new)
    pm = jnp.exp2(s - m_new).astype(BF16)
    ones = jnp.ones((ONES_ROWS, keys), BF16)
    for e in range(2):
        vt_ext = jnp.concatenate([vt_tile[e * HEAD_DIM:(e + 1) * HEAD_DIM], ones], axis=0)
        cols = slice(e * tq, (e + 1) * tq)
        acc_ref[p, e] = alpha[:, cols] * acc_ref[p, e] + jnp.dot(vt_ext, pm[:, cols],
                                                                preferred_element_type=F32)
    m_ref[p] = m_new


def _flash_tiles(tiles, qk, mask, vt_tile, m_ref, acc_ref):
    ss = [qk(*t) for t in tiles[:QK_LOOKAHEAD]]
    for n, (u, p) in enumerate(tiles):
        if n + QK_LOOKAHEAD < len(tiles):
            ss.append(qk(*tiles[n + QK_LOOKAHEAD]))
        _flash_update(mask(ss[n], u, p), vt_tile(u, p), m_ref, acc_ref, p)
        ss[n] = None


def _flash_finish(o_ref, ot_ref, acc_ref, tq):
    for p in range(N_PAIRS):
        for e in range(2):
            lo = p * LANES + e * HEAD_DIM
            ot_ref[lo:lo + HEAD_DIM, :] = acc_ref[p, e, :HEAD_DIM] / acc_ref[p, e, HEAD_DIM:HEAD_DIM + 1]
    o_ref[...] = ot_ref[...].T


MOBA_UNROLL = 2
DSA_UNROLL = 2
SCORE_UNROLL = 2


def _moba_kernel(qt_ref, k_ref, vt_ref, o_ref, kmean_ref, qop_ref, sel_ref, m_ref, acc_ref,
                 ot_ref, *, seq, topk):
    i = pl.program_id(1)
    blk = MOBA_BLOCK
    nb = seq // blk
    tq = blk

    @pl.when(i == 0)
    def _():
        r = lax.broadcasted_iota(jnp.int32, (nb, seq), 0)
        c = lax.broadcasted_iota(jnp.int32, (nb, seq), 1)
        member = jnp.where((c >= r * blk) & (c < (r + 1) * blk), 1.0, 0.0).astype(BF16)
        ksum = jnp.dot(member, k_ref[...], preferred_element_type=F32)
        kmean_ref[...] = (ksum * (1.0 / blk)).astype(BF16)

    blk_id = lax.broadcasted_iota(jnp.int32, (nb, 2 * tq), 0)
    past = blk_id < i
    own_start = pl.multiple_of(i * blk, blk)

    for p in range(N_PAIRS):
        lanes = slice(p * LANES, (p + 1) * LANES)
        qop = _pair_operand(qt_ref, p)
        qop_ref[p] = qop
        gate = jnp.dot(kmean_ref[:, lanes], qop, preferred_element_type=F32)
        gate = jnp.where(past, gate, -jnp.inf)
        sel = jnp.zeros((nb, 2 * tq), jnp.bool_)
        for _ in range(topk):
            mx = jnp.max(gate, axis=0, keepdims=True)
            first = jnp.min(jnp.where(gate == mx, blk_id, nb), axis=0, keepdims=True)
            hit = blk_id == first
            sel = sel | hit
            gate = jnp.where(hit, -jnp.inf, gate)
        sel_ref[p] = jnp.where(sel & past, 1.0, 0.0)

    _flash_init(m_ref, acc_ref)

    def lanes(p):
        return slice(p * LANES, (p + 1) * LANES)

    krow = lax.broadcasted_iota(jnp.int32, (blk, 2 * tq), 0)
    qcol = lax.broadcasted_iota(jnp.int32, (blk, 2 * tq), 1)
    causal = krow <= jnp.where(qcol < tq, qcol, qcol - tq)
    _flash_tiles(
        [(0, p) for p in range(N_PAIRS)],
        lambda u, p: jnp.dot(k_ref[pl.ds(own_start, blk), lanes(p)], qop_ref[p], preferred_element_type=F32),
        lambda s, u, p: jnp.where(causal, s, -jnp.inf),
        lambda u, p: vt_ref[lanes(p), pl.ds(own_start, blk)],
        m_ref, acc_ref)

    def body(c, carry):
        js = [c * MOBA_UNROLL + u for u in range(MOBA_UNROLL)]
        starts = [pl.multiple_of(j * blk, blk) for j in js]
        _flash_tiles(
            [(u, p) for u in range(MOBA_UNROLL) for p in range(N_PAIRS)],
            lambda u, p: jnp.dot(k_ref[pl.ds(starts[u], blk), lanes(p)], qop_ref[p],
                                 preferred_element_type=F32),
            lambda s, u, p: jnp.where(sel_ref[p, pl.ds(js[u], 1), :] > 0.5, s, -jnp.inf),
            lambda u, p: vt_ref[lanes(p), pl.ds(starts[u], blk)],
            m_ref, acc_ref)
        return carry

    lax.fori_loop(0, (i + MOBA_UNROLL - 1) // MOBA_UNROLL, body, 0)
    _flash_finish(o_ref, ot_ref, acc_ref, tq)


def _moba(qt_all, k_all, vt_all, batch, seq, q_row_blk, k_col_blk, v_row_blk):
    tq = MOBA_BLOCK
    nq = seq // tq
    nb = seq // MOBA_BLOCK
    topk = min(MOBA_TOPK, nb - 1)
    w = BRANCH_WIDTH
    return pl.pallas_call(
        functools.partial(_moba_kernel, seq=seq, topk=topk),
        grid=(batch, nq),
        in_specs=[
            pl.BlockSpec((w, tq), lambda b, i: (q_row_blk, b * nq + i)),
            pl.BlockSpec((seq, w), lambda b, i: (b, k_col_blk), pipeline_mode=pl.Buffered(1)),
            pl.BlockSpec((w, seq), lambda b, i: (v_row_blk, b), pipeline_mode=pl.Buffered(1)),
        ],
        out_specs=pl.BlockSpec((tq, w), lambda b, i: (b * nq + i, 0)),
        out_shape=jax.ShapeDtypeStruct((batch * seq, w), F32),
        scratch_shapes=[pltpu.VMEM((nb, w), BF16),
                        pltpu.VMEM((N_PAIRS, LANES, 2 * tq), BF16),
                        pltpu.VMEM((N_PAIRS, nb, 2 * tq), F32),
                        pltpu.VMEM((N_PAIRS, 1, 2 * tq), F32),
                        pltpu.VMEM((N_PAIRS, 2, HEAD_DIM + ONES_ROWS, tq), F32),
                        pltpu.VMEM((w, tq), F32)],
        compiler_params=_cparams(2),
        name="moba",
    )(qt_all, k_all, vt_all)


def _ordered_key(bits):
    return jnp.where(bits < 0, jnp.int32(INT_MIN) - bits, bits)


def _tree_sum(parts):
    while len(parts) > 1:
        parts = [parts[a] + parts[a + 1] for a in range(0, len(parts) - 1, 2)] + (
            [parts[-1]] if len(parts) % 2 else [])
    return parts[0]


def _dsa_kernel(qit_ref, ki_ref, wt_ref, qt_ref, k_ref, vt_ref, o_ref,
                score_ref, hi_ref, lo_ref, bias_ref, qop_ref, m_ref, acc_ref, ot_ref, *, tq, kc, kcb, n_sel):
    i = pl.program_id(1)
    nch = (i * tq + tq) // kc
    nchb = (i * tq + tq + kcb - 1) // kcb
    ntrip = (nch + DSA_UNROLL - 1) // DSA_UNROLL
    krow = lax.broadcasted_iota(jnp.int32, (kc, tq), 0)
    qpos = i * tq + lax.broadcasted_iota(jnp.int32, (kc, tq), 1)
    n_sel_f = float(n_sel)

    for p in range(N_PAIRS):
        qop_ref[p] = _pair_operand(qit_ref, p)
    w_rows = [wt_ref[h:h + 1, :] * (N_HEADS ** -0.5) for h in range(N_HEADS)]

    def score_body(c, carry, masked, unroll):
        starts = [pl.multiple_of((c * unroll + u) * kc, kc) for u in range(unroll)]
        rs = [[jnp.dot(ki_ref[pl.ds(st, kc), :], qop_ref[p], preferred_element_type=F32)
               for p in range(N_PAIRS)] for st in starts]
        for u, start in enumerate(starts):
            acc = jnp.zeros((kc, tq), F32)
            for p in range(N_PAIRS):
                acc = acc + jnp.maximum(rs[u][p][:, :tq], 0.0) * w_rows[2 * p]
                acc = acc + jnp.maximum(rs[u][p][:, tq:], 0.0) * w_rows[2 * p + 1]
            sc = jnp.where(start + krow <= qpos, acc, -jnp.inf) if masked else acc
            score_ref[pl.ds(start, kc), :] = sc
            key = _ordered_key(lax.bitcast_convert_type(sc, jnp.int32))
            hi_ref[pl.ds(start, kc), :] = jnp.right_shift(key, 16).astype(jnp.int16)
            lo_ref[pl.ds(start, kc), :] = (key ^ jnp.int32(0x8000)).astype(jnp.int16)
        return carry

    n_fast = (i * tq + 1) // (kc * SCORE_UNROLL)
    lax.fori_loop(0, n_fast, functools.partial(score_body, masked=False, unroll=SCORE_UNROLL), 0)
    lax.fori_loop(n_fast * SCORE_UNROLL, nch, functools.partial(score_body, masked=True, unroll=1), 0)

    @pl.when(nch * kc < nchb * kcb)
    def _():
        pad_rows = pl.ds(pl.multiple_of(nch * kc, kc), kcb - kc)
        score_ref[pad_rows, :] = jnp.full((kcb - kc, tq), -jnp.inf, F32)
        hi_ref[pad_rows, :] = jnp.full((kcb - kc, tq), I16_MIN, jnp.int16)
        lo_ref[pad_rows, :] = jnp.full((kcb - kc, tq), I16_MIN, jnp.int16)

    def count16(ref, pred):
        n_acc = 4
        rows = 2 * SUBLANES

        def body(c, accs):
            x = ref[pl.ds(pl.multiple_of(c * kcb, kcb), kcb), :]
            accs = list(accs)
            for r in range(kcb // rows):
                hit = jnp.where(pred(x[r * rows:(r + 1) * rows]), jnp.int16(1), jnp.int16(0))
                accs[r % n_acc] = accs[r % n_acc] + hit
            return tuple(accs)
        accs = lax.fori_loop(0, nchb, body, tuple(jnp.zeros((rows, tq), jnp.int16) for _ in range(n_acc)))
        tot = _tree_sum(list(accs)).astype(jnp.int32).astype(F32)
        return jnp.sum(tot, axis=0, keepdims=True)

    def bisect16(ref, target, count_at_min):
        def body(it, carry):
            t, c_ge = carry
            cand = t + jnp.left_shift(jnp.int32(1), 15 - it)
            cand16 = cand.astype(jnp.int16)
            cnt = count16(ref, lambda x: x >= cand16)
            ok = cnt >= target
            return jnp.where(ok, cand, t), jnp.where(ok, cnt, c_ge)
        return lax.fori_loop(0, 16, body, (jnp.full((1, tq), I16_MIN, jnp.int32), count_at_min))

    rows_seen = jnp.full((1, tq), (nchb * kcb).astype(F32))
    t_hi, c_ge_hi = bisect16(hi_ref, n_sel_f, rows_seen)
    t_hi16 = t_hi.astype(jnp.int16)
    c_gt_hi = count16(hi_ref, lambda x: x > t_hi16)

    def low_body(c, carry):
        rows = pl.ds(pl.multiple_of(c * kcb, kcb), kcb)
        lo_ref[rows, :] = jnp.where(hi_ref[rows, :] == t_hi16, lo_ref[rows, :], jnp.int16(I16_MIN))
        return carry

    lax.fori_loop(0, nchb, low_body, 0)
    t_lo, c_ge_lo = bisect16(lo_ref, n_sel_f - c_gt_hi, c_ge_hi - c_gt_hi)
    n_ge = c_gt_hi + c_ge_lo
    key_t = t_hi * 65536 + (t_lo + 32768)
    short = key_t <= KEY_NEG_INF
    thr = jnp.where(short, -jnp.inf, lax.bitcast_convert_type(_ordered_key(key_t), F32))
    has_tie = jnp.max(jnp.where(short, 0.0, n_ge - n_sel_f)) > 0.0

    @pl.when(jnp.logical_not(has_tie))
    def _():
        def body(c, carry):
            start = pl.multiple_of(c * kc, kc)
            sc = score_ref[pl.ds(start, kc), :]
            keep = (sc >= thr) & (start + krow <= qpos)
            bias_ref[pl.ds(start, kc), :] = jnp.where(keep, 0.0, NEG_BIG)
            return carry
        lax.fori_loop(0, nch, body, 0)

    @pl.when(has_tie)
    def _():
        lower = (lax.broadcasted_iota(jnp.int32, (kc, kc), 1)
                 < lax.broadcasted_iota(jnp.int32, (kc, kc), 0))
        lower = jnp.where(lower, 1.0, 0.0).astype(BF16)

        def gt_body(c, cnt):
            sc = score_ref[pl.ds(pl.multiple_of(c * kc, kc), kc), :]
            return cnt + jnp.sum(jnp.where(sc > thr, 1.0, 0.0), axis=0, keepdims=True)
        n_gt = lax.fori_loop(0, nch, gt_body, jnp.zeros((1, tq), F32))
        need = n_sel_f - n_gt

        def body(c, seen):
            start = pl.multiple_of(c * kc, kc)
            sc = score_ref[pl.ds(start, kc), :]
            eq = jnp.where(sc == thr, 1.0, 0.0)
            rank = seen + jnp.dot(lower, eq.astype(BF16), preferred_element_type=F32)
            keep = (sc > thr) | ((sc == thr) & (rank < need))
            keep = keep & (start + krow <= qpos)
            bias_ref[pl.ds(start, kc), :] = jnp.where(keep, 0.0, NEG_BIG)
            return seen + jnp.sum(eq, axis=0, keepdims=True)
        lax.fori_loop(0, nch, body, jnp.zeros((1, tq), F32))

    for p in range(N_PAIRS):
        qop_ref[p] = _pair_operand(qt_ref, p)
    _flash_init(m_ref, acc_ref)

    @pl.when(nch < ntrip * DSA_UNROLL)
    def _():
        pad = (DSA_UNROLL - 1) * kc
        bias_ref[pl.ds(pl.multiple_of(nch * kc, kc), pad), :] = jnp.full((pad, tq), NEG_BIG, F32)

    def lanes(p):
        return slice(p * LANES, (p + 1) * LANES)

    def att_body(c, carry):
        starts = [pl.multiple_of((c * DSA_UNROLL + u) * kc, kc) for u in range(DSA_UNROLL)]

        def add_bias(s, u, p):
            bias = bias_ref[pl.ds(starts[u], kc), :]
            return jnp.concatenate([s[:, :tq] + bias, s[:, tq:] + bias], axis=1)

        _flash_tiles(
            [(u, p) for u in range(DSA_UNROLL) for p in range(N_PAIRS)],
            lambda u, p: jnp.dot(k_ref[pl.ds(starts[u], kc), lanes(p)], qop_ref[p],
                                 preferred_element_type=F32),
            add_bias,
            lambda u, p: vt_ref[lanes(p), pl.ds(starts[u], kc)],
            m_ref, acc_ref)
        return carry

    lax.fori_loop(0, ntrip, att_body, 0)
    _flash_finish(o_ref, ot_ref, acc_ref, tq)


def _dsa(qt_all, k_all, vt_all, wt, batch, seq, tq, kc, kcb):
    nq = seq // tq
    w = BRANCH_WIDTH
    n_sel = min(IDX_TOPK_MAX, seq // 4)
    assert tq % kc == 0 and kcb % kc == 0 and kcb > kc and seq % kcb == 0
    ki_col_blk = (2 * w) // LANES
    return pl.pallas_call(
        functools.partial(_dsa_kernel, tq=tq, kc=kc, kcb=kcb, n_sel=n_sel),
        grid=(batch, nq),
        in_specs=[
            pl.BlockSpec((w, tq), lambda b, i: (2, b * nq + i)),
            pl.BlockSpec((seq, LANES), lambda b, i: (b, ki_col_blk), pipeline_mode=pl.Buffered(1)),
            pl.BlockSpec((2 * SUBLANES, tq), lambda b, i: (0, b * nq + i)),
            pl.BlockSpec((w, tq), lambda b, i: (1, b * nq + i)),
            pl.BlockSpec((seq, w), lambda b, i: (b, 1), pipeline_mode=pl.Buffered(1)),
            pl.BlockSpec((w, seq), lambda b, i: (1, b), pipeline_mode=pl.Buffered(1)),
        ],
        out_specs=pl.BlockSpec((tq, w), lambda b, i: (b * nq + i, 0)),
        out_shape=jax.ShapeDtypeStruct((batch * seq, w), F32),
        scratch_shapes=[pltpu.VMEM((seq, tq), F32),
                        pltpu.VMEM((seq, tq), jnp.int16),
                        pltpu.VMEM((seq, tq), jnp.int16),
                        pltpu.VMEM((seq, tq), F32),
                        pltpu.VMEM((N_PAIRS, LANES, 2 * tq), BF16),
                        pltpu.VMEM((N_PAIRS, 1, 2 * tq), F32),
                        pltpu.VMEM((N_PAIRS, 2, HEAD_DIM + ONES_ROWS, tq), F32),
                        pltpu.VMEM((w, tq), F32)],
        compiler_params=_cparams(2),
        name="dsa",
    )(qt_all, k_all, wt, qt_all, k_all, vt_all)


def _merge_kernel(ya_ref, yb_ref, h_ref, kv_ref, wmq_ref, wg_ref, wmix_ref, wb_ref, wo_ref, x_ref, o_ref):
    w = BRANCH_WIDTH
    d = x_ref.shape[1]
    h = h_ref[...]
    mq = jnp.dot(h, wmq_ref[...], preferred_element_type=F32).astype(BF16)
    ym_heads = []
    for hd in range(M_HEADS):
        cols = slice(hd * M_HEAD_DIM, (hd + 1) * M_HEAD_DIM)
        km = kv_ref[:, cols]
        vm = kv_ref[:, w + hd * M_HEAD_DIM:w + (hd + 1) * M_HEAD_DIM]
        sc = lax.dot_general(mq[:, cols], km, (((1,), (1,)), ((), ())),
                             preferred_element_type=F32) * (M_HEAD_DIM ** -0.5)
        sc = sc - jnp.max(sc, axis=-1, keepdims=True)
        e = jnp.exp(sc)
        p = e / jnp.sum(e, axis=-1, keepdims=True)
        ym_heads.append(jnp.dot(p.astype(BF16), vm, preferred_element_type=F32))
    ym = jnp.concatenate(ym_heads, axis=-1)
    ys = (ya_ref[...], yb_ref[...], ym)
    merged = None
    for n in range(3):
        g = jnp.dot(h, wg_ref[:, n * w:(n + 1) * w], preferred_element_type=F32)
        y = (ys[n] * (g * jax.nn.sigmoid(g))).astype(BF16)
        up = jnp.dot(y, wb_ref[n], preferred_element_type=F32)
        mix = jax.nn.sigmoid(jnp.dot(h, wmix_ref[:, n * d:(n + 1) * d], preferred_element_type=F32))
        term = mix * up
        merged = term if merged is None else merged + term
    o_ref[...] = x_ref[...] + jnp.dot(merged.astype(BF16), wo_ref[...], preferred_element_type=F32)


def _merge(ya, yb, h, kv, w_mq, w_g, w_mix, wb, wo, x2d, seq, n_mem, tm):
    m, d = x2d.shape
    w = BRANCH_WIDTH
    nt = seq // tm

    def resident(shape):
        return pl.BlockSpec(shape, lambda i: (0,) * len(shape), pipeline_mode=pl.Buffered(1))

    return pl.pallas_call(
        _merge_kernel,
        grid=(m // tm,),
        in_specs=[
            pl.BlockSpec((tm, w), lambda i: (i, 0)),
            pl.BlockSpec((tm, w), lambda i: (i, 0)),
            pl.BlockSpec((tm, d), lambda i: (i, 0)),
            pl.BlockSpec((n_mem, 2 * w), lambda i: (i // nt, 0)),
            resident((d, w)),
            resident((d, 3 * w)),
            resident((d, 3 * d)),
            resident((3, w, d)),
            resident((d, d)),
            pl.BlockSpec((tm, d), lambda i: (i, 0)),
        ],
        out_specs=pl.BlockSpec((tm, d), lambda i: (i, 0)),
        out_shape=jax.ShapeDtypeStruct((m, d), F32),
        compiler_params=_cparams(1),
        name="merge",
    )(ya, yb, h, kv, w_mq, w_g, w_mix, wb, wo, x2d)


def _rope_tables(seq):
    inv_freq = ROPE_THETA ** (-jnp.arange(ROT_HALF, dtype=F32) / ROT_HALF)
    ang = jnp.arange(seq, dtype=jnp.int32).astype(F32)[:, None] * inv_freq[None, :]
    cos, sin = jnp.cos(ang), jnp.sin(ang)
    r = np.arange(LANES) % HEAD_DIM
    f = r % ROT_HALF
    first = jnp.asarray(r < ROT_HALF)
    second = jnp.asarray((r >= ROT_HALF) & (r < 2 * ROT_HALF))
    rot = jnp.asarray(r < 2 * ROT_HALF)
    c_tok = jnp.where(rot[None, :], cos[:, f], 1.0)
    s1_tok = jnp.where(first[None, :], -sin[:, f], 0.0)
    s2_tok = jnp.where(second[None, :], sin[:, f], 0.0)
    return (c_tok, s1_tok, s2_tok), (cos.T, sin.T)


def kernel(x, mem, norm_g, w_in, mem_norm_g, w_mem_kv, w_branch, w_out, final_g):
    batch, seq, d = x.shape
    n_mem = mem.shape[1]
    depth = norm_g.shape[0]
    w = BRANCH_WIDTH
    m = batch * seq
    assert seq % MOBA_BLOCK == 0 and d % LANES == 0

    tm = min(1024, seq)
    tm_merge = min(512, seq)
    dsa_tq = 256
    dsa_kc = 256
    dsa_kcb = min(512, seq)

    tok_tabs, feat_tabs = _rope_tables(seq)
    offs = np.cumsum([0, w, w, w, w, w, w, w, w, N_HEADS * HEAD_DIM, HEAD_DIM, N_HEADS, w, w, 3 * d])
    (o_aq, o_ak, o_av, o_ag, o_bq, o_bk, o_bv, o_bg, o_iq, o_ik, o_iw, o_mq, o_mg, o_mix, o_end) = offs
    attn_scale = HEAD_DIM ** -0.5 * float(np.log2(np.e))
    idx_scale = HEAD_DIM ** -0.5

    x2d = x.reshape(m, d)
    mem2d = mem.reshape(batch * n_mem, d)
    for l in range(depth):
        wl = w_in[l]
        col = lambda a, b: wl[:, a:b]
        w_k = jnp.concatenate([col(o_ak, o_av), col(o_bk, o_bv), col(o_ik, o_iw), col(o_ik, o_iw)],
                              axis=1).astype(BF16)
        w_qt = jnp.concatenate([col(o_aq, o_ak) * attn_scale, col(o_bq, o_bk) * attn_scale,
                                col(o_iq, o_ik) * idx_scale], axis=1).T.astype(BF16)
        w_vt = jnp.concatenate([col(o_av, o_ag), col(o_bv, o_bg)], axis=1).T.astype(BF16)
        w_iwt = jnp.pad(col(o_iw, o_mq).T, ((0, 2 * SUBLANES - N_HEADS), (0, 0))).astype(BF16)
        w_mq = col(o_mq, o_mg).astype(BF16)
        w_g = jnp.concatenate([col(o_ag, o_bq), col(o_bg, o_iq), col(o_mg, o_mix)], axis=1).astype(BF16)
        w_mix = col(o_mix, o_end).astype(BF16)

        h = _rmsnorm(x2d, norm_g[l], BF16, tm)
        k_all = _mm_tok(h, w_k, "rope", BF16, tm, w_k.shape[1], tok_tabs, seq)
        qt_all = _mm_feat(w_qt, h, "rope", BF16, tm, w, feat_tabs, seq)
        vt_all = _mm_feat(w_vt, h, "none", BF16, tm, w)
        iwt = _mm_feat(w_iwt, h, "none", F32, tm, 2 * SUBLANES)

        ya = _moba(qt_all, k_all, vt_all, batch, seq, 0, 0, 0)
        yb = _dsa(qt_all, k_all, vt_all, iwt, batch, seq, dsa_tq, dsa_kc, dsa_kcb)

        mn = _rmsnorm(mem2d, mem_norm_g[l], BF16, n_mem)
        kv = _mm_tok(mn, w_mem_kv[l].astype(BF16), "none", BF16, n_mem, 2 * w)

        x2d = _merge(ya, yb, h, kv, w_mq, w_g, w_mix, w_branch[l].astype(BF16), w_out[l].astype(BF16),
                     x2d, seq, n_mem, tm_merge)

    out = _rmsnorm(x2d, final_g, F32, tm)
    return out.reshape(batch, seq, d)
```

```python
import functools

import jax
import jax.numpy as jnp
import numpy as np
from jax import lax
from jax.experimental import pallas as pl
from jax.experimental.pallas import tpu as pltpu

F32 = jnp.float32
BF16 = jnp.bfloat16

HEAD_DIM = 64
N_HEADS = 8
BRANCH_WIDTH = 512
M_HEADS = 4
M_HEAD_DIM = 128
MOBA_BLOCK = 256
MOBA_TOPK = 3
IDX_TOPK_MAX = 256
ROPE_THETA = 500000.0
ROT_HALF = HEAD_DIM // 4 // 2
RMS_EPS = 1e-6

LANES = 128
SUBLANES = 8
VMEM_LIMIT = 56 * 1024 * 1024
NEG_BIG = -1e30
INT_MIN = -(2 ** 31)
I16_MIN = -(2 ** 15)
KEY_NEG_INF = -0x7F800000


def _cparams(n_axes):
    return pltpu.CompilerParams(dimension_semantics=("arbitrary",) * n_axes,
                                vmem_limit_bytes=VMEM_LIMIT)


def _rmsnorm_kernel(x_ref, g_ref, o_ref):
    xf = x_ref[...]
    y = xf * lax.rsqrt(jnp.mean(xf * xf, axis=-1, keepdims=True) + RMS_EPS)
    o_ref[...] = (y * g_ref[...]).astype(o_ref.dtype)


def _rmsnorm(x2d, g, out_dtype, tm):
    m, d = x2d.shape
    return pl.pallas_call(
        _rmsnorm_kernel,
        grid=(m // tm,),
        in_specs=[pl.BlockSpec((tm, d), lambda i: (i, 0)),
                  pl.BlockSpec((1, d), lambda i: (0, 0))],
        out_specs=pl.BlockSpec((tm, d), lambda i: (i, 0)),
        out_shape=jax.ShapeDtypeStruct((m, d), out_dtype),
        compiler_params=_cparams(1),
        name="rmsnorm",
    )(x2d, g.reshape(1, d))


def _mm_tok_kernel(h_ref, w_ref, *rest, epilogue):
    o_ref = rest[-1]
    acc = jnp.dot(h_ref[...], w_ref[...], preferred_element_type=F32)
    if epilogue == "rope":
        c_ref, s1_ref, s2_ref = rest[:3]
        c, s1, s2 = c_ref[...], s1_ref[...], s2_ref[...]
        for j in range(acc.shape[1] // LANES):
            piece = acc[:, j * LANES:(j + 1) * LANES]
            up = pltpu.roll(piece, LANES - ROT_HALF, 1)
            down = pltpu.roll(piece, ROT_HALF, 1)
            o_ref[:, j * LANES:(j + 1) * LANES] = (piece * c + up * s1 + down * s2).astype(o_ref.dtype)
    else:
        o_ref[...] = acc.astype(o_ref.dtype)


def _mm_tok(h, w, epilogue, out_dtype, tm, tn, rope_tabs=None, seq=None):
    m, k = h.shape
    n = w.shape[1]
    in_specs = [pl.BlockSpec((tm, k), lambda i, j: (i, 0)),
                pl.BlockSpec((k, tn), lambda i, j: (0, j))]
    args = [h, w]
    if epilogue == "rope":
        nt = seq // tm
        for t in rope_tabs:
            in_specs.append(pl.BlockSpec((tm, LANES), lambda i, j: (i % nt, 0)))
            args.append(t)
    return pl.pallas_call(
        functools.partial(_mm_tok_kernel, epilogue=epilogue),
        grid=(m // tm, n // tn),
        in_specs=in_specs,
        out_specs=pl.BlockSpec((tm, tn), lambda i, j: (i, j)),
        out_shape=jax.ShapeDtypeStruct((m, n), out_dtype),
        compiler_params=_cparams(2),
        name="proj_tok_" + epilogue,
    )(*args)


def _mm_feat_kernel(wt_ref, h_ref, *rest, epilogue):
    o_ref = rest[-1]
    acc = lax.dot_general(wt_ref[...], h_ref[...], (((1,), (1,)), ((), ())),
                          preferred_element_type=F32)
    if epilogue == "rope":
        cos_ref, sin_ref = rest[:2]
        c, s = cos_ref[...], sin_ref[...]
        pieces = []
        for hh in range(acc.shape[0] // HEAD_DIM):
            base = hh * HEAD_DIM
            x1 = acc[base:base + ROT_HALF]
            x2 = acc[base + ROT_HALF:base + 2 * ROT_HALF]
            pieces += [x1 * c - x2 * s, x2 * c + x1 * s, acc[base + 2 * ROT_HALF:base + HEAD_DIM]]
        acc = jnp.concatenate(pieces, axis=0)
    o_ref[...] = acc.astype(o_ref.dtype)


def _mm_feat(wt, h, epilogue, out_dtype, tm, tn, rope_tabs=None, seq=None):
    n, k = wt.shape
    m = h.shape[0]
    in_specs = [pl.BlockSpec((tn, k), lambda i, j: (j, 0)),
                pl.BlockSpec((tm, k), lambda i, j: (i, 0))]
    args = [wt, h]
    if epilogue == "rope":
        nt = seq // tm
        for t in rope_tabs:
            in_specs.append(pl.BlockSpec((SUBLANES, tm), lambda i, j: (0, i % nt)))
            args.append(t)
    return pl.pallas_call(
        functools.partial(_mm_feat_kernel, epilogue=epilogue),
        grid=(m // tm, n // tn),
        in_specs=in_specs,
        out_specs=pl.BlockSpec((tn, tm), lambda i, j: (j, i)),
        out_shape=jax.ShapeDtypeStruct((n, m), out_dtype),
        compiler_params=_cparams(2),
        name="proj_feat_" + epilogue,
    )(*args)


N_PAIRS = N_HEADS // 2


def _pair_operand(qt_ref, pair):
    qp = qt_ref[pair * LANES:(pair + 1) * LANES, :].astype(F32)
    row = lax.broadcasted_iota(jnp.int32, qp.shape, 0)
    lo = jnp.where(row < HEAD_DIM, qp, 0.0)
    hi = jnp.where(row >= HEAD_DIM, qp, 0.0)
    return jnp.concatenate([lo, hi], axis=1).astype(BF16)


ONES_ROWS = 16
QK_LOOKAHEAD = 4


OFFSET_SLACK = 1.0 + 2.0 ** -5
L_FLOOR = 2.0 ** -80


def _pv_accumulate(pm, vt_tile, acc_ref, p, scale=None):
    keys = vt_tile.shape[1]
    tq = pm.shape[1] // 2
    ones = jnp.ones((ONES_ROWS, keys), BF16)
    for e in range(2):
        vt_ext = jnp.concatenate([vt_tile[e * HEAD_DIM:(e + 1) * HEAD_DIM], ones], axis=0)
        cols = slice(e * tq, (e + 1) * tq)
        upd = jnp.dot(vt_ext, pm[:, cols], preferred_element_type=F32)
        old = acc_ref[p, e] if scale is None else scale[:, cols] * acc_ref[p, e]
        acc_ref[p, e] = old + upd


def _online_update(s, vt_tile, m_ref, acc_ref, p):
    m = m_ref[p]
    m_new = jnp.maximum(m, jnp.max(s, axis=0, keepdims=True))
    _pv_accumulate(jnp.exp2(s - m_new).astype(BF16), vt_tile, acc_ref, p, scale=jnp.exp2(m - m_new))
    m_ref[p] = m_new


def _offset_update(s, vt_tile, acc_ref, p):
    _pv_accumulate(jnp.exp2(s).astype(BF16), vt_tile, acc_ref, p)


def _flash_tiles(tiles, qk, mask, update):
    ss = [qk(*t) for t in tiles[:QK_LOOKAHEAD]]
    for n, (u, p) in enumerate(tiles):
        if n + QK_LOOKAHEAD < len(tiles):
            ss.append(qk(*tiles[n + QK_LOOKAHEAD]))
        update(mask(ss[n], u, p), u, p)
        ss[n] = None


def _key_norm_bound(k_ref, kb_ref, seq, tq):
    chunk = min(512, seq)
    width = k_ref.shape[1]
    lane = lax.broadcasted_iota(jnp.int32, (width, LANES), 0)
    head = lax.broadcasted_iota(jnp.int32, (width, LANES), 1)
    group = jnp.where((lane >= head * HEAD_DIM) & (lane < (head + 1) * HEAD_DIM), 1.0, 0.0).astype(BF16)

    def body(c, mx):
        kk = k_ref[pl.ds(pl.multiple_of(c * chunk, chunk), chunk), :].astype(F32)
        n2 = jnp.dot((kk * kk).astype(BF16), group, preferred_element_type=F32)
        return jnp.maximum(mx, jnp.max(n2, axis=0, keepdims=True))
    mx = lax.fori_loop(0, seq // chunk, body, jnp.zeros((1, LANES), F32))
    for p in range(N_PAIRS):
        kb_ref[p] = jnp.concatenate([jnp.broadcast_to(mx[:, 2 * p:2 * p + 1], (1, tq)),
                                     jnp.broadcast_to(mx[:, 2 * p + 1:2 * p + 2], (1, tq))], axis=1)


def _offset_rows(qop, kb2):
    qf = qop.astype(F32)
    bound = jnp.sqrt(jnp.sum(qf * qf, axis=0, keepdims=True) * kb2) * OFFSET_SLACK
    row = lax.broadcasted_iota(jnp.int32, (ONES_ROWS, qop.shape[1]), 0)
    return jnp.where(row == 0, -bound, 0.0).astype(BF16)


def _denominator_ok(acc_ref):
    return jnp.min(acc_ref[:, :, HEAD_DIM:HEAD_DIM + 1, :]) > L_FLOOR


def _flash_finish(o_ref, ot_ref, acc_ref, tq):
    for p in range(N_PAIRS):
        for e in range(2):
            lo = p * LANES + e * HEAD_DIM
            ot_ref[lo:lo + HEAD_DIM, :] = acc_ref[p, e, :HEAD_DIM] / acc_ref[p, e, HEAD_DIM:HEAD_DIM + 1]
    o_ref[...] = ot_ref[...].T


MOBA_UNROLL = 2
DSA_UNROLL = 2
SCORE_UNROLL = 2


def _moba_kernel(qt_ref, k_ref, vt_ref, o_ref, kmean_ref, kb_ref, qop_ref, sel_ref, m_ref, acc_ref,
                 ot_ref, *, seq, topk):
    i = pl.program_id(1)
    blk = MOBA_BLOCK
    nb = seq // blk
    tq = blk

    @pl.when(i == 0)
    def _():
        r = lax.broadcasted_iota(jnp.int32, (nb, seq), 0)
        c = lax.broadcasted_iota(jnp.int32, (nb, seq), 1)
        member = jnp.where((c >= r * blk) & (c < (r + 1) * blk), 1.0, 0.0).astype(BF16)
        ksum = jnp.dot(member, k_ref[...], preferred_element_type=F32)
        kmean_ref[...] = (ksum * (1.0 / blk)).astype(BF16)
        _key_norm_bound(k_ref, kb_ref, seq, tq)

    blk_id = lax.broadcasted_iota(jnp.int32, (nb, 2 * tq), 0)
    past = blk_id < i
    own_start = pl.multiple_of(i * blk, blk)

    for p in range(N_PAIRS):
        lanes = slice(p * LANES, (p + 1) * LANES)
        qop = _pair_operand(qt_ref, p)
        qop_ref[p, :LANES] = qop
        qop_ref[p, LANES:LANES + ONES_ROWS] = _offset_rows(qop, kb_ref[p])
        gate = jnp.dot(kmean_ref[:, lanes], qop, preferred_element_type=F32)
        gate = jnp.where(past, gate, -jnp.inf)
        sel = jnp.zeros((nb, 2 * tq), jnp.bool_)
        for _ in range(topk):
            mx = jnp.max(gate, axis=0, keepdims=True)
            first = jnp.min(jnp.where(gate == mx, blk_id, nb), axis=0, keepdims=True)
            hit = blk_id == first
            sel = sel | hit
            gate = jnp.where(hit, -jnp.inf, gate)
        sel_ref[p] = jnp.where(sel & past, 1.0, 0.0)
        qop_ref[p, LANES + ONES_ROWS:LANES + ONES_ROWS + nb] = jnp.where(sel & past, 0.0, NEG_BIG).astype(BF16)
        qop_ref[p, LANES + ONES_ROWS + nb:] = jnp.zeros((LANES - ONES_ROWS - nb, 2 * tq), BF16)

    def lanes(p):
        return slice(p * LANES, (p + 1) * LANES)

    krow = lax.broadcasted_iota(jnp.int32, (blk, 2 * tq), 0)
    qcol = lax.broadcasted_iota(jnp.int32, (blk, 2 * tq), 1)
    causal = krow <= jnp.where(qcol < tq, qcol, qcol - tq)
    lane = lax.broadcasted_iota(jnp.int32, (blk, LANES), 1)
    ntrip = (i + MOBA_UNROLL - 1) // MOBA_UNROLL

    def attend(fast):
        acc_ref[...] = jnp.zeros(acc_ref.shape, F32)
        if not fast:
            m_ref[...] = jnp.full(m_ref.shape, NEG_BIG, F32)

        def key_operand(start, extra_lane):
            k_tile = k_ref[pl.ds(start, blk), :]
            if not fast:
                return lambda p: k_tile[:, lanes(p)]
            sel_lanes = (lane == 0) if extra_lane is None else ((lane == 0) | (lane == extra_lane))
            extra = jnp.where(sel_lanes, 1.0, 0.0).astype(BF16)
            return lambda p: jnp.concatenate([k_tile[:, lanes(p)], extra], axis=1)

        def query_operand(p):
            return qop_ref[p] if fast else qop_ref[p, :LANES]

        def update(vt_tile):
            if fast:
                return lambda s, u, p: _offset_update(s, vt_tile(u, p), acc_ref, p)
            return lambda s, u, p: _online_update(s, vt_tile(u, p), m_ref, acc_ref, p)

        own_keys = key_operand(own_start, None)
        _flash_tiles(
            [(0, p) for p in range(N_PAIRS)],
            lambda u, p: jnp.dot(own_keys(p), query_operand(p), preferred_element_type=F32),
            lambda s, u, p: jnp.where(causal, s, -jnp.inf),
            update(lambda u, p: vt_ref[lanes(p), pl.ds(own_start, blk)]))

        def body(c, carry):
            js = [c * MOBA_UNROLL + u for u in range(MOBA_UNROLL)]
            starts = [pl.multiple_of(j * blk, blk) for j in js]
            keys = [key_operand(starts[u], ONES_ROWS + js[u]) for u in range(MOBA_UNROLL)]
            _flash_tiles(
                [(u, p) for u in range(MOBA_UNROLL) for p in range(N_PAIRS)],
                lambda u, p: jnp.dot(keys[u](p), query_operand(p), preferred_element_type=F32),
                (lambda s, u, p: s) if fast else
                (lambda s, u, p: jnp.where(sel_ref[p, pl.ds(js[u], 1), :] > 0.5, s, -jnp.inf)),
                update(lambda u, p: vt_ref[lanes(p), pl.ds(starts[u], blk)]))
            return carry

        lax.fori_loop(0, ntrip, body, 0)

    attend(True)

    @pl.when(jnp.logical_not(_denominator_ok(acc_ref)))
    def _():
        attend(False)

    _flash_finish(o_ref, ot_ref, acc_ref, tq)


def _moba(qt_all, k_all, vt_all, batch, seq, q_row_blk, k_col_blk, v_row_blk):
    tq = MOBA_BLOCK
    nq = seq // tq
    nb = seq // MOBA_BLOCK
    topk = min(MOBA_TOPK, nb - 1)
    w = BRANCH_WIDTH
    return pl.pallas_call(
        functools.partial(_moba_kernel, seq=seq, topk=topk),
        grid=(batch, nq),
        in_specs=[
            pl.BlockSpec((w, tq), lambda b, i: (q_row_blk, b * nq + i)),
            pl.BlockSpec((seq, w), lambda b, i: (b, k_col_blk), pipeline_mode=pl.Buffered(1)),
            pl.BlockSpec((w, seq), lambda b, i: (v_row_blk, b), pipeline_mode=pl.Buffered(1)),
        ],
        out_specs=pl.BlockSpec((tq, w), lambda b, i: (b * nq + i, 0)),
        out_shape=jax.ShapeDtypeStruct((batch * seq, w), F32),
        scratch_shapes=[pltpu.VMEM((nb, w), BF16),
                        pltpu.VMEM((N_PAIRS, 1, 2 * tq), F32),
                        pltpu.VMEM((N_PAIRS, 2 * LANES, 2 * tq), BF16),
                        pltpu.VMEM((N_PAIRS, nb, 2 * tq), F32),
                        pltpu.VMEM((N_PAIRS, 1, 2 * tq), F32),
                        pltpu.VMEM((N_PAIRS, 2, HEAD_DIM + ONES_ROWS, tq), F32),
                        pltpu.VMEM((w, tq), F32)],
        compiler_params=_cparams(2),
        name="moba",
    )(qt_all, k_all, vt_all)


def _ordered_key(bits):
    return jnp.where(bits < 0, jnp.int32(INT_MIN) - bits, bits)


def _tree_sum(parts):
    while len(parts) > 1:
        parts = [parts[a] + parts[a + 1] for a in range(0, len(parts) - 1, 2)] + (
            [parts[-1]] if len(parts) % 2 else [])
    return parts[0]


def _dsa_kernel(qit_ref, ki_ref, wt_ref, qt_ref, k_ref, vt_ref, o_ref,
                score_ref, hi_ref, lo_ref, bias_ref, kb_ref, qop_ref, m_ref, acc_ref, ot_ref,
                *, tq, kc, kcb, n_sel):
    i = pl.program_id(1)
    nch = (i * tq + tq) // kc
    nchb = (i * tq + tq + kcb - 1) // kcb
    ntrip = (nch + DSA_UNROLL - 1) // DSA_UNROLL
    krow = lax.broadcasted_iota(jnp.int32, (kc, tq), 0)
    qpos = i * tq + lax.broadcasted_iota(jnp.int32, (kc, tq), 1)
    n_sel_f = float(n_sel)

    @pl.when(i == 0)
    def _():
        _key_norm_bound(k_ref, kb_ref, k_ref.shape[0], tq)

    for p in range(N_PAIRS):
        qop_ref[p, :LANES] = _pair_operand(qit_ref, p)
    w_rows = [wt_ref[h:h + 1, :] * (N_HEADS ** -0.5) for h in range(N_HEADS)]

    def score_body(c, carry, masked, unroll):
        starts = [pl.multiple_of((c * unroll + u) * kc, kc) for u in range(unroll)]
        rs = [[jnp.dot(ki_ref[pl.ds(st, kc), :], qop_ref[p, :LANES], preferred_element_type=F32)
               for p in range(N_PAIRS)] for st in starts]
        for u, start in enumerate(starts):
            acc = jnp.zeros((kc, tq), F32)
            for p in range(N_PAIRS):
                acc = acc + jnp.maximum(rs[u][p][:, :tq], 0.0) * w_rows[2 * p]
                acc = acc + jnp.maximum(rs[u][p][:, tq:], 0.0) * w_rows[2 * p + 1]
            sc = jnp.where(start + krow <= qpos, acc, -jnp.inf) if masked else acc
            score_ref[pl.ds(start, kc), :] = sc
            key = _ordered_key(lax.bitcast_convert_type(sc, jnp.int32))
            hi_ref[pl.ds(start, kc), :] = jnp.right_shift(key, 16).astype(jnp.int16)
            lo_ref[pl.ds(start, kc), :] = (key ^ jnp.int32(0x8000)).astype(jnp.int16)
        return carry

    n_fast = (i * tq + 1) // (kc * SCORE_UNROLL)
    lax.fori_loop(0, n_fast, functools.partial(score_body, masked=False, unroll=SCORE_UNROLL), 0)
    lax.fori_loop(n_fast * SCORE_UNROLL, nch, functools.partial(score_body, masked=True, unroll=1), 0)

    @pl.when(nch * kc < nchb * kcb)
    def _():
        pad_rows = pl.ds(pl.multiple_of(nch * kc, kc), kcb - kc)
        score_ref[pad_rows, :] = jnp.full((kcb - kc, tq), -jnp.inf, F32)
        hi_ref[pad_rows, :] = jnp.full((kcb - kc, tq), I16_MIN, jnp.int16)
        lo_ref[pad_rows, :] = jnp.full((kcb - kc, tq), I16_MIN, jnp.int16)

    def count16(ref, pred):
        n_acc = 4
        rows = 2 * SUBLANES

        def body(c, accs):
            x = ref[pl.ds(pl.multiple_of(c * kcb, kcb), kcb), :]
            accs = list(accs)
            for r in range(kcb // rows):
                hit = jnp.where(pred(x[r * rows:(r + 1) * rows]), jnp.int16(1), jnp.int16(0))
                accs[r % n_acc] = accs[r % n_acc] + hit
            return tuple(accs)
        accs = lax.fori_loop(0, nchb, body, tuple(jnp.zeros((rows, tq), jnp.int16) for _ in range(n_acc)))
        tot = _tree_sum(list(accs)).astype(jnp.int32).astype(F32)
        return jnp.sum(tot, axis=0, keepdims=True)

    def bisect16(ref, target, count_at_min):
        def body(it, carry):
            t, c_ge = carry
            cand = t + jnp.left_shift(jnp.int32(1), 15 - it)
            cand16 = cand.astype(jnp.int16)
            cnt = count16(ref, lambda x: x >= cand16)
            ok = cnt >= target
            return jnp.where(ok, cand, t), jnp.where(ok, cnt, c_ge)
        return lax.fori_loop(0, 16, body, (jnp.full((1, tq), I16_MIN, jnp.int32), count_at_min))

    rows_seen = jnp.full((1, tq), (nchb * kcb).astype(F32))
    t_hi, c_ge_hi = bisect16(hi_ref, n_sel_f, rows_seen)
    t_hi16 = t_hi.astype(jnp.int16)
    c_gt_hi = count16(hi_ref, lambda x: x > t_hi16)

    def low_body(c, carry):
        rows = pl.ds(pl.multiple_of(c * kcb, kcb), kcb)
        lo_ref[rows, :] = jnp.where(hi_ref[rows, :] == t_hi16, lo_ref[rows, :], jnp.int16(I16_MIN))
        return carry

    lax.fori_loop(0, nchb, low_body, 0)
    t_lo, c_ge_lo = bisect16(lo_ref, n_sel_f - c_gt_hi, c_ge_hi - c_gt_hi)
    n_ge = c_gt_hi + c_ge_lo
    key_t = t_hi * 65536 + (t_lo + 32768)
    short = key_t <= KEY_NEG_INF
    thr = jnp.where(short, -jnp.inf, lax.bitcast_convert_type(_ordered_key(key_t), F32))
    has_tie = jnp.max(jnp.where(short, 0.0, n_ge - n_sel_f)) > 0.0

    @pl.when(jnp.logical_not(has_tie))
    def _():
        def body(c, carry):
            start = pl.multiple_of(c * kc, kc)
            sc = score_ref[pl.ds(start, kc), :]
            keep = (sc >= thr) & (start + krow <= qpos)
            bias_ref[pl.ds(start, kc), :] = jnp.where(keep, 0.0, NEG_BIG)
            return carry
        lax.fori_loop(0, nch, body, 0)

    @pl.when(has_tie)
    def _():
        lower = (lax.broadcasted_iota(jnp.int32, (kc, kc), 1)
                 < lax.broadcasted_iota(jnp.int32, (kc, kc), 0))
        lower = jnp.where(lower, 1.0, 0.0).astype(BF16)

        def gt_body(c, cnt):
            sc = score_ref[pl.ds(pl.multiple_of(c * kc, kc), kc), :]
            return cnt + jnp.sum(jnp.where(sc > thr, 1.0, 0.0), axis=0, keepdims=True)
        n_gt = lax.fori_loop(0, nch, gt_body, jnp.zeros((1, tq), F32))
        need = n_sel_f - n_gt

        def body(c, seen):
            start = pl.multiple_of(c * kc, kc)
            sc = score_ref[pl.ds(start, kc), :]
            eq = jnp.where(sc == thr, 1.0, 0.0)
            rank = seen + jnp.dot(lower, eq.astype(BF16), preferred_element_type=F32)
            keep = (sc > thr) | ((sc == thr) & (rank < need))
            keep = keep & (start + krow <= qpos)
            bias_ref[pl.ds(start, kc), :] = jnp.where(keep, 0.0, NEG_BIG)
            return seen + jnp.sum(eq, axis=0, keepdims=True)
        lax.fori_loop(0, nch, body, jnp.zeros((1, tq), F32))

    for p in range(N_PAIRS):
        qop = _pair_operand(qt_ref, p)
        qop_ref[p, :LANES] = qop
        qop_ref[p, LANES:LANES + ONES_ROWS] = _offset_rows(qop, kb_ref[p])
        qop_ref[p, LANES + ONES_ROWS:] = jnp.zeros((LANES - ONES_ROWS, 2 * tq), BF16)

    @pl.when(nch < ntrip * DSA_UNROLL)
    def _():
        pad = (DSA_UNROLL - 1) * kc
        bias_ref[pl.ds(pl.multiple_of(nch * kc, kc), pad), :] = jnp.full((pad, tq), NEG_BIG, F32)

    def lanes(p):
        return slice(p * LANES, (p + 1) * LANES)

    def attend(fast):
        acc_ref[...] = jnp.zeros(acc_ref.shape, F32)
        if not fast:
            m_ref[...] = jnp.full(m_ref.shape, NEG_BIG, F32)
        ones = jnp.ones((kc, LANES), BF16)

        def att_body(c, carry):
            starts = [pl.multiple_of((c * DSA_UNROLL + u) * kc, kc) for u in range(DSA_UNROLL)]

            def qk(u, p):
                k_tile = k_ref[pl.ds(starts[u], kc), lanes(p)]
                if fast:
                    return jnp.dot(jnp.concatenate([k_tile, ones], axis=1), qop_ref[p],
                                   preferred_element_type=F32)
                return jnp.dot(k_tile, qop_ref[p, :LANES], preferred_element_type=F32)

            def add_bias(s, u, p):
                bias = bias_ref[pl.ds(starts[u], kc), :]
                return jnp.concatenate([s[:, :tq] + bias, s[:, tq:] + bias], axis=1)

            def update(s, u, p):
                vt_tile = vt_ref[lanes(p), pl.ds(starts[u], kc)]
                if fast:
                    _offset_update(s, vt_tile, acc_ref, p)
                else:
                    _online_update(s, vt_tile, m_ref, acc_ref, p)

            _flash_tiles([(u, p) for u in range(DSA_UNROLL) for p in range(N_PAIRS)], qk, add_bias, update)
            return carry

        lax.fori_loop(0, ntrip, att_body, 0)

    attend(True)

    @pl.when(jnp.logical_not(_denominator_ok(acc_ref)))
    def _():
        attend(False)

    _flash_finish(o_ref, ot_ref, acc_ref, tq)


def _dsa(qt_all, k_all, vt_all, wt, batch, seq, tq, kc, kcb):
    nq = seq // tq
    w = BRANCH_WIDTH
    n_sel = min(IDX_TOPK_MAX, seq // 4)
    assert tq % kc == 0 and kcb % kc == 0 and kcb > kc and seq % kcb == 0
    ki_col_blk = (2 * w) // LANES
    return pl.pallas_call(
        functools.partial(_dsa_kernel, tq=tq, kc=kc, kcb=kcb, n_sel=n_sel),
        grid=(batch, nq),
        in_specs=[
            pl.BlockSpec((w, tq), lambda b, i: (2, b * nq + i)),
            pl.BlockSpec((seq, LANES), lambda b, i: (b, ki_col_blk), pipeline_mode=pl.Buffered(1)),
            pl.BlockSpec((2 * SUBLANES, tq), lambda b, i: (0, b * nq + i)),
            pl.BlockSpec((w, tq), lambda b, i: (1, b * nq + i)),
            pl.BlockSpec((seq, w), lambda b, i: (b, 1), pipeline_mode=pl.Buffered(1)),
            pl.BlockSpec((w, seq), lambda b, i: (1, b), pipeline_mode=pl.Buffered(1)),
        ],
        out_specs=pl.BlockSpec((tq, w), lambda b, i: (b * nq + i, 0)),
        out_shape=jax.ShapeDtypeStruct((batch * seq, w), F32),
        scratch_shapes=[pltpu.VMEM((seq, tq), F32),
                        pltpu.VMEM((seq, tq), jnp.int16),
                        pltpu.VMEM((seq, tq), jnp.int16),
                        pltpu.VMEM((seq, tq), F32),
                        pltpu.VMEM((N_PAIRS, 1, 2 * tq), F32),
                        pltpu.VMEM((N_PAIRS, 2 * LANES, 2 * tq), BF16),
                        pltpu.VMEM((N_PAIRS, 1, 2 * tq), F32),
                        pltpu.VMEM((N_PAIRS, 2, HEAD_DIM + ONES_ROWS, tq), F32),
                        pltpu.VMEM((w, tq), F32)],
        compiler_params=_cparams(2),
        name="dsa",
    )(qt_all, k_all, wt, qt_all, k_all, vt_all)


def _merge_kernel(ya_ref, yb_ref, h_ref, kv_ref, wmq_ref, wg_ref, wmix_ref, wb_ref, wo_ref, x_ref, o_ref):
    w = BRANCH_WIDTH
    d = x_ref.shape[1]
    h = h_ref[...]
    mq = jnp.dot(h, wmq_ref[...], preferred_element_type=F32).astype(BF16)
    ym_heads = []
    for hd in range(M_HEADS):
        cols = slice(hd * M_HEAD_DIM, (hd + 1) * M_HEAD_DIM)
        km = kv_ref[:, cols]
        vm = kv_ref[:, w + hd * M_HEAD_DIM:w + (hd + 1) * M_HEAD_DIM]
        sc = lax.dot_general(mq[:, cols], km, (((1,), (1,)), ((), ())),
                             preferred_element_type=F32) * (M_HEAD_DIM ** -0.5)
        sc = sc - jnp.max(sc, axis=-1, keepdims=True)
        e = jnp.exp(sc)
        p = e / jnp.sum(e, axis=-1, keepdims=True)
        ym_heads.append(jnp.dot(p.astype(BF16), vm, preferred_element_type=F32))
    ym = jnp.concatenate(ym_heads, axis=-1)
    ys = (ya_ref[...], yb_ref[...], ym)
    merged = None
    for n in range(3):
        g = jnp.dot(h, wg_ref[:, n * w:(n + 1) * w], preferred_element_type=F32)
        y = (ys[n] * (g * jax.nn.sigmoid(g))).astype(BF16)
        up = jnp.dot(y, wb_ref[n], preferred_element_type=F32)
        mix = jax.nn.sigmoid(jnp.dot(h, wmix_ref[:, n * d:(n + 1) * d], preferred_element_type=F32))
        term = mix * up
        merged = term if merged is None else merged + term
    o_ref[...] = x_ref[...] + jnp.dot(merged.astype(BF16), wo_ref[...], preferred_element_type=F32)


def _merge(ya, yb, h, kv, w_mq, w_g, w_mix, wb, wo, x2d, seq, n_mem, tm):
    m, d = x2d.shape
    w = BRANCH_WIDTH
    nt = seq // tm

    def resident(shape):
        return pl.BlockSpec(shape, lambda i: (0,) * len(shape), pipeline_mode=pl.Buffered(1))

    return pl.pallas_call(
        _merge_kernel,
        grid=(m // tm,),
        in_specs=[
            pl.BlockSpec((tm, w), lambda i: (i, 0)),
            pl.BlockSpec((tm, w), lambda i: (i, 0)),
            pl.BlockSpec((tm, d), lambda i: (i, 0)),
            pl.BlockSpec((n_mem, 2 * w), lambda i: (i // nt, 0)),
            resident((d, w)),
            resident((d, 3 * w)),
            resident((d, 3 * d)),
            resident((3, w, d)),
            resident((d, d)),
            pl.BlockSpec((tm, d), lambda i: (i, 0)),
        ],
        out_specs=pl.BlockSpec((tm, d), lambda i: (i, 0)),
        out_shape=jax.ShapeDtypeStruct((m, d), F32),
        compiler_params=_cparams(1),
        name="merge",
    )(ya, yb, h, kv, w_mq, w_g, w_mix, wb, wo, x2d)


def _rope_tables(seq):
    inv_freq = ROPE_THETA ** (-jnp.arange(ROT_HALF, dtype=F32) / ROT_HALF)
    ang = jnp.arange(seq, dtype=jnp.int32).astype(F32)[:, None] * inv_freq[None, :]
    cos, sin = jnp.cos(ang), jnp.sin(ang)
    r = np.arange(LANES) % HEAD_DIM
    f = r % ROT_HALF
    first = jnp.asarray(r < ROT_HALF)
    second = jnp.asarray((r >= ROT_HALF) & (r < 2 * ROT_HALF))
    rot = jnp.asarray(r < 2 * ROT_HALF)
    c_tok = jnp.where(rot[None, :], cos[:, f], 1.0)
    s1_tok = jnp.where(first[None, :], -sin[:, f], 0.0)
    s2_tok = jnp.where(second[None, :], sin[:, f], 0.0)
    return (c_tok, s1_tok, s2_tok), (cos.T, sin.T)


def kernel(x, mem, norm_g, w_in, mem_norm_g, w_mem_kv, w_branch, w_out, final_g):
    batch, seq, d = x.shape
    n_mem = mem.shape[1]
    depth = norm_g.shape[0]
    w = BRANCH_WIDTH
    m = batch * seq
    assert seq % MOBA_BLOCK == 0 and d % LANES == 0

    tm = min(1024, seq)
    tm_merge = min(512, seq)
    dsa_tq = 256
    dsa_kc = 256
    dsa_kcb = min(512, seq)

    tok_tabs, feat_tabs = _rope_tables(seq)
    offs = np.cumsum([0, w, w, w, w, w, w, w, w, N_HEADS * HEAD_DIM, HEAD_DIM, N_HEADS, w, w, 3 * d])
    (o_aq, o_ak, o_av, o_ag, o_bq, o_bk, o_bv, o_bg, o_iq, o_ik, o_iw, o_mq, o_mg, o_mix, o_end) = offs
    attn_scale = HEAD_DIM ** -0.5 * float(np.log2(np.e))
    idx_scale = HEAD_DIM ** -0.5

    x2d = x.reshape(m, d)
    mem2d = mem.reshape(batch * n_mem, d)
    for l in range(depth):
        wl = w_in[l]
        col = lambda a, b: wl[:, a:b]
        w_k = jnp.concatenate([col(o_ak, o_av), col(o_bk, o_bv), col(o_ik, o_iw), col(o_ik, o_iw)],
                              axis=1).astype(BF16)
        w_qt = jnp.concatenate([col(o_aq, o_ak) * attn_scale, col(o_bq, o_bk) * attn_scale,
                                col(o_iq, o_ik) * idx_scale], axis=1).T.astype(BF16)
        w_vt = jnp.concatenate([col(o_av, o_ag), col(o_bv, o_bg)], axis=1).T.astype(BF16)
        w_iwt = jnp.pad(col(o_iw, o_mq).T, ((0, 2 * SUBLANES - N_HEADS), (0, 0))).astype(BF16)
        w_mq = col(o_mq, o_mg).astype(BF16)
        w_g = jnp.concatenate([col(o_ag, o_bq), col(o_bg, o_iq), col(o_mg, o_mix)], axis=1).astype(BF16)
        w_mix = col(o_mix, o_end).astype(BF16)

        h = _rmsnorm(x2d, norm_g[l], BF16, tm)
        k_all = _mm_tok(h, w_k, "rope", BF16, tm, w_k.shape[1], tok_tabs, seq)
        qt_all = _mm_feat(w_qt, h, "rope", BF16, tm, w, feat_tabs, seq)
        vt_all = _mm_feat(w_vt, h, "none", BF16, tm, w)
        iwt = _mm_feat(w_iwt, h, "none", F32, tm, 2 * SUBLANES)

        ya = _moba(qt_all, k_all, vt_all, batch, seq, 0, 0, 0)
        yb = _dsa(qt_all, k_all, vt_all, iwt, batch, seq, dsa_tq, dsa_kc, dsa_kcb)

        mn = _rmsnorm(mem2d, mem_norm_g[l], BF16, n_mem)
        kv = _mm_tok(mn, w_mem_kv[l].astype(BF16), "none", BF16, n_mem, 2 * w)

        x2d = _merge(ya, yb, h, kv, w_mq, w_g, w_mix, w_branch[l].astype(BF16), w_out[l].astype(BF16),
                     x2d, seq, n_mem, tm_merge)

    out = _rmsnorm(x2d, final_g, F32, tm)
    return out.reshape(batch, seq, d)
```

```python
import functools

import jax
import jax.numpy as jnp
import numpy as np
from jax import lax
from jax.experimental import pallas as pl
from jax.experimental.pallas import tpu as pltpu

F32 = jnp.float32
BF16 = jnp.bfloat16

HEAD_DIM = 64
N_HEADS = 8
BRANCH_WIDTH = 512
M_HEADS = 4
M_HEAD_DIM = 128
MOBA_BLOCK = 256
MOBA_TOPK = 3
IDX_TOPK_MAX = 256
ROPE_THETA = 500000.0
ROT_HALF = HEAD_DIM // 4 // 2
RMS_EPS = 1e-6

LANES = 128
SUBLANES = 8
VMEM_LIMIT = 56 * 1024 * 1024
NEG_BIG = -1e30
INT_MIN = -(2 ** 31)
I16_MIN = -(2 ** 15)
KEY_NEG_INF = -0x7F800000


def _cparams(n_axes):
    return pltpu.CompilerParams(dimension_semantics=("arbitrary",) * n_axes,
                                vmem_limit_bytes=VMEM_LIMIT)


def _rmsnorm_kernel(x_ref, g_ref, o_ref):
    xf = x_ref[...]
    y = xf * lax.rsqrt(jnp.mean(xf * xf, axis=-1, keepdims=True) + RMS_EPS)
    o_ref[...] = (y * g_ref[...]).astype(o_ref.dtype)


def _rmsnorm(x2d, g, out_dtype, tm):
    m, d = x2d.shape
    return pl.pallas_call(
        _rmsnorm_kernel,
        grid=(m // tm,),
        in_specs=[pl.BlockSpec((tm, d), lambda i: (i, 0)),
                  pl.BlockSpec((1, d), lambda i: (0, 0))],
        out_specs=pl.BlockSpec((tm, d), lambda i: (i, 0)),
        out_shape=jax.ShapeDtypeStruct((m, d), out_dtype),
        compiler_params=_cparams(1),
        name="rmsnorm",
    )(x2d, g.reshape(1, d))


def _mm_tok_kernel(h_ref, w_ref, *rest, epilogue):
    o_ref = rest[-1]
    acc = jnp.dot(h_ref[...], w_ref[...], preferred_element_type=F32)
    if epilogue == "rope":
        c_ref, s1_ref, s2_ref = rest[:3]
        c, s1, s2 = c_ref[...], s1_ref[...], s2_ref[...]
        for j in range(acc.shape[1] // LANES):
            piece = acc[:, j * LANES:(j + 1) * LANES]
            up = pltpu.roll(piece, LANES - ROT_HALF, 1)
            down = pltpu.roll(piece, ROT_HALF, 1)
            o_ref[:, j * LANES:(j + 1) * LANES] = (piece * c + up * s1 + down * s2).astype(o_ref.dtype)
    else:
        o_ref[...] = acc.astype(o_ref.dtype)


def _mm_tok(h, w, epilogue, out_dtype, tm, tn, rope_tabs=None, seq=None):
    m, k = h.shape
    n = w.shape[1]
    in_specs = [pl.BlockSpec((tm, k), lambda i, j: (i, 0)),
                pl.BlockSpec((k, tn), lambda i, j: (0, j))]
    args = [h, w]
    if epilogue == "rope":
        nt = seq // tm
        for t in rope_tabs:
            in_specs.append(pl.BlockSpec((tm, LANES), lambda i, j: (i % nt, 0)))
            args.append(t)
    return pl.pallas_call(
        functools.partial(_mm_tok_kernel, epilogue=epilogue),
        grid=(m // tm, n // tn),
        in_specs=in_specs,
        out_specs=pl.BlockSpec((tm, tn), lambda i, j: (i, j)),
        out_shape=jax.ShapeDtypeStruct((m, n), out_dtype),
        compiler_params=_cparams(2),
        name="proj_tok_" + epilogue,
    )(*args)


def _mm_feat_kernel(wt_ref, h_ref, *rest, epilogue):
    o_ref = rest[-1]
    acc = lax.dot_general(wt_ref[...], h_ref[...], (((1,), (1,)), ((), ())),
                          preferred_element_type=F32)
    if epilogue == "rope":
        cos_ref, sin_ref = rest[:2]
        c, s = cos_ref[...], sin_ref[...]
        pieces = []
        for hh in range(acc.shape[0] // HEAD_DIM):
            base = hh * HEAD_DIM
            x1 = acc[base:base + ROT_HALF]
            x2 = acc[base + ROT_HALF:base + 2 * ROT_HALF]
            pieces += [x1 * c - x2 * s, x2 * c + x1 * s, acc[base + 2 * ROT_HALF:base + HEAD_DIM]]
        acc = jnp.concatenate(pieces, axis=0)
    o_ref[...] = acc.astype(o_ref.dtype)


def _mm_feat(wt, h, epilogue, out_dtype, tm, tn, rope_tabs=None, seq=None):
    n, k = wt.shape
    m = h.shape[0]
    in_specs = [pl.BlockSpec((tn, k), lambda i, j: (j, 0)),
                pl.BlockSpec((tm, k), lambda i, j: (i, 0))]
    args = [wt, h]
    if epilogue == "rope":
        nt = seq // tm
        for t in rope_tabs:
            in_specs.append(pl.BlockSpec((SUBLANES, tm), lambda i, j: (0, i % nt)))
            args.append(t)
    return pl.pallas_call(
        functools.partial(_mm_feat_kernel, epilogue=epilogue),
        grid=(m // tm, n // tn),
        in_specs=in_specs,
        out_specs=pl.BlockSpec((tn, tm), lambda i, j: (j, i)),
        out_shape=jax.ShapeDtypeStruct((n, m), out_dtype),
        compiler_params=_cparams(2),
        name="proj_feat_" + epilogue,
    )(*args)


N_PAIRS = N_HEADS // 2


def _pair_operand(qt_ref, pair):
    qp = qt_ref[pair * LANES:(pair + 1) * LANES, :].astype(F32)
    row = lax.broadcasted_iota(jnp.int32, qp.shape, 0)
    lo = jnp.where(row < HEAD_DIM, qp, 0.0)
    hi = jnp.where(row >= HEAD_DIM, qp, 0.0)
    return jnp.concatenate([lo, hi], axis=1).astype(BF16)


ONES_ROWS = 16
QK_LOOKAHEAD = 4


OFFSET_SLACK = 1.0 + 2.0 ** -5
L_FLOOR = 2.0 ** -80


def _pv_accumulate(pm, vt_tile, acc_ref, p, scale=None):
    keys = vt_tile.shape[1]
    tq = pm.shape[1] // 2
    ones = jnp.ones((ONES_ROWS, keys), BF16)
    for e in range(2):
        vt_ext = jnp.concatenate([vt_tile[e * HEAD_DIM:(e + 1) * HEAD_DIM], ones], axis=0)
        cols = slice(e * tq, (e + 1) * tq)
        upd = jnp.dot(vt_ext, pm[:, cols], preferred_element_type=F32)
        old = acc_ref[p, e] if scale is None else scale[:, cols] * acc_ref[p, e]
        acc_ref[p, e] = old + upd


def _online_update(s, vt_tile, m_ref, acc_ref, p):
    m = m_ref[p]
    m_new = jnp.maximum(m, jnp.max(s, axis=0, keepdims=True))
    _pv_accumulate(jnp.exp2(s - m_new).astype(BF16), vt_tile, acc_ref, p, scale=jnp.exp2(m - m_new))
    m_ref[p] = m_new


def _offset_update(s, vt_tile, acc_ref, p):
    _pv_accumulate(jnp.exp2(s).astype(BF16), vt_tile, acc_ref, p)


def _flash_tiles(tiles, qk, mask, update):
    ss = [qk(*t) for t in tiles[:QK_LOOKAHEAD]]
    for n, (u, p) in enumerate(tiles):
        if n + QK_LOOKAHEAD < len(tiles):
            ss.append(qk(*tiles[n + QK_LOOKAHEAD]))
        update(mask(ss[n], u, p), u, p)
        ss[n] = None


def _key_norm_bound(k_ref, kb_ref, seq, tq):
    chunk = min(512, seq)
    width = k_ref.shape[1]
    lane = lax.broadcasted_iota(jnp.int32, (width, LANES), 0)
    head = lax.broadcasted_iota(jnp.int32, (width, LANES), 1)
    group = jnp.where((lane >= head * HEAD_DIM) & (lane < (head + 1) * HEAD_DIM), 1.0, 0.0).astype(BF16)

    def body(c, mx):
        kk = k_ref[pl.ds(pl.multiple_of(c * chunk, chunk), chunk), :].astype(F32)
        n2 = jnp.dot((kk * kk).astype(BF16), group, preferred_element_type=F32)
        return jnp.maximum(mx, jnp.max(n2, axis=0, keepdims=True))
    mx = lax.fori_loop(0, seq // chunk, body, jnp.zeros((1, LANES), F32))
    for p in range(N_PAIRS):
        kb_ref[p] = jnp.concatenate([jnp.broadcast_to(mx[:, 2 * p:2 * p + 1], (1, tq)),
                                     jnp.broadcast_to(mx[:, 2 * p + 1:2 * p + 2], (1, tq))], axis=1)


def _offset_rows(qop, kb2):
    qf = qop.astype(F32)
    bound = jnp.sqrt(jnp.sum(qf * qf, axis=0, keepdims=True) * kb2) * OFFSET_SLACK
    row = lax.broadcasted_iota(jnp.int32, (ONES_ROWS, qop.shape[1]), 0)
    return jnp.where(row == 0, -bound, 0.0).astype(BF16)


def _denominator_ok(acc_ref):
    return jnp.min(acc_ref[:, :, HEAD_DIM:HEAD_DIM + 1, :]) > L_FLOOR


def _flash_finish(o_ref, ot_ref, acc_ref, tq):
    for p in range(N_PAIRS):
        for e in range(2):
            lo = p * LANES + e * HEAD_DIM
            ot_ref[lo:lo + HEAD_DIM, :] = acc_ref[p, e, :HEAD_DIM] / acc_ref[p, e, HEAD_DIM:HEAD_DIM + 1]
    o_ref[...] = ot_ref[...].T


MOBA_UNROLL = 2
DSA_UNROLL = 2
SCORE_UNROLL = 4
HI_WINDOW_BITS = 8


def _moba_kernel(qt_ref, k_ref, vt_ref, o_ref, kmean_ref, kb_ref, qop_ref, sel_ref, m_ref, acc_ref,
                 ot_ref, *, seq, topk):
    i = pl.program_id(1)
    blk = MOBA_BLOCK
    nb = seq // blk
    tq = blk

    @pl.when(i == 0)
    def _():
        r = lax.broadcasted_iota(jnp.int32, (nb, seq), 0)
        c = lax.broadcasted_iota(jnp.int32, (nb, seq), 1)
        member = jnp.where((c >= r * blk) & (c < (r + 1) * blk), 1.0, 0.0).astype(BF16)
        ksum = jnp.dot(member, k_ref[...], preferred_element_type=F32)
        kmean_ref[...] = (ksum * (1.0 / blk)).astype(BF16)
        _key_norm_bound(k_ref, kb_ref, seq, tq)

    blk_id = lax.broadcasted_iota(jnp.int32, (nb, 2 * tq), 0)
    past = blk_id < i
    own_start = pl.multiple_of(i * blk, blk)

    for p in range(N_PAIRS):
        lanes = slice(p * LANES, (p + 1) * LANES)
        qop = _pair_operand(qt_ref, p)
        qop_ref[p, :LANES] = qop
        qop_ref[p, LANES:LANES + ONES_ROWS] = _offset_rows(qop, kb_ref[p])
        gate = jnp.dot(kmean_ref[:, lanes], qop, preferred_element_type=F32)
        gate = jnp.where(past, gate, -jnp.inf)
        sel = jnp.zeros((nb, 2 * tq), jnp.bool_)
        for _ in range(topk):
            mx = jnp.max(gate, axis=0, keepdims=True)
            first = jnp.min(jnp.where(gate == mx, blk_id, nb), axis=0, keepdims=True)
            hit = blk_id == first
            sel = sel | hit
            gate = jnp.where(hit, -jnp.inf, gate)
        sel_ref[p] = jnp.where(sel & past, 1.0, 0.0)
        qop_ref[p, LANES + ONES_ROWS:LANES + ONES_ROWS + nb] = jnp.where(sel & past, 0.0, NEG_BIG).astype(BF16)
        qop_ref[p, LANES + ONES_ROWS + nb:] = jnp.zeros((LANES - ONES_ROWS - nb, 2 * tq), BF16)

    def lanes(p):
        return slice(p * LANES, (p + 1) * LANES)

    krow = lax.broadcasted_iota(jnp.int32, (blk, 2 * tq), 0)
    qcol = lax.broadcasted_iota(jnp.int32, (blk, 2 * tq), 1)
    causal = krow <= jnp.where(qcol < tq, qcol, qcol - tq)
    lane = lax.broadcasted_iota(jnp.int32, (blk, LANES), 1)
    ntrip = (i + MOBA_UNROLL - 1) // MOBA_UNROLL

    def attend(fast):
        acc_ref[...] = jnp.zeros(acc_ref.shape, F32)
        if not fast:
            m_ref[...] = jnp.full(m_ref.shape, NEG_BIG, F32)

        def key_operand(start, extra_lane):
            k_tile = k_ref[pl.ds(start, blk), :]
            if not fast:
                return lambda p: k_tile[:, lanes(p)]
            sel_lanes = (lane == 0) if extra_lane is None else ((lane == 0) | (lane == extra_lane))
            extra = jnp.where(sel_lanes, 1.0, 0.0).astype(BF16)
            return lambda p: jnp.concatenate([k_tile[:, lanes(p)], extra], axis=1)

        def query_operand(p):
            return qop_ref[p] if fast else qop_ref[p, :LANES]

        def update(vt_tile):
            if fast:
                return lambda s, u, p: _offset_update(s, vt_tile(u, p), acc_ref, p)
            return lambda s, u, p: _online_update(s, vt_tile(u, p), m_ref, acc_ref, p)

        own_keys = key_operand(own_start, None)
        _flash_tiles(
            [(0, p) for p in range(N_PAIRS)],
            lambda u, p: jnp.dot(own_keys(p), query_operand(p), preferred_element_type=F32),
            lambda s, u, p: jnp.where(causal, s, -jnp.inf),
            update(lambda u, p: vt_ref[lanes(p), pl.ds(own_start, blk)]))

        def body(c, carry):
            js = [c * MOBA_UNROLL + u for u in range(MOBA_UNROLL)]
            starts = [pl.multiple_of(j * blk, blk) for j in js]
            keys = [key_operand(starts[u], ONES_ROWS + js[u]) for u in range(MOBA_UNROLL)]
            _flash_tiles(
                [(u, p) for u in range(MOBA_UNROLL) for p in range(N_PAIRS)],
                lambda u, p: jnp.dot(keys[u](p), query_operand(p), preferred_element_type=F32),
                (lambda s, u, p: s) if fast else
                (lambda s, u, p: jnp.where(sel_ref[p, pl.ds(js[u], 1), :] > 0.5, s, -jnp.inf)),
                update(lambda u, p: vt_ref[lanes(p), pl.ds(starts[u], blk)]))
            return carry

        lax.fori_loop(0, ntrip, body, 0)

    attend(True)

    @pl.when(jnp.logical_not(_denominator_ok(acc_ref)))
    def _():
        attend(False)

    _flash_finish(o_ref, ot_ref, acc_ref, tq)


def _moba(qt_all, k_all, vt_all, batch, seq, q_row_blk, k_col_blk, v_row_blk):
    tq = MOBA_BLOCK
    nq = seq // tq
    nb = seq // MOBA_BLOCK
    topk = min(MOBA_TOPK, nb - 1)
    w = BRANCH_WIDTH
    return pl.pallas_call(
        functools.partial(_moba_kernel, seq=seq, topk=topk),
        grid=(batch, nq),
        in_specs=[
            pl.BlockSpec((w, tq), lambda b, i: (q_row_blk, b * nq + i)),
            pl.BlockSpec((seq, w), lambda b, i: (b, k_col_blk), pipeline_mode=pl.Buffered(1)),
            pl.BlockSpec((w, seq), lambda b, i: (v_row_blk, b), pipeline_mode=pl.Buffered(1)),
        ],
        out_specs=pl.BlockSpec((tq, w), lambda b, i: (b * nq + i, 0)),
        out_shape=jax.ShapeDtypeStruct((batch * seq, w), F32),
        scratch_shapes=[pltpu.VMEM((nb, w), BF16),
                        pltpu.VMEM((N_PAIRS, 1, 2 * tq), F32),
                        pltpu.VMEM((N_PAIRS, 2 * LANES, 2 * tq), BF16),
                        pltpu.VMEM((N_PAIRS, nb, 2 * tq), F32),
                        pltpu.VMEM((N_PAIRS, 1, 2 * tq), F32),
                        pltpu.VMEM((N_PAIRS, 2, HEAD_DIM + ONES_ROWS, tq), F32),
                        pltpu.VMEM((w, tq), F32)],
        compiler_params=_cparams(2),
        name="moba",
    )(qt_all, k_all, vt_all)


def _ordered_key(bits):
    return jnp.where(bits < 0, jnp.int32(INT_MIN) - bits, bits)


def _tree_sum(parts):
    while len(parts) > 1:
        parts = [parts[a] + parts[a + 1] for a in range(0, len(parts) - 1, 2)] + (
            [parts[-1]] if len(parts) % 2 else [])
    return parts[0]


def _dsa_kernel(qit_ref, ki_ref, wt_ref, qt_ref, k_ref, vt_ref, o_ref,
                score_ref, hi_ref, lo_ref, bias_ref, kb_ref, qop_ref, m_ref, acc_ref, ot_ref,
                *, tq, kc, kcb, n_sel):
    i = pl.program_id(1)
    nch = (i * tq + tq) // kc
    nchb = (i * tq + tq + kcb - 1) // kcb
    ntrip = (nch + DSA_UNROLL - 1) // DSA_UNROLL
    krow = lax.broadcasted_iota(jnp.int32, (kc, tq), 0)
    qpos = i * tq + lax.broadcasted_iota(jnp.int32, (kc, tq), 1)
    n_sel_f = float(n_sel)

    @pl.when(i == 0)
    def _():
        _key_norm_bound(k_ref, kb_ref, k_ref.shape[0], tq)

    for p in range(N_PAIRS):
        qop_ref[p, :LANES] = _pair_operand(qit_ref, p)
    w_rows = [wt_ref[h:h + 1, :] * (N_HEADS ** -0.5) for h in range(N_HEADS)]

    def score_body(c, carry, masked, unroll):
        starts = [pl.multiple_of((c * unroll + u) * kc, kc) for u in range(unroll)]
        rs = [[jnp.dot(ki_ref[pl.ds(st, kc), :], qop_ref[p, :LANES], preferred_element_type=F32)
               for p in range(N_PAIRS)] for st in starts]
        for u, start in enumerate(starts):
            acc = jnp.zeros((kc, tq), F32)
            for p in range(N_PAIRS):
                acc = acc + jnp.maximum(rs[u][p][:, :tq], 0.0) * w_rows[2 * p]
                acc = acc + jnp.maximum(rs[u][p][:, tq:], 0.0) * w_rows[2 * p + 1]
            sc = jnp.where(start + krow <= qpos, acc, -jnp.inf) if masked else acc
            score_ref[pl.ds(start, kc), :] = sc
            key = _ordered_key(lax.bitcast_convert_type(sc, jnp.int32))
            hi_ref[pl.ds(start, kc), :] = jnp.right_shift(key, 16).astype(jnp.int16)
            lo_ref[pl.ds(start, kc), :] = (key ^ jnp.int32(0x8000)).astype(jnp.int16)
        return carry

    n_full = (i * tq + 1) // kc
    n_fast = n_full // SCORE_UNROLL
    lax.fori_loop(0, n_fast, functools.partial(score_body, masked=False, unroll=SCORE_UNROLL), 0)
    lax.fori_loop(n_fast * SCORE_UNROLL, n_full, functools.partial(score_body, masked=False, unroll=1), 0)
    lax.fori_loop(n_full, nch, functools.partial(score_body, masked=True, unroll=1), 0)

    @pl.when(nch * kc < nchb * kcb)
    def _():
        pad_rows = pl.ds(pl.multiple_of(nch * kc, kc), kcb - kc)
        score_ref[pad_rows, :] = jnp.full((kcb - kc, tq), -jnp.inf, F32)
        hi_ref[pad_rows, :] = jnp.full((kcb - kc, tq), I16_MIN, jnp.int16)
        lo_ref[pad_rows, :] = jnp.full((kcb - kc, tq), I16_MIN, jnp.int16)

    def count16(ref, pred):
        n_acc = 4
        rows = 2 * SUBLANES

        def body(c, accs):
            x = ref[pl.ds(pl.multiple_of(c * kcb, kcb), kcb), :]
            accs = list(accs)
            for r in range(kcb // rows):
                hit = jnp.where(pred(x[r * rows:(r + 1) * rows]), jnp.int16(1), jnp.int16(0))
                accs[r % n_acc] = accs[r % n_acc] + hit
            return tuple(accs)
        accs = lax.fori_loop(0, nchb, body, tuple(jnp.zeros((rows, tq), jnp.int16) for _ in range(n_acc)))
        tot = _tree_sum(list(accs)).astype(jnp.int32).astype(F32)
        return jnp.sum(tot, axis=0, keepdims=True)

    def bisect16(ref, target, start, count_at_start, bits):
        def body(it, carry):
            t, c_ge = carry
            cand = t + jnp.left_shift(jnp.int32(1), bits - 1 - it)
            cand16 = cand.astype(jnp.int16)
            cnt = count16(ref, lambda x: x >= cand16)
            ok = cnt >= target
            return jnp.where(ok, cand, t), jnp.where(ok, cnt, c_ge)
        return lax.fori_loop(0, bits, body, (start, count_at_start))

    def max16(ref):
        n_acc = 4
        rows = 2 * SUBLANES

        def body(c, accs):
            x = ref[pl.ds(pl.multiple_of(c * kcb, kcb), kcb), :]
            accs = list(accs)
            for r in range(kcb // rows):
                xs = x[r * rows:(r + 1) * rows]
                accs[r % n_acc] = jnp.where(xs > accs[r % n_acc], xs, accs[r % n_acc])
            return tuple(accs)
        accs = lax.fori_loop(0, nchb, body,
                             tuple(jnp.full((rows, tq), I16_MIN, jnp.int16) for _ in range(n_acc)))
        wide = [a.astype(jnp.int32) for a in accs]
        top = jnp.maximum(jnp.maximum(wide[0], wide[1]), jnp.maximum(wide[2], wide[3]))
        return jnp.max(top, axis=0, keepdims=True)

    i16_min_row = jnp.full((1, tq), I16_MIN, jnp.int32)
    rows_seen = jnp.full((1, tq), (nchb * kcb).astype(F32))
    win_start = jnp.maximum(max16(hi_ref) - (2 ** HI_WINDOW_BITS - 1), I16_MIN)
    win_start16 = win_start.astype(jnp.int16)
    win_count = count16(hi_ref, lambda x: x >= win_start16)
    t_hi, c_ge_hi = lax.cond(
        jnp.min(win_count) >= n_sel_f,
        lambda: bisect16(hi_ref, n_sel_f, win_start, win_count, HI_WINDOW_BITS),
        lambda: bisect16(hi_ref, n_sel_f, i16_min_row, rows_seen, 16))
    t_hi16 = t_hi.astype(jnp.int16)
    c_gt_hi = count16(hi_ref, lambda x: x > t_hi16)

    def low_body(c, carry):
        rows = pl.ds(pl.multiple_of(c * kcb, kcb), kcb)
        lo_ref[rows, :] = jnp.where(hi_ref[rows, :] == t_hi16, lo_ref[rows, :], jnp.int16(I16_MIN))
        return carry

    lax.fori_loop(0, nchb, low_body, 0)
    t_lo, c_ge_lo = bisect16(lo_ref, n_sel_f - c_gt_hi, i16_min_row, c_ge_hi - c_gt_hi, 16)
    n_ge = c_gt_hi + c_ge_lo
    key_t = t_hi * 65536 + (t_lo + 32768)
    short = key_t <= KEY_NEG_INF
    thr = jnp.where(short, -jnp.inf, lax.bitcast_convert_type(_ordered_key(key_t), F32))
    has_tie = jnp.max(jnp.where(short, 0.0, n_ge - n_sel_f)) > 0.0

    @pl.when(jnp.logical_not(has_tie))
    def _():
        def body(c, carry):
            start = pl.multiple_of(c * kc, kc)
            sc = score_ref[pl.ds(start, kc), :]
            keep = (sc >= thr) & (start + krow <= qpos)
            bias_ref[pl.ds(start, kc), :] = jnp.where(keep, 0.0, NEG_BIG)
            return carry
        lax.fori_loop(0, nch, body, 0)

    @pl.when(has_tie)
    def _():
        lower = (lax.broadcasted_iota(jnp.int32, (kc, kc), 1)
                 < lax.broadcasted_iota(jnp.int32, (kc, kc), 0))
        lower = jnp.where(lower, 1.0, 0.0).astype(BF16)

        def gt_body(c, cnt):
            sc = score_ref[pl.ds(pl.multiple_of(c * kc, kc), kc), :]
            return cnt + jnp.sum(jnp.where(sc > thr, 1.0, 0.0), axis=0, keepdims=True)
        n_gt = lax.fori_loop(0, nch, gt_body, jnp.zeros((1, tq), F32))
        need = n_sel_f - n_gt

        def body(c, seen):
            start = pl.multiple_of(c * kc, kc)
            sc = score_ref[pl.ds(start, kc), :]
            eq = jnp.where(sc == thr, 1.0, 0.0)
            rank = seen + jnp.dot(lower, eq.astype(BF16), preferred_element_type=F32)
            keep = (sc > thr) | ((sc == thr) & (rank < need))
            keep = keep & (start + krow <= qpos)
            bias_ref[pl.ds(start, kc), :] = jnp.where(keep, 0.0, NEG_BIG)
            return seen + jnp.sum(eq, axis=0, keepdims=True)
        lax.fori_loop(0, nch, body, jnp.zeros((1, tq), F32))

    for p in range(N_PAIRS):
        qop = _pair_operand(qt_ref, p)
        qop_ref[p, :LANES] = qop
        qop_ref[p, LANES:LANES + ONES_ROWS] = _offset_rows(qop, kb_ref[p])
        qop_ref[p, LANES + ONES_ROWS:] = jnp.zeros((LANES - ONES_ROWS, 2 * tq), BF16)

    @pl.when(nch < ntrip * DSA_UNROLL)
    def _():
        pad = (DSA_UNROLL - 1) * kc
        bias_ref[pl.ds(pl.multiple_of(nch * kc, kc), pad), :] = jnp.full((pad, tq), NEG_BIG, F32)

    def lanes(p):
        return slice(p * LANES, (p + 1) * LANES)

    def attend(fast):
        acc_ref[...] = jnp.zeros(acc_ref.shape, F32)
        if not fast:
            m_ref[...] = jnp.full(m_ref.shape, NEG_BIG, F32)
        ones = jnp.ones((kc, LANES), BF16)

        def att_body(c, carry):
            starts = [pl.multiple_of((c * DSA_UNROLL + u) * kc, kc) for u in range(DSA_UNROLL)]

            def qk(u, p):
                k_tile = k_ref[pl.ds(starts[u], kc), lanes(p)]
                if fast:
                    return jnp.dot(jnp.concatenate([k_tile, ones], axis=1), qop_ref[p],
                                   preferred_element_type=F32)
                return jnp.dot(k_tile, qop_ref[p, :LANES], preferred_element_type=F32)

            def add_bias(s, u, p):
                bias = bias_ref[pl.ds(starts[u], kc), :]
                return jnp.concatenate([s[:, :tq] + bias, s[:, tq:] + bias], axis=1)

            def update(s, u, p):
                vt_tile = vt_ref[lanes(p), pl.ds(starts[u], kc)]
                if fast:
                    _offset_update(s, vt_tile, acc_ref, p)
                else:
                    _online_update(s, vt_tile, m_ref, acc_ref, p)

            _flash_tiles([(u, p) for u in range(DSA_UNROLL) for p in range(N_PAIRS)], qk, add_bias, update)
            return carry

        lax.fori_loop(0, ntrip, att_body, 0)

    attend(True)

    @pl.when(jnp.logical_not(_denominator_ok(acc_ref)))
    def _():
        attend(False)

    _flash_finish(o_ref, ot_ref, acc_ref, tq)


def _dsa(qt_all, k_all, vt_all, wt, batch, seq, tq, kc, kcb):
    nq = seq // tq
    w = BRANCH_WIDTH
    n_sel = min(IDX_TOPK_MAX, seq // 4)
    assert tq % kc == 0 and kcb % kc == 0 and kcb > kc and seq % kcb == 0
    ki_col_blk = (2 * w) // LANES
    return pl.pallas_call(
        functools.partial(_dsa_kernel, tq=tq, kc=kc, kcb=kcb, n_sel=n_sel),
        grid=(batch, nq),
        in_specs=[
            pl.BlockSpec((w, tq), lambda b, i: (2, b * nq + i)),
            pl.BlockSpec((seq, LANES), lambda b, i: (b, ki_col_blk), pipeline_mode=pl.Buffered(1)),
            pl.BlockSpec((2 * SUBLANES, tq), lambda b, i: (0, b * nq + i)),
            pl.BlockSpec((w, tq), lambda b, i: (1, b * nq + i)),
            pl.BlockSpec((seq, w), lambda b, i: (b, 1), pipeline_mode=pl.Buffered(1)),
            pl.BlockSpec((w, seq), lambda b, i: (1, b), pipeline_mode=pl.Buffered(1)),
        ],
        out_specs=pl.BlockSpec((tq, w), lambda b, i: (b * nq + i, 0)),
        out_shape=jax.ShapeDtypeStruct((batch * seq, w), F32),
        scratch_shapes=[pltpu.VMEM((seq, tq), F32),
                        pltpu.VMEM((seq, tq), jnp.int16),
                        pltpu.VMEM((seq, tq), jnp.int16),
                        pltpu.VMEM((seq, tq), F32),
                        pltpu.VMEM((N_PAIRS, 1, 2 * tq), F32),
                        pltpu.VMEM((N_PAIRS, 2 * LANES, 2 * tq), BF16),
                        pltpu.VMEM((N_PAIRS, 1, 2 * tq), F32),
                        pltpu.VMEM((N_PAIRS, 2, HEAD_DIM + ONES_ROWS, tq), F32),
                        pltpu.VMEM((w, tq), F32)],
        compiler_params=_cparams(2),
        name="dsa",
    )(qt_all, k_all, wt, qt_all, k_all, vt_all)


def _merge_kernel(ya_ref, yb_ref, h_ref, kv_ref, wmq_ref, wg_ref, wmix_ref, wb_ref, wo_ref, x_ref, gn_ref,
                  *out_refs, last):
    w = BRANCH_WIDTH
    d = x_ref.shape[1]
    h = h_ref[...]
    mq = jnp.dot(h, wmq_ref[...], preferred_element_type=F32).astype(BF16)
    ym_heads = []
    for hd in range(M_HEADS):
        cols = slice(hd * M_HEAD_DIM, (hd + 1) * M_HEAD_DIM)
        km = kv_ref[:, cols]
        vm = kv_ref[:, w + hd * M_HEAD_DIM:w + (hd + 1) * M_HEAD_DIM]
        sc = lax.dot_general(mq[:, cols], km, (((1,), (1,)), ((), ())),
                             preferred_element_type=F32) * (M_HEAD_DIM ** -0.5)
        sc = sc - jnp.max(sc, axis=-1, keepdims=True)
        e = jnp.exp(sc)
        p = e / jnp.sum(e, axis=-1, keepdims=True)
        ym_heads.append(jnp.dot(p.astype(BF16), vm, preferred_element_type=F32))
    ym = jnp.concatenate(ym_heads, axis=-1)
    ys = (ya_ref[...], yb_ref[...], ym)
    merged = None
    for n in range(3):
        g = jnp.dot(h, wg_ref[:, n * w:(n + 1) * w], preferred_element_type=F32)
        y = (ys[n] * (g * jax.nn.sigmoid(g))).astype(BF16)
        up = jnp.dot(y, wb_ref[n], preferred_element_type=F32)
        mix = jax.nn.sigmoid(jnp.dot(h, wmix_ref[:, n * d:(n + 1) * d], preferred_element_type=F32))
        term = mix * up
        merged = term if merged is None else merged + term
    x_new = x_ref[...] + jnp.dot(merged.astype(BF16), wo_ref[...], preferred_element_type=F32)
    y = x_new * lax.rsqrt(jnp.mean(x_new * x_new, axis=-1, keepdims=True) + RMS_EPS) * gn_ref[...]
    if last:
        out_refs[0][...] = y
    else:
        out_refs[0][...] = x_new
        out_refs[1][...] = y.astype(BF16)


def _merge(ya, yb, h, kv, w_mq, w_g, w_mix, wb, wo, x2d, g_next, last, seq, n_mem, tm):
    m, d = x2d.shape
    w = BRANCH_WIDTH
    nt = seq // tm

    def resident(shape):
        return pl.BlockSpec(shape, lambda i: (0,) * len(shape), pipeline_mode=pl.Buffered(1))

    row_tile = pl.BlockSpec((tm, d), lambda i: (i, 0))
    out_f32 = jax.ShapeDtypeStruct((m, d), F32)
    return pl.pallas_call(
        functools.partial(_merge_kernel, last=last),
        grid=(m // tm,),
        in_specs=[
            pl.BlockSpec((tm, w), lambda i: (i, 0)),
            pl.BlockSpec((tm, w), lambda i: (i, 0)),
            pl.BlockSpec((tm, d), lambda i: (i, 0)),
            pl.BlockSpec((n_mem, 2 * w), lambda i: (i // nt, 0)),
            resident((d, w)),
            resident((d, 3 * w)),
            resident((d, 3 * d)),
            resident((3, w, d)),
            resident((d, d)),
            row_tile,
            resident((1, d)),
        ],
        out_specs=row_tile if last else (row_tile, row_tile),
        out_shape=out_f32 if last else (out_f32, jax.ShapeDtypeStruct((m, d), BF16)),
        compiler_params=_cparams(1),
        name="merge",
    )(ya, yb, h, kv, w_mq, w_g, w_mix, wb, wo, x2d, g_next.reshape(1, d))


def _rope_tables(seq):
    inv_freq = ROPE_THETA ** (-jnp.arange(ROT_HALF, dtype=F32) / ROT_HALF)
    ang = jnp.arange(seq, dtype=jnp.int32).astype(F32)[:, None] * inv_freq[None, :]
    cos, sin = jnp.cos(ang), jnp.sin(ang)
    r = np.arange(LANES) % HEAD_DIM
    f = r % ROT_HALF
    first = jnp.asarray(r < ROT_HALF)
    second = jnp.asarray((r >= ROT_HALF) & (r < 2 * ROT_HALF))
    rot = jnp.asarray(r < 2 * ROT_HALF)
    c_tok = jnp.where(rot[None, :], cos[:, f], 1.0)
    s1_tok = jnp.where(first[None, :], -sin[:, f], 0.0)
    s2_tok = jnp.where(second[None, :], sin[:, f], 0.0)
    return (c_tok, s1_tok, s2_tok), (cos.T, sin.T)


def kernel(x, mem, norm_g, w_in, mem_norm_g, w_mem_kv, w_branch, w_out, final_g):
    batch, seq, d = x.shape
    n_mem = mem.shape[1]
    depth = norm_g.shape[0]
    w = BRANCH_WIDTH
    m = batch * seq
    assert seq % MOBA_BLOCK == 0 and d % LANES == 0

    tm = min(1024, seq)
    tm_merge = min(512, seq)
    dsa_tq = 256
    dsa_kc = 256
    dsa_kcb = min(512, seq)

    tok_tabs, feat_tabs = _rope_tables(seq)
    offs = np.cumsum([0, w, w, w, w, w, w, w, w, N_HEADS * HEAD_DIM, HEAD_DIM, N_HEADS, w, w, 3 * d])
    (o_aq, o_ak, o_av, o_ag, o_bq, o_bk, o_bv, o_bg, o_iq, o_ik, o_iw, o_mq, o_mg, o_mix, o_end) = offs
    attn_scale = HEAD_DIM ** -0.5 * float(np.log2(np.e))
    idx_scale = HEAD_DIM ** -0.5

    x2d = x.reshape(m, d)
    mem2d = mem.reshape(batch * n_mem, d)
    h = _rmsnorm(x2d, norm_g[0], BF16, tm)
    for l in range(depth):
        wl = w_in[l]
        col = lambda a, b: wl[:, a:b]
        w_k = jnp.concatenate([col(o_ak, o_av), col(o_bk, o_bv), col(o_ik, o_iw), col(o_ik, o_iw)],
                              axis=1).astype(BF16)
        w_qt = jnp.concatenate([col(o_aq, o_ak) * attn_scale, col(o_bq, o_bk) * attn_scale,
                                col(o_iq, o_ik) * idx_scale], axis=1).T.astype(BF16)
        w_vt = jnp.concatenate([col(o_av, o_ag), col(o_bv, o_bg)], axis=1).T.astype(BF16)
        w_iwt = jnp.pad(col(o_iw, o_mq).T, ((0, 2 * SUBLANES - N_HEADS), (0, 0))).astype(BF16)
        w_mq = col(o_mq, o_mg).astype(BF16)
        w_g = jnp.concatenate([col(o_ag, o_bq), col(o_bg, o_iq), col(o_mg, o_mix)], axis=1).astype(BF16)
        w_mix = col(o_mix, o_end).astype(BF16)

        k_all = _mm_tok(h, w_k, "rope", BF16, tm, w_k.shape[1], tok_tabs, seq)
        qt_all = _mm_feat(w_qt, h, "rope", BF16, tm, w, feat_tabs, seq)
        vt_all = _mm_feat(w_vt, h, "none", BF16, tm, w)
        iwt = _mm_feat(w_iwt, h, "none", F32, tm, 2 * SUBLANES)

        ya = _moba(qt_all, k_all, vt_all, batch, seq, 0, 0, 0)
        yb = _dsa(qt_all, k_all, vt_all, iwt, batch, seq, dsa_tq, dsa_kc, dsa_kcb)

        mn = _rmsnorm(mem2d, mem_norm_g[l], BF16, n_mem)
        kv = _mm_tok(mn, w_mem_kv[l].astype(BF16), "none", BF16, n_mem, 2 * w)

        last = l == depth - 1
        res = _merge(ya, yb, h, kv, w_mq, w_g, w_mix, w_branch[l].astype(BF16), w_out[l].astype(BF16),
                     x2d, final_g if last else norm_g[l + 1], last, seq, n_mem, tm_merge)
        if not last:
            x2d, h = res

    return res.reshape(batch, seq, d)
```

```python
import functools

import jax
import jax.numpy as jnp
import numpy as np
from jax import lax
from jax.experimental import pallas as pl
from jax.experimental.pallas import tpu as pltpu

F32 = jnp.float32
BF16 = jnp.bfloat16

HEAD_DIM = 64
N_HEADS = 8
BRANCH_WIDTH = 512
M_HEADS = 4
M_HEAD_DIM = 128
MOBA_BLOCK = 256
MOBA_TOPK = 3
IDX_TOPK_MAX = 256
ROPE_THETA = 500000.0
ROT_HALF = HEAD_DIM // 4 // 2
RMS_EPS = 1e-6

LANES = 128
SUBLANES = 8
VMEM_LIMIT = 56 * 1024 * 1024
NEG_BIG = -1e30
FLT_MAX = float(np.finfo(np.float32).max)
INT_MIN = -(2 ** 31)
I16_MIN = -(2 ** 15)
KEY_NEG_INF = -0x7F800000


def _cparams(n_axes):
    return pltpu.CompilerParams(dimension_semantics=("arbitrary",) * n_axes,
                                vmem_limit_bytes=VMEM_LIMIT)


def _rmsnorm_kernel(x_ref, g_ref, o_ref):
    xf = x_ref[...]
    y = xf * lax.rsqrt(jnp.mean(xf * xf, axis=-1, keepdims=True) + RMS_EPS)
    o_ref[...] = (y * g_ref[...]).astype(o_ref.dtype)


def _rmsnorm(x2d, g, out_dtype, tm):
    m, d = x2d.shape
    return pl.pallas_call(
        _rmsnorm_kernel,
        grid=(m // tm,),
        in_specs=[pl.BlockSpec((tm, d), lambda i: (i, 0)),
                  pl.BlockSpec((1, d), lambda i: (0, 0))],
        out_specs=pl.BlockSpec((tm, d), lambda i: (i, 0)),
        out_shape=jax.ShapeDtypeStruct((m, d), out_dtype),
        compiler_params=_cparams(1),
        name="rmsnorm",
    )(x2d, g.reshape(1, d))


def _mm_tok_kernel(h_ref, w_ref, *rest, epilogue):
    o_ref = rest[-1]
    acc = jnp.dot(h_ref[...], w_ref[...], preferred_element_type=F32)
    if epilogue == "rope":
        c_ref, s1_ref, s2_ref = rest[:3]
        c, s1, s2 = c_ref[...], s1_ref[...], s2_ref[...]
        for j in range(acc.shape[1] // LANES):
            piece = acc[:, j * LANES:(j + 1) * LANES]
            up = pltpu.roll(piece, LANES - ROT_HALF, 1)
            down = pltpu.roll(piece, ROT_HALF, 1)
            o_ref[:, j * LANES:(j + 1) * LANES] = (piece * c + up * s1 + down * s2).astype(o_ref.dtype)
    else:
        o_ref[...] = acc.astype(o_ref.dtype)


def _mm_tok(h, w, epilogue, out_dtype, tm, tn, rope_tabs=None, seq=None):
    m, k = h.shape
    n = w.shape[1]
    in_specs = [pl.BlockSpec((tm, k), lambda i, j: (i, 0)),
                pl.BlockSpec((k, tn), lambda i, j: (0, j))]
    args = [h, w]
    if epilogue == "rope":
        nt = seq // tm
        for t in rope_tabs:
            in_specs.append(pl.BlockSpec((tm, LANES), lambda i, j: (i % nt, 0)))
            args.append(t)
    return pl.pallas_call(
        functools.partial(_mm_tok_kernel, epilogue=epilogue),
        grid=(m // tm, n // tn),
        in_specs=in_specs,
        out_specs=pl.BlockSpec((tm, tn), lambda i, j: (i, j)),
        out_shape=jax.ShapeDtypeStruct((m, n), out_dtype),
        compiler_params=_cparams(2),
        name="proj_tok_" + epilogue,
    )(*args)


def _mm_feat_kernel(wt_ref, h_ref, *rest, epilogue):
    o_ref = rest[-1]
    acc = lax.dot_general(wt_ref[...], h_ref[...], (((1,), (1,)), ((), ())),
                          preferred_element_type=F32)
    if epilogue == "rope":
        cos_ref, sin_ref = rest[:2]
        c, s = cos_ref[...], sin_ref[...]
        pieces = []
        for hh in range(acc.shape[0] // HEAD_DIM):
            base = hh * HEAD_DIM
            x1 = acc[base:base + ROT_HALF]
            x2 = acc[base + ROT_HALF:base + 2 * ROT_HALF]
            pieces += [x1 * c - x2 * s, x2 * c + x1 * s, acc[base + 2 * ROT_HALF:base + HEAD_DIM]]
        acc = jnp.concatenate(pieces, axis=0)
    o_ref[...] = acc.astype(o_ref.dtype)


def _mm_feat(wt, h, epilogue, out_dtype, tm, tn, rope_tabs=None, seq=None):
    n, k = wt.shape
    m = h.shape[0]
    in_specs = [pl.BlockSpec((tn, k), lambda i, j: (j, 0)),
                pl.BlockSpec((tm, k), lambda i, j: (i, 0))]
    args = [wt, h]
    if epilogue == "rope":
        nt = seq // tm
        for t in rope_tabs:
            in_specs.append(pl.BlockSpec((SUBLANES, tm), lambda i, j: (0, i % nt)))
            args.append(t)
    return pl.pallas_call(
        functools.partial(_mm_feat_kernel, epilogue=epilogue),
        grid=(m // tm, n // tn),
        in_specs=in_specs,
        out_specs=pl.BlockSpec((tn, tm), lambda i, j: (j, i)),
        out_shape=jax.ShapeDtypeStruct((n, m), out_dtype),
        compiler_params=_cparams(2),
        name="proj_feat_" + epilogue,
    )(*args)


N_PAIRS = N_HEADS // 2


def _pair_operand(qt_ref, pair):
    qp = qt_ref[pair * LANES:(pair + 1) * LANES, :].astype(F32)
    row = lax.broadcasted_iota(jnp.int32, qp.shape, 0)
    lo = jnp.where(row < HEAD_DIM, qp, 0.0)
    hi = jnp.where(row >= HEAD_DIM, qp, 0.0)
    return jnp.concatenate([lo, hi], axis=1).astype(BF16)


ONES_ROWS = 16
QK_LOOKAHEAD = 4


OFFSET_SLACK = 1.0 + 2.0 ** -5
L_FLOOR = 2.0 ** -80


def _pv_accumulate(pm, vt_tile, acc_ref, p, scale=None):
    keys = vt_tile.shape[1]
    tq = pm.shape[1] // 2
    ones = jnp.ones((ONES_ROWS, keys), BF16)
    for e in range(2):
        vt_ext = jnp.concatenate([vt_tile[e * HEAD_DIM:(e + 1) * HEAD_DIM], ones], axis=0)
        cols = slice(e * tq, (e + 1) * tq)
        upd = jnp.dot(vt_ext, pm[:, cols], preferred_element_type=F32)
        old = acc_ref[p, e] if scale is None else scale[:, cols] * acc_ref[p, e]
        acc_ref[p, e] = old + upd


def _online_update(s, vt_tile, m_ref, acc_ref, p):
    m = m_ref[p]
    m_new = jnp.maximum(m, jnp.max(s, axis=0, keepdims=True))
    _pv_accumulate(jnp.exp2(s - m_new).astype(BF16), vt_tile, acc_ref, p, scale=jnp.exp2(m - m_new))
    m_ref[p] = m_new


def _offset_update(s, vt_tile, acc_ref, p):
    _pv_accumulate(jnp.exp2(s).astype(BF16), vt_tile, acc_ref, p)


def _flash_tiles(tiles, qk, mask, update):
    ss = [qk(*t) for t in tiles[:QK_LOOKAHEAD]]
    for n, (u, p) in enumerate(tiles):
        if n + QK_LOOKAHEAD < len(tiles):
            ss.append(qk(*tiles[n + QK_LOOKAHEAD]))
        update(mask(ss[n], u, p), u, p)
        ss[n] = None


def _key_norm_bound(k_ref, kb_ref, seq, tq):
    chunk = min(512, seq)
    width = k_ref.shape[1]
    lane = lax.broadcasted_iota(jnp.int32, (width, LANES), 0)
    head = lax.broadcasted_iota(jnp.int32, (width, LANES), 1)
    group = jnp.where((lane >= head * HEAD_DIM) & (lane < (head + 1) * HEAD_DIM), 1.0, 0.0).astype(BF16)

    def body(c, mx):
        kk = k_ref[pl.ds(pl.multiple_of(c * chunk, chunk), chunk), :].astype(F32)
        n2 = jnp.dot((kk * kk).astype(BF16), group, preferred_element_type=F32)
        return jnp.maximum(mx, jnp.max(n2, axis=0, keepdims=True))
    mx = lax.fori_loop(0, seq // chunk, body, jnp.zeros((1, LANES), F32))
    for p in range(N_PAIRS):
        kb_ref[p] = jnp.concatenate([jnp.broadcast_to(mx[:, 2 * p:2 * p + 1], (1, tq)),
                                     jnp.broadcast_to(mx[:, 2 * p + 1:2 * p + 2], (1, tq))], axis=1)


def _offset_rows(qop, kb2):
    qf = qop.astype(F32)
    bound = jnp.sqrt(jnp.sum(qf * qf, axis=0, keepdims=True) * kb2) * OFFSET_SLACK
    row = lax.broadcasted_iota(jnp.int32, (ONES_ROWS, qop.shape[1]), 0)
    return jnp.where(row == 0, -bound, 0.0).astype(BF16)


def _denominator_ok(acc_ref):
    return jnp.min(acc_ref[:, :, HEAD_DIM:HEAD_DIM + 1, :]) > L_FLOOR


def _flash_finish(o_ref, ot_ref, acc_ref, tq):
    for p in range(N_PAIRS):
        for e in range(2):
            lo = p * LANES + e * HEAD_DIM
            ot_ref[lo:lo + HEAD_DIM, :] = acc_ref[p, e, :HEAD_DIM] / acc_ref[p, e, HEAD_DIM:HEAD_DIM + 1]
    o_ref[...] = ot_ref[...].T


MOBA_UNROLL = 2
DSA_UNROLL = 2
SCORE_UNROLL = 4


def _moba_kernel(qt_ref, k_ref, vt_ref, o_ref, kmean_ref, kb_ref, qop_ref, sel_ref, m_ref, acc_ref,
                 ot_ref, *, seq, topk):
    i = pl.program_id(1)
    blk = MOBA_BLOCK
    nb = seq // blk
    tq = blk

    @pl.when(i == 0)
    def _():
        r = lax.broadcasted_iota(jnp.int32, (nb, seq), 0)
        c = lax.broadcasted_iota(jnp.int32, (nb, seq), 1)
        member = jnp.where((c >= r * blk) & (c < (r + 1) * blk), 1.0, 0.0).astype(BF16)
        ksum = jnp.dot(member, k_ref[...], preferred_element_type=F32)
        kmean_ref[...] = (ksum * (1.0 / blk)).astype(BF16)
        _key_norm_bound(k_ref, kb_ref, seq, tq)

    blk_id = lax.broadcasted_iota(jnp.int32, (nb, 2 * tq), 0)
    past = blk_id < i
    own_start = pl.multiple_of(i * blk, blk)

    for p in range(N_PAIRS):
        lanes = slice(p * LANES, (p + 1) * LANES)
        qop = _pair_operand(qt_ref, p)
        qop_ref[p, :LANES] = qop
        qop_ref[p, LANES:LANES + ONES_ROWS] = _offset_rows(qop, kb_ref[p])
        gate = jnp.dot(kmean_ref[:, lanes], qop, preferred_element_type=F32)
        gate = jnp.where(past, gate, -jnp.inf)
        sel = jnp.zeros((nb, 2 * tq), jnp.bool_)
        for _ in range(topk):
            mx = jnp.max(gate, axis=0, keepdims=True)
            first = jnp.min(jnp.where(gate == mx, blk_id, nb), axis=0, keepdims=True)
            hit = blk_id == first
            sel = sel | hit
            gate = jnp.where(hit, -jnp.inf, gate)
        sel_ref[p] = jnp.where(sel & past, 1.0, 0.0)
        qop_ref[p, LANES + ONES_ROWS:LANES + ONES_ROWS + nb] = jnp.where(sel & past, 0.0, NEG_BIG).astype(BF16)
        qop_ref[p, LANES + ONES_ROWS + nb:] = jnp.zeros((LANES - ONES_ROWS - nb, 2 * tq), BF16)

    def lanes(p):
        return slice(p * LANES, (p + 1) * LANES)

    krow = lax.broadcasted_iota(jnp.int32, (blk, 2 * tq), 0)
    qcol = lax.broadcasted_iota(jnp.int32, (blk, 2 * tq), 1)
    causal = krow <= jnp.where(qcol < tq, qcol, qcol - tq)
    lane = lax.broadcasted_iota(jnp.int32, (blk, LANES), 1)
    ntrip = (i + MOBA_UNROLL - 1) // MOBA_UNROLL

    def attend(fast):
        acc_ref[...] = jnp.zeros(acc_ref.shape, F32)
        if not fast:
            m_ref[...] = jnp.full(m_ref.shape, NEG_BIG, F32)

        def key_operand(start, extra_lane):
            k_tile = k_ref[pl.ds(start, blk), :]
            if not fast:
                return lambda p: k_tile[:, lanes(p)]
            sel_lanes = (lane == 0) if extra_lane is None else ((lane == 0) | (lane == extra_lane))
            extra = jnp.where(sel_lanes, 1.0, 0.0).astype(BF16)
            return lambda p: jnp.concatenate([k_tile[:, lanes(p)], extra], axis=1)

        def query_operand(p):
            return qop_ref[p] if fast else qop_ref[p, :LANES]

        def update(vt_tile):
            if fast:
                return lambda s, u, p: _offset_update(s, vt_tile(u, p), acc_ref, p)
            return lambda s, u, p: _online_update(s, vt_tile(u, p), m_ref, acc_ref, p)

        own_keys = key_operand(own_start, None)
        _flash_tiles(
            [(0, p) for p in range(N_PAIRS)],
            lambda u, p: jnp.dot(own_keys(p), query_operand(p), preferred_element_type=F32),
            lambda s, u, p: jnp.where(causal, s, -jnp.inf),
            update(lambda u, p: vt_ref[lanes(p), pl.ds(own_start, blk)]))

        def body(c, carry):
            js = [c * MOBA_UNROLL + u for u in range(MOBA_UNROLL)]
            starts = [pl.multiple_of(j * blk, blk) for j in js]
            keys = [key_operand(starts[u], ONES_ROWS + js[u]) for u in range(MOBA_UNROLL)]
            _flash_tiles(
                [(u, p) for u in range(MOBA_UNROLL) for p in range(N_PAIRS)],
                lambda u, p: jnp.dot(keys[u](p), query_operand(p), preferred_element_type=F32),
                (lambda s, u, p: s) if fast else
                (lambda s, u, p: jnp.where(sel_ref[p, pl.ds(js[u], 1), :] > 0.5, s, -jnp.inf)),
                update(lambda u, p: vt_ref[lanes(p), pl.ds(starts[u], blk)]))
            return carry

        lax.fori_loop(0, ntrip, body, 0)

    attend(True)

    @pl.when(jnp.logical_not(_denominator_ok(acc_ref)))
    def _():
        attend(False)

    _flash_finish(o_ref, ot_ref, acc_ref, tq)


def _moba(qt_all, k_all, vt_all, batch, seq, q_row_blk, k_col_blk, v_row_blk):
    tq = MOBA_BLOCK
    nq = seq // tq
    nb = seq // MOBA_BLOCK
    topk = min(MOBA_TOPK, nb - 1)
    w = BRANCH_WIDTH
    return pl.pallas_call(
        functools.partial(_moba_kernel, seq=seq, topk=topk),
        grid=(batch, nq),
        in_specs=[
            pl.BlockSpec((w, tq), lambda b, i: (q_row_blk, b * nq + i)),
            pl.BlockSpec((seq, w), lambda b, i: (b, k_col_blk), pipeline_mode=pl.Buffered(1)),
            pl.BlockSpec((w, seq), lambda b, i: (v_row_blk, b), pipeline_mode=pl.Buffered(1)),
        ],
        out_specs=pl.BlockSpec((tq, w), lambda b, i: (b * nq + i, 0)),
        out_shape=jax.ShapeDtypeStruct((batch * seq, w), F32),
        scratch_shapes=[pltpu.VMEM((nb, w), BF16),
                        pltpu.VMEM((N_PAIRS, 1, 2 * tq), F32),
                        pltpu.VMEM((N_PAIRS, 2 * LANES, 2 * tq), BF16),
                        pltpu.VMEM((N_PAIRS, nb, 2 * tq), F32),
                        pltpu.VMEM((N_PAIRS, 1, 2 * tq), F32),
                        pltpu.VMEM((N_PAIRS, 2, HEAD_DIM + ONES_ROWS, tq), F32),
                        pltpu.VMEM((w, tq), F32)],
        compiler_params=_cparams(2),
        name="moba",
    )(qt_all, k_all, vt_all)


def _ordered_key(bits):
    return jnp.where(bits < 0, jnp.int32(INT_MIN) - bits, bits)


def _tree_sum(parts):
    while len(parts) > 1:
        parts = [parts[a] + parts[a + 1] for a in range(0, len(parts) - 1, 2)] + (
            [parts[-1]] if len(parts) % 2 else [])
    return parts[0]


def _dsa_kernel(qit_ref, ki_ref, wt_ref, qt_ref, k_ref, vt_ref, o_ref,
                score_ref, hi_ref, lo_ref, kb_ref, qop_ref, m_ref, acc_ref, ot_ref,
                *, tq, kc, kcb, n_sel):
    i = pl.program_id(1)
    nch = (i * tq + tq) // kc
    nchb = (i * tq + tq + kcb - 1) // kcb
    ntrip = (nch + DSA_UNROLL - 1) // DSA_UNROLL
    krow = lax.broadcasted_iota(jnp.int32, (kc, tq), 0)
    qpos = i * tq + lax.broadcasted_iota(jnp.int32, (kc, tq), 1)
    n_sel_f = float(n_sel)

    @pl.when(i == 0)
    def _():
        _key_norm_bound(k_ref, kb_ref, k_ref.shape[0], tq)

    for p in range(N_PAIRS):
        qop_ref[p, :LANES] = _pair_operand(qit_ref, p)
    w_rows = [wt_ref[h:h + 1, :] * (N_HEADS ** -0.5) for h in range(N_HEADS)]

    def score_body(c, carry, masked, unroll):
        starts = [pl.multiple_of((c * unroll + u) * kc, kc) for u in range(unroll)]
        rs = [[jnp.dot(ki_ref[pl.ds(st, kc), :], qop_ref[p, :LANES], preferred_element_type=F32)
               for p in range(N_PAIRS)] for st in starts]
        for u, start in enumerate(starts):
            acc = jnp.zeros((kc, tq), F32)
            for p in range(N_PAIRS):
                acc = acc + jnp.maximum(rs[u][p][:, :tq], 0.0) * w_rows[2 * p]
                acc = acc + jnp.maximum(rs[u][p][:, tq:], 0.0) * w_rows[2 * p + 1]
            sc = jnp.where(start + krow <= qpos, acc, -jnp.inf) if masked else acc
            score_ref[pl.ds(start, kc), :] = sc
            key = _ordered_key(lax.bitcast_convert_type(sc, jnp.int32))
            hi_ref[pl.ds(start, kc), :] = jnp.right_shift(key, 16).astype(jnp.int16)
            lo_ref[pl.ds(start, kc), :] = (key ^ jnp.int32(0x8000)).astype(jnp.int16)
        return carry

    n_full = (i * tq + 1) // kc
    n_fast = n_full // SCORE_UNROLL
    lax.fori_loop(0, n_fast, functools.partial(score_body, masked=False, unroll=SCORE_UNROLL), 0)
    lax.fori_loop(n_fast * SCORE_UNROLL, n_full, functools.partial(score_body, masked=False, unroll=1), 0)
    lax.fori_loop(n_full, nch, functools.partial(score_body, masked=True, unroll=1), 0)

    @pl.when(nch * kc < nchb * kcb)
    def _():
        pad_rows = pl.ds(pl.multiple_of(nch * kc, kc), kcb - kc)
        score_ref[pad_rows, :] = jnp.full((kcb - kc, tq), -jnp.inf, F32)
        hi_ref[pad_rows, :] = jnp.full((kcb - kc, tq), I16_MIN, jnp.int16)
        lo_ref[pad_rows, :] = jnp.full((kcb - kc, tq), I16_MIN, jnp.int16)

    def count16(ref, pred):
        n_acc = 4
        rows = 2 * SUBLANES

        def body(c, accs):
            x = ref[pl.ds(pl.multiple_of(c * kcb, kcb), kcb), :]
            accs = list(accs)
            for r in range(kcb // rows):
                hit = jnp.where(pred(x[r * rows:(r + 1) * rows]), jnp.int16(1), jnp.int16(0))
                accs[r % n_acc] = accs[r % n_acc] + hit
            return tuple(accs)
        accs = lax.fori_loop(0, nchb, body, tuple(jnp.zeros((rows, tq), jnp.int16) for _ in range(n_acc)))
        tot = _tree_sum(list(accs)).astype(jnp.int32).astype(F32)
        return jnp.sum(tot, axis=0, keepdims=True)

    def bisect16(ref, target, start, count_at_start, bits):
        def body(it, carry):
            t, c_ge = carry
            cand = t + jnp.left_shift(jnp.int32(1), bits - 1 - it)
            cand16 = cand.astype(jnp.int16)
            cnt = count16(ref, lambda x: x >= cand16)
            ok = cnt >= target
            return jnp.where(ok, cand, t), jnp.where(ok, cnt, c_ge)
        return lax.fori_loop(0, bits, body, (start, count_at_start))

    i16_min_row = jnp.full((1, tq), I16_MIN, jnp.int32)
    rows_seen = jnp.full((1, tq), (nchb * kcb).astype(F32))
    t_hi, c_ge_hi = bisect16(hi_ref, n_sel_f, i16_min_row, rows_seen, 16)
    t_hi16 = t_hi.astype(jnp.int16)
    c_gt_hi = count16(hi_ref, lambda x: x > t_hi16)

    def low_body(c, carry):
        rows = pl.ds(pl.multiple_of(c * kcb, kcb), kcb)
        lo_ref[rows, :] = jnp.where(hi_ref[rows, :] == t_hi16, lo_ref[rows, :], jnp.int16(I16_MIN))
        return carry

    lax.fori_loop(0, nchb, low_body, 0)
    t_lo, c_ge_lo = bisect16(lo_ref, n_sel_f - c_gt_hi, i16_min_row, c_ge_hi - c_gt_hi, 16)
    n_ge = c_gt_hi + c_ge_lo
    key_t = t_hi * 65536 + (t_lo + 32768)
    short = key_t <= KEY_NEG_INF
    thr = jnp.where(short, -FLT_MAX, lax.bitcast_convert_type(_ordered_key(key_t), F32))
    has_tie = jnp.max(jnp.where(short, 0.0, n_ge - n_sel_f)) > 0.0

    @pl.when(has_tie)
    def _():
        lower = (lax.broadcasted_iota(jnp.int32, (kc, kc), 1)
                 < lax.broadcasted_iota(jnp.int32, (kc, kc), 0))
        lower = jnp.where(lower, 1.0, 0.0).astype(BF16)

        def gt_body(c, cnt):
            sc = score_ref[pl.ds(pl.multiple_of(c * kc, kc), kc), :]
            return cnt + jnp.sum(jnp.where(sc > thr, 1.0, 0.0), axis=0, keepdims=True)
        n_gt = lax.fori_loop(0, nch, gt_body, jnp.zeros((1, tq), F32))
        need = n_sel_f - n_gt

        def body(c, seen):
            start = pl.multiple_of(c * kc, kc)
            sc = score_ref[pl.ds(start, kc), :]
            eq = jnp.where(sc == thr, 1.0, 0.0)
            rank = seen + jnp.dot(lower, eq.astype(BF16), preferred_element_type=F32)
            keep = (sc > thr) | ((sc == thr) & (rank < need))
            keep = keep & (start + krow <= qpos)
            score_ref[pl.ds(start, kc), :] = jnp.where(keep, jnp.inf, -jnp.inf)
            return seen + jnp.sum(eq, axis=0, keepdims=True)
        lax.fori_loop(0, nch, body, jnp.zeros((1, tq), F32))

    for p in range(N_PAIRS):
        qop = _pair_operand(qt_ref, p)
        qop_ref[p, :LANES] = qop
        qop_ref[p, LANES:LANES + ONES_ROWS] = _offset_rows(qop, kb_ref[p])
        qop_ref[p, LANES + ONES_ROWS:] = jnp.zeros((LANES - ONES_ROWS, 2 * tq), BF16)

    def lanes(p):
        return slice(p * LANES, (p + 1) * LANES)

    def attend(fast):
        acc_ref[...] = jnp.zeros(acc_ref.shape, F32)
        if not fast:
            m_ref[...] = jnp.full(m_ref.shape, NEG_BIG, F32)
        ones = jnp.ones((kc, LANES), BF16)

        def att_body(c, carry):
            starts = [pl.multiple_of((c * DSA_UNROLL + u) * kc, kc) for u in range(DSA_UNROLL)]

            def qk(u, p):
                k_tile = k_ref[pl.ds(starts[u], kc), lanes(p)]
                if fast:
                    return jnp.dot(jnp.concatenate([k_tile, ones], axis=1), qop_ref[p],
                                   preferred_element_type=F32)
                return jnp.dot(k_tile, qop_ref[p, :LANES], preferred_element_type=F32)

            def add_bias(s, u, p):
                bias = jnp.where(score_ref[pl.ds(starts[u], kc), :] >= thr, 0.0, NEG_BIG)
                return jnp.concatenate([s[:, :tq] + bias, s[:, tq:] + bias], axis=1)

            def update(s, u, p):
                vt_tile = vt_ref[lanes(p), pl.ds(starts[u], kc)]
                if fast:
                    _offset_update(s, vt_tile, acc_ref, p)
                else:
                    _online_update(s, vt_tile, m_ref, acc_ref, p)

            _flash_tiles([(u, p) for u in range(DSA_UNROLL) for p in range(N_PAIRS)], qk, add_bias, update)
            return carry

        lax.fori_loop(0, ntrip, att_body, 0)

    attend(True)

    @pl.when(jnp.logical_not(_denominator_ok(acc_ref)))
    def _():
        attend(False)

    _flash_finish(o_ref, ot_ref, acc_ref, tq)


def _dsa(qt_all, k_all, vt_all, wt, batch, seq, tq, kc, kcb):
    nq = seq // tq
    w = BRANCH_WIDTH
    n_sel = min(IDX_TOPK_MAX, seq // 4)
    assert tq % kc == 0 and kcb == DSA_UNROLL * kc and seq % kcb == 0
    ki_col_blk = (2 * w) // LANES
    return pl.pallas_call(
        functools.partial(_dsa_kernel, tq=tq, kc=kc, kcb=kcb, n_sel=n_sel),
        grid=(batch, nq),
        in_specs=[
            pl.BlockSpec((w, tq), lambda b, i: (2, b * nq + i)),
            pl.BlockSpec((seq, LANES), lambda b, i: (b, ki_col_blk), pipeline_mode=pl.Buffered(1)),
            pl.BlockSpec((2 * SUBLANES, tq), lambda b, i: (0, b * nq + i)),
            pl.BlockSpec((w, tq), lambda b, i: (1, b * nq + i)),
            pl.BlockSpec((seq, w), lambda b, i: (b, 1), pipeline_mode=pl.Buffered(1)),
            pl.BlockSpec((w, seq), lambda b, i: (1, b), pipeline_mode=pl.Buffered(1)),
        ],
        out_specs=pl.BlockSpec((tq, w), lambda b, i: (b * nq + i, 0)),
        out_shape=jax.ShapeDtypeStruct((batch * seq, w), F32),
        scratch_shapes=[pltpu.VMEM((seq, tq), F32),
                        pltpu.VMEM((seq, tq), jnp.int16),
                        pltpu.VMEM((seq, tq), jnp.int16),
                        pltpu.VMEM((N_PAIRS, 1, 2 * tq), F32),
                        pltpu.VMEM((N_PAIRS, 2 * LANES, 2 * tq), BF16),
                        pltpu.VMEM((N_PAIRS, 1, 2 * tq), F32),
                        pltpu.VMEM((N_PAIRS, 2, HEAD_DIM + ONES_ROWS, tq), F32),
                        pltpu.VMEM((w, tq), F32)],
        compiler_params=_cparams(2),
        name="dsa",
    )(qt_all, k_all, wt, qt_all, k_all, vt_all)


def _merge_kernel(ya_ref, yb_ref, h_ref, kv_ref, wmq_ref, wg_ref, wmix_ref, wb_ref, wo_ref, x_ref, gn_ref,
                  *out_refs, last):
    w = BRANCH_WIDTH
    d = x_ref.shape[1]
    h = h_ref[...]
    mq = jnp.dot(h, wmq_ref[...], preferred_element_type=F32).astype(BF16)
    ym_heads = []
    for hd in range(M_HEADS):
        cols = slice(hd * M_HEAD_DIM, (hd + 1) * M_HEAD_DIM)
        km = kv_ref[:, cols]
        vm = kv_ref[:, w + hd * M_HEAD_DIM:w + (hd + 1) * M_HEAD_DIM]
        sc = lax.dot_general(mq[:, cols], km, (((1,), (1,)), ((), ())),
                             preferred_element_type=F32) * (M_HEAD_DIM ** -0.5)
        sc = sc - jnp.max(sc, axis=-1, keepdims=True)
        e = jnp.exp(sc)
        p = e / jnp.sum(e, axis=-1, keepdims=True)
        ym_heads.append(jnp.dot(p.astype(BF16), vm, preferred_element_type=F32))
    ym = jnp.concatenate(ym_heads, axis=-1)
    ys = (ya_ref[...], yb_ref[...], ym)
    merged = None
    for n in range(3):
        g = jnp.dot(h, wg_ref[:, n * w:(n + 1) * w], preferred_element_type=F32)
        y = (ys[n] * (g * jax.nn.sigmoid(g))).astype(BF16)
        up = jnp.dot(y, wb_ref[n], preferred_element_type=F32)
        mix = jax.nn.sigmoid(jnp.dot(h, wmix_ref[:, n * d:(n + 1) * d], preferred_element_type=F32))
        term = mix * up
        merged = term if merged is None else merged + term
    x_new = x_ref[...] + jnp.dot(merged.astype(BF16), wo_ref[...], preferred_element_type=F32)
    y = x_new * lax.rsqrt(jnp.mean(x_new * x_new, axis=-1, keepdims=True) + RMS_EPS) * gn_ref[...]
    if last:
        out_refs[0][...] = y
    else:
        out_refs[0][...] = x_new
        out_refs[1][...] = y.astype(BF16)


def _merge(ya, yb, h, kv, w_mq, w_g, w_mix, wb, wo, x2d, g_next, last, seq, n_mem, tm):
    m, d = x2d.shape
    w = BRANCH_WIDTH
    nt = seq // tm

    def resident(shape):
        return pl.BlockSpec(shape, lambda i: (0,) * len(shape), pipeline_mode=pl.Buffered(1))

    row_tile = pl.BlockSpec((tm, d), lambda i: (i, 0))
    out_f32 = jax.ShapeDtypeStruct((m, d), F32)
    return pl.pallas_call(
        functools.partial(_merge_kernel, last=last),
        grid=(m // tm,),
        in_specs=[
            pl.BlockSpec((tm, w), lambda i: (i, 0)),
            pl.BlockSpec((tm, w), lambda i: (i, 0)),
            pl.BlockSpec((tm, d), lambda i: (i, 0)),
            pl.BlockSpec((n_mem, 2 * w), lambda i: (i // nt, 0)),
            resident((d, w)),
            resident((d, 3 * w)),
            resident((d, 3 * d)),
            resident((3, w, d)),
            resident((d, d)),
            row_tile,
            resident((1, d)),
        ],
        out_specs=row_tile if last else (row_tile, row_tile),
        out_shape=out_f32 if last else (out_f32, jax.ShapeDtypeStruct((m, d), BF16)),
        compiler_params=_cparams(1),
        name="merge",
    )(ya, yb, h, kv, w_mq, w_g, w_mix, wb, wo, x2d, g_next.reshape(1, d))


def _rope_tables(seq):
    inv_freq = ROPE_THETA ** (-jnp.arange(ROT_HALF, dtype=F32) / ROT_HALF)
    ang = jnp.arange(seq, dtype=jnp.int32).astype(F32)[:, None] * inv_freq[None, :]
    cos, sin = jnp.cos(ang), jnp.sin(ang)
    r = np.arange(LANES) % HEAD_DIM
    f = r % ROT_HALF
    first = jnp.asarray(r < ROT_HALF)
    second = jnp.asarray((r >= ROT_HALF) & (r < 2 * ROT_HALF))
    rot = jnp.asarray(r < 2 * ROT_HALF)
    c_tok = jnp.where(rot[None, :], cos[:, f], 1.0)
    s1_tok = jnp.where(first[None, :], -sin[:, f], 0.0)
    s2_tok = jnp.where(second[None, :], sin[:, f], 0.0)
    return (c_tok, s1_tok, s2_tok), (cos.T, sin.T)


def kernel(x, mem, norm_g, w_in, mem_norm_g, w_mem_kv, w_branch, w_out, final_g):
    batch, seq, d = x.shape
    n_mem = mem.shape[1]
    depth = norm_g.shape[0]
    w = BRANCH_WIDTH
    m = batch * seq
    assert seq % MOBA_BLOCK == 0 and d % LANES == 0

    tm = min(1024, seq)
    tm_merge = min(512, seq)
    dsa_tq = 256
    dsa_kc = 256
    dsa_kcb = min(512, seq)

    tok_tabs, feat_tabs = _rope_tables(seq)
    offs = np.cumsum([0, w, w, w, w, w, w, w, w, N_HEADS * HEAD_DIM, HEAD_DIM, N_HEADS, w, w, 3 * d])
    (o_aq, o_ak, o_av, o_ag, o_bq, o_bk, o_bv, o_bg, o_iq, o_ik, o_iw, o_mq, o_mg, o_mix, o_end) = offs
    attn_scale = HEAD_DIM ** -0.5 * float(np.log2(np.e))
    idx_scale = HEAD_DIM ** -0.5

    x2d = x.reshape(m, d)
    mem2d = mem.reshape(batch * n_mem, d)
    h = _rmsnorm(x2d, norm_g[0], BF16, tm)
    for l in range(depth):
        wl = w_in[l]
        col = lambda a, b: wl[:, a:b]
        w_k = jnp.concatenate([col(o_ak, o_av), col(o_bk, o_bv), col(o_ik, o_iw), col(o_ik, o_iw)],
                              axis=1).astype(BF16)
        w_qt = jnp.concatenate([col(o_aq, o_ak) * attn_scale, col(o_bq, o_bk) * attn_scale,
                                col(o_iq, o_ik) * idx_scale], axis=1).T.astype(BF16)
        w_vt = jnp.concatenate([col(o_av, o_ag), col(o_bv, o_bg)], axis=1).T.astype(BF16)
        w_iwt = jnp.pad(col(o_iw, o_mq).T, ((0, 2 * SUBLANES - N_HEADS), (0, 0))).astype(BF16)
        w_mq = col(o_mq, o_mg).astype(BF16)
        w_g = jnp.concatenate([col(o_ag, o_bq), col(o_bg, o_iq), col(o_mg, o_mix)], axis=1).astype(BF16)
        w_mix = col(o_mix, o_end).astype(BF16)

        k_all = _mm_tok(h, w_k, "rope", BF16, tm, w_k.shape[1], tok_tabs, seq)
        qt_all = _mm_feat(w_qt, h, "rope", BF16, tm, w, feat_tabs, seq)
        vt_all = _mm_feat(w_vt, h, "none", BF16, tm, w)
        iwt = _mm_feat(w_iwt, h, "none", F32, tm, 2 * SUBLANES)

        ya = _moba(qt_all, k_all, vt_all, batch, seq, 0, 0, 0)
        yb = _dsa(qt_all, k_all, vt_all, iwt, batch, seq, dsa_tq, dsa_kc, dsa_kcb)

        mn = _rmsnorm(mem2d, mem_norm_g[l], BF16, n_mem)
        kv = _mm_tok(mn, w_mem_kv[l].astype(BF16), "none", BF16, n_mem, 2 * w)

        last = l == depth - 1
        res = _merge(ya, yb, h, kv, w_mq, w_g, w_mix, w_branch[l].astype(BF16), w_out[l].astype(BF16),
                     x2d, final_g if last else norm_g[l + 1], last, seq, n_mem, tm_merge)
        if not last:
            x2d, h = res

    return res.reshape(batch, seq, d)
```

```python
import functools

import jax
import jax.numpy as jnp
import numpy as np
from jax import lax
from jax.experimental import pallas as pl
from jax.experimental.pallas import tpu as pltpu

F32 = jnp.float32
BF16 = jnp.bfloat16

HEAD_DIM = 64
N_HEADS = 8
BRANCH_WIDTH = 512
M_HEADS = 4
M_HEAD_DIM = 128
MOBA_BLOCK = 256
MOBA_TOPK = 3
IDX_TOPK_MAX = 256
ROPE_THETA = 500000.0
ROT_HALF = HEAD_DIM // 4 // 2
RMS_EPS = 1e-6

LANES = 128
SUBLANES = 8
VMEM_LIMIT = 56 * 1024 * 1024
NEG_BIG = -1e30
FLT_MAX = float(np.finfo(np.float32).max)
INT_MIN = -(2 ** 31)
I16_MIN = -(2 ** 15)
KEY_NEG_INF = -0x7F800000


def _cparams(n_axes):
    return pltpu.CompilerParams(dimension_semantics=("arbitrary",) * n_axes,
                                vmem_limit_bytes=VMEM_LIMIT)


def _rmsnorm_kernel(x_ref, g_ref, o_ref):
    xf = x_ref[...]
    y = xf * lax.rsqrt(jnp.mean(xf * xf, axis=-1, keepdims=True) + RMS_EPS)
    o_ref[...] = (y * g_ref[...]).astype(o_ref.dtype)


def _rmsnorm(x2d, g, out_dtype, tm):
    m, d = x2d.shape
    return pl.pallas_call(
        _rmsnorm_kernel,
        grid=(m // tm,),
        in_specs=[pl.BlockSpec((tm, d), lambda i: (i, 0)),
                  pl.BlockSpec((1, d), lambda i: (0, 0))],
        out_specs=pl.BlockSpec((tm, d), lambda i: (i, 0)),
        out_shape=jax.ShapeDtypeStruct((m, d), out_dtype),
        compiler_params=_cparams(1),
        name="rmsnorm",
    )(x2d, g.reshape(1, d))


def _mm_tok_kernel(h_ref, w_ref, o_ref):
    o_ref[...] = jnp.dot(h_ref[...], w_ref[...], preferred_element_type=F32).astype(o_ref.dtype)


def _mm_tok(h, w, out_dtype, tm, tn):
    m, k = h.shape
    n = w.shape[1]
    return pl.pallas_call(
        _mm_tok_kernel,
        grid=(m // tm, n // tn),
        in_specs=[pl.BlockSpec((tm, k), lambda i, j: (i, 0)),
                  pl.BlockSpec((k, tn), lambda i, j: (0, j))],
        out_specs=pl.BlockSpec((tm, tn), lambda i, j: (i, j)),
        out_shape=jax.ShapeDtypeStruct((m, n), out_dtype),
        compiler_params=_cparams(2),
        name="proj_tok",
    )(h, w)


def _in_proj_kernel(h_ref, wk_ref, wqt_ref, wvt_ref, wiw_ref, c_ref, s1_ref, s2_ref, cos_ref, sin_ref,
                    k_ref, qt_ref, vt_ref, iw_ref):
    h = h_ref[...]
    nt_dims = (((1,), (1,)), ((), ()))

    k_acc = jnp.dot(h, wk_ref[...], preferred_element_type=F32)
    c, s1, s2 = c_ref[...], s1_ref[...], s2_ref[...]
    for j in range(k_acc.shape[1] // LANES):
        piece = k_acc[:, j * LANES:(j + 1) * LANES]
        up = pltpu.roll(piece, LANES - ROT_HALF, 1)
        down = pltpu.roll(piece, ROT_HALF, 1)
        k_ref[:, j * LANES:(j + 1) * LANES] = (piece * c + up * s1 + down * s2).astype(k_ref.dtype)

    cos, sin = cos_ref[...], sin_ref[...]
    w = BRANCH_WIDTH
    for blk in range(wqt_ref.shape[0] // w):
        acc = lax.dot_general(wqt_ref[blk * w:(blk + 1) * w, :], h, nt_dims, preferred_element_type=F32)
        pieces = []
        for hh in range(w // HEAD_DIM):
            base = hh * HEAD_DIM
            x1 = acc[base:base + ROT_HALF]
            x2 = acc[base + ROT_HALF:base + 2 * ROT_HALF]
            pieces += [x1 * cos - x2 * sin, x2 * cos + x1 * sin, acc[base + 2 * ROT_HALF:base + HEAD_DIM]]
        qt_ref[blk * w:(blk + 1) * w, :] = jnp.concatenate(pieces, axis=0).astype(qt_ref.dtype)

    vt_ref[...] = lax.dot_general(wvt_ref[...], h, nt_dims, preferred_element_type=F32).astype(vt_ref.dtype)
    iw_ref[...] = lax.dot_general(wiw_ref[...], h, nt_dims, preferred_element_type=F32)


def _in_proj(h, w_k, w_qt, w_vt, w_iwt, tok_tabs, feat_tabs, seq, tm):
    m, k = h.shape
    nt = seq // tm

    def resident(shape):
        return pl.BlockSpec(shape, lambda i: (0,) * len(shape), pipeline_mode=pl.Buffered(1))

    tok_tab = pl.BlockSpec((tm, LANES), lambda i: (i % nt, 0))
    feat_tab = pl.BlockSpec((SUBLANES, tm), lambda i: (0, i % nt))
    n_k, n_q, n_v, n_w = w_k.shape[1], w_qt.shape[0], w_vt.shape[0], w_iwt.shape[0]
    return pl.pallas_call(
        _in_proj_kernel,
        grid=(m // tm,),
        in_specs=[pl.BlockSpec((tm, k), lambda i: (i, 0)),
                  resident(w_k.shape), resident(w_qt.shape), resident(w_vt.shape), resident(w_iwt.shape),
                  tok_tab, tok_tab, tok_tab, feat_tab, feat_tab],
        out_specs=(pl.BlockSpec((tm, n_k), lambda i: (i, 0)),
                   pl.BlockSpec((n_q, tm), lambda i: (0, i)),
                   pl.BlockSpec((n_v, tm), lambda i: (0, i)),
                   pl.BlockSpec((n_w, tm), lambda i: (0, i))),
        out_shape=(jax.ShapeDtypeStruct((m, n_k), BF16),
                   jax.ShapeDtypeStruct((n_q, m), BF16),
                   jax.ShapeDtypeStruct((n_v, m), BF16),
                   jax.ShapeDtypeStruct((n_w, m), F32)),
        compiler_params=_cparams(1),
        name="in_proj",
    )(h, w_k, w_qt, w_vt, w_iwt, *tok_tabs, *feat_tabs)


N_PAIRS = N_HEADS // 2


def _pair_operand(qt_ref, pair):
    qp = qt_ref[pair * LANES:(pair + 1) * LANES, :].astype(F32)
    row = lax.broadcasted_iota(jnp.int32, qp.shape, 0)
    lo = jnp.where(row < HEAD_DIM, qp, 0.0)
    hi = jnp.where(row >= HEAD_DIM, qp, 0.0)
    return jnp.concatenate([lo, hi], axis=1).astype(BF16)


ONES_ROWS = 16
QK_LOOKAHEAD = 4


OFFSET_SLACK = 1.0 + 2.0 ** -5
L_FLOOR = 2.0 ** -80


def _pv_accumulate(pm, vt_tile, acc_ref, p, scale=None):
    keys = vt_tile.shape[1]
    tq = pm.shape[1] // 2
    ones = jnp.ones((ONES_ROWS, keys), BF16)
    for e in range(2):
        vt_ext = jnp.concatenate([vt_tile[e * HEAD_DIM:(e + 1) * HEAD_DIM], ones], axis=0)
        cols = slice(e * tq, (e + 1) * tq)
        upd = jnp.dot(vt_ext, pm[:, cols], preferred_element_type=F32)
        old = acc_ref[p, e] if scale is None else scale[:, cols] * acc_ref[p, e]
        acc_ref[p, e] = old + upd


def _online_update(s, vt_tile, m_ref, acc_ref, p):
    m = m_ref[p]
    m_new = jnp.maximum(m, jnp.max(s, axis=0, keepdims=True))
    _pv_accumulate(jnp.exp2(s - m_new).astype(BF16), vt_tile, acc_ref, p, scale=jnp.exp2(m - m_new))
    m_ref[p] = m_new


def _offset_update(s, vt_tile, acc_ref, p):
    _pv_accumulate(jnp.exp2(s).astype(BF16), vt_tile, acc_ref, p)


def _flash_tiles(tiles, qk, mask, update):
    ss = [qk(*t) for t in tiles[:QK_LOOKAHEAD]]
    for n, (u, p) in enumerate(tiles):
        if n + QK_LOOKAHEAD < len(tiles):
            ss.append(qk(*tiles[n + QK_LOOKAHEAD]))
        update(mask(ss[n], u, p), u, p)
        ss[n] = None


def _key_norm_bound(k_ref, kb_ref, seq, tq):
    chunk = min(512, seq)
    width = k_ref.shape[1]
    lane = lax.broadcasted_iota(jnp.int32, (width, LANES), 0)
    head = lax.broadcasted_iota(jnp.int32, (width, LANES), 1)
    group = jnp.where((lane >= head * HEAD_DIM) & (lane < (head + 1) * HEAD_DIM), 1.0, 0.0).astype(BF16)

    def body(c, mx):
        kk = k_ref[pl.ds(pl.multiple_of(c * chunk, chunk), chunk), :].astype(F32)
        n2 = jnp.dot((kk * kk).astype(BF16), group, preferred_element_type=F32)
        return jnp.maximum(mx, jnp.max(n2, axis=0, keepdims=True))
    mx = lax.fori_loop(0, seq // chunk, body, jnp.zeros((1, LANES), F32))
    for p in range(N_PAIRS):
        kb_ref[p] = jnp.concatenate([jnp.broadcast_to(mx[:, 2 * p:2 * p + 1], (1, tq)),
                                     jnp.broadcast_to(mx[:, 2 * p + 1:2 * p + 2], (1, tq))], axis=1)


def _offset_rows(qop, kb2):
    qf = qop.astype(F32)
    bound = jnp.sqrt(jnp.sum(qf * qf, axis=0, keepdims=True) * kb2) * OFFSET_SLACK
    row = lax.broadcasted_iota(jnp.int32, (ONES_ROWS, qop.shape[1]), 0)
    return jnp.where(row == 0, -bound, 0.0).astype(BF16)


def _denominator_ok(acc_ref):
    return jnp.min(acc_ref[:, :, HEAD_DIM:HEAD_DIM + 1, :]) > L_FLOOR


def _flash_finish(o_ref, ot_ref, acc_ref, tq):
    for p in range(N_PAIRS):
        for e in range(2):
            lo = p * LANES + e * HEAD_DIM
            ot_ref[lo:lo + HEAD_DIM, :] = acc_ref[p, e, :HEAD_DIM] / acc_ref[p, e, HEAD_DIM:HEAD_DIM + 1]
    o_ref[...] = ot_ref[...].T


MOBA_UNROLL = 2
DSA_UNROLL = 2
SCORE_UNROLL = 4


def _moba_kernel(qt_ref, k_ref, vt_ref, o_ref, kmean_ref, kb_ref, qop_ref, sel_ref, m_ref, acc_ref,
                 ot_ref, *, seq, topk):
    i = pl.program_id(1)
    blk = MOBA_BLOCK
    nb = seq // blk
    tq = blk

    @pl.when(i == 0)
    def _():
        r = lax.broadcasted_iota(jnp.int32, (nb, seq), 0)
        c = lax.broadcasted_iota(jnp.int32, (nb, seq), 1)
        member = jnp.where((c >= r * blk) & (c < (r + 1) * blk), 1.0, 0.0).astype(BF16)
        ksum = jnp.dot(member, k_ref[...], preferred_element_type=F32)
        kmean_ref[...] = (ksum * (1.0 / blk)).astype(BF16)
        _key_norm_bound(k_ref, kb_ref, seq, tq)

    blk_id = lax.broadcasted_iota(jnp.int32, (nb, 2 * tq), 0)
    past = blk_id < i
    own_start = pl.multiple_of(i * blk, blk)

    for p in range(N_PAIRS):
        lanes = slice(p * LANES, (p + 1) * LANES)
        qop = _pair_operand(qt_ref, p)
        qop_ref[p, :LANES] = qop
        qop_ref[p, LANES:LANES + ONES_ROWS] = _offset_rows(qop, kb_ref[p])
        gate = jnp.dot(kmean_ref[:, lanes], qop, preferred_element_type=F32)
        gate = jnp.where(past, gate, -jnp.inf)
        sel = jnp.zeros((nb, 2 * tq), jnp.bool_)
        for _ in range(topk):
            mx = jnp.max(gate, axis=0, keepdims=True)
            first = jnp.min(jnp.where(gate == mx, blk_id, nb), axis=0, keepdims=True)
            hit = blk_id == first
            sel = sel | hit
            gate = jnp.where(hit, -jnp.inf, gate)
        sel_ref[p] = jnp.where(sel & past, 1.0, 0.0)
        qop_ref[p, LANES + ONES_ROWS:LANES + ONES_ROWS + nb] = jnp.where(sel & past, 0.0, NEG_BIG).astype(BF16)
        qop_ref[p, LANES + ONES_ROWS + nb:] = jnp.zeros((LANES - ONES_ROWS - nb, 2 * tq), BF16)

    def lanes(p):
        return slice(p * LANES, (p + 1) * LANES)

    krow = lax.broadcasted_iota(jnp.int32, (blk, 2 * tq), 0)
    qcol = lax.broadcasted_iota(jnp.int32, (blk, 2 * tq), 1)
    causal = krow <= jnp.where(qcol < tq, qcol, qcol - tq)
    lane = lax.broadcasted_iota(jnp.int32, (blk, LANES), 1)
    ntrip = (i + MOBA_UNROLL - 1) // MOBA_UNROLL

    def attend(fast):
        acc_ref[...] = jnp.zeros(acc_ref.shape, F32)
        if not fast:
            m_ref[...] = jnp.full(m_ref.shape, NEG_BIG, F32)

        def key_operand(start, extra_lane):
            k_tile = k_ref[pl.ds(start, blk), :]
            if not fast:
                return lambda p: k_tile[:, lanes(p)]
            sel_lanes = (lane == 0) if extra_lane is None else ((lane == 0) | (lane == extra_lane))
            extra = jnp.where(sel_lanes, 1.0, 0.0).astype(BF16)
            return lambda p: jnp.concatenate([k_tile[:, lanes(p)], extra], axis=1)

        def query_operand(p):
            return qop_ref[p] if fast else qop_ref[p, :LANES]

        def update(vt_tile):
            if fast:
                return lambda s, u, p: _offset_update(s, vt_tile(u, p), acc_ref, p)
            return lambda s, u, p: _online_update(s, vt_tile(u, p), m_ref, acc_ref, p)

        own_keys = key_operand(own_start, None)
        _flash_tiles(
            [(0, p) for p in range(N_PAIRS)],
            lambda u, p: jnp.dot(own_keys(p), query_operand(p), preferred_element_type=F32),
            lambda s, u, p: jnp.where(causal, s, -jnp.inf),
            update(lambda u, p: vt_ref[lanes(p), pl.ds(own_start, blk)]))

        def body(c, carry):
            js = [c * MOBA_UNROLL + u for u in range(MOBA_UNROLL)]
            starts = [pl.multiple_of(j * blk, blk) for j in js]
            keys = [key_operand(starts[u], ONES_ROWS + js[u]) for u in range(MOBA_UNROLL)]
            _flash_tiles(
                [(u, p) for u in range(MOBA_UNROLL) for p in range(N_PAIRS)],
                lambda u, p: jnp.dot(keys[u](p), query_operand(p), preferred_element_type=F32),
                (lambda s, u, p: s) if fast else
                (lambda s, u, p: jnp.where(sel_ref[p, pl.ds(js[u], 1), :] > 0.5, s, -jnp.inf)),
                update(lambda u, p: vt_ref[lanes(p), pl.ds(starts[u], blk)]))
            return carry

        lax.fori_loop(0, ntrip, body, 0)

    attend(True)

    @pl.when(jnp.logical_not(_denominator_ok(acc_ref)))
    def _():
        attend(False)

    _flash_finish(o_ref, ot_ref, acc_ref, tq)


def _moba(qt_all, k_all, vt_all, batch, seq, q_row_blk, k_col_blk, v_row_blk):
    tq = MOBA_BLOCK
    nq = seq // tq
    nb = seq // MOBA_BLOCK
    topk = min(MOBA_TOPK, nb - 1)
    w = BRANCH_WIDTH
    return pl.pallas_call(
        functools.partial(_moba_kernel, seq=seq, topk=topk),
        grid=(batch, nq),
        in_specs=[
            pl.BlockSpec((w, tq), lambda b, i: (q_row_blk, b * nq + i)),
            pl.BlockSpec((seq, w), lambda b, i: (b, k_col_blk), pipeline_mode=pl.Buffered(1)),
            pl.BlockSpec((w, seq), lambda b, i: (v_row_blk, b), pipeline_mode=pl.Buffered(1)),
        ],
        out_specs=pl.BlockSpec((tq, w), lambda b, i: (b * nq + i, 0)),
        out_shape=jax.ShapeDtypeStruct((batch * seq, w), F32),
        scratch_shapes=[pltpu.VMEM((nb, w), BF16),
                        pltpu.VMEM((N_PAIRS, 1, 2 * tq), F32),
                        pltpu.VMEM((N_PAIRS, 2 * LANES, 2 * tq), BF16),
                        pltpu.VMEM((N_PAIRS, nb, 2 * tq), F32),
                        pltpu.VMEM((N_PAIRS, 1, 2 * tq), F32),
                        pltpu.VMEM((N_PAIRS, 2, HEAD_DIM + ONES_ROWS, tq), F32),
                        pltpu.VMEM((w, tq), F32)],
        compiler_params=_cparams(2),
        name="moba",
    )(qt_all, k_all, vt_all)


def _ordered_key(bits):
    return jnp.where(bits < 0, jnp.int32(INT_MIN) - bits, bits)


def _tree_sum(parts):
    while len(parts) > 1:
        parts = [parts[a] + parts[a + 1] for a in range(0, len(parts) - 1, 2)] + (
            [parts[-1]] if len(parts) % 2 else [])
    return parts[0]


def _dsa_kernel(qit_ref, ki_ref, wt_ref, qt_ref, k_ref, vt_ref, o_ref,
                score_ref, hi_ref, lo_ref, kb_ref, qop_ref, m_ref, acc_ref, ot_ref,
                *, tq, kc, kcb, n_sel):
    i = pl.program_id(1)
    nch = (i * tq + tq) // kc
    nchb = (i * tq + tq + kcb - 1) // kcb
    ntrip = (nch + DSA_UNROLL - 1) // DSA_UNROLL
    krow = lax.broadcasted_iota(jnp.int32, (kc, tq), 0)
    qpos = i * tq + lax.broadcasted_iota(jnp.int32, (kc, tq), 1)
    n_sel_f = float(n_sel)

    @pl.when(i == 0)
    def _():
        _key_norm_bound(k_ref, kb_ref, k_ref.shape[0], tq)

    for p in range(N_PAIRS):
        qop_ref[p, :LANES] = _pair_operand(qit_ref, p)
    w_rows = [wt_ref[h:h + 1, :] * (N_HEADS ** -0.5) for h in range(N_HEADS)]

    def score_body(c, carry, masked, unroll):
        starts = [pl.multiple_of((c * unroll + u) * kc, kc) for u in range(unroll)]
        rs = [[jnp.dot(ki_ref[pl.ds(st, kc), :], qop_ref[p, :LANES], preferred_element_type=F32)
               for p in range(N_PAIRS)] for st in starts]
        for u, start in enumerate(starts):
            acc = jnp.zeros((kc, tq), F32)
            for p in range(N_PAIRS):
                acc = acc + jnp.maximum(rs[u][p][:, :tq], 0.0) * w_rows[2 * p]
                acc = acc + jnp.maximum(rs[u][p][:, tq:], 0.0) * w_rows[2 * p + 1]
            sc = jnp.where(start + krow <= qpos, acc, -jnp.inf) if masked else acc
            score_ref[pl.ds(start, kc), :] = sc
            key = _ordered_key(lax.bitcast_convert_type(sc, jnp.int32))
            hi_ref[pl.ds(start, kc), :] = jnp.right_shift(key, 16).astype(jnp.int16)
            lo_ref[pl.ds(start, kc), :] = (key ^ jnp.int32(0x8000)).astype(jnp.int16)
        return carry

    n_full = (i * tq + 1) // kc
    n_fast = n_full // SCORE_UNROLL
    lax.fori_loop(0, n_fast, functools.partial(score_body, masked=False, unroll=SCORE_UNROLL), 0)
    lax.fori_loop(n_fast * SCORE_UNROLL, n_full, functools.partial(score_body, masked=False, unroll=1), 0)
    lax.fori_loop(n_full, nch, functools.partial(score_body, masked=True, unroll=1), 0)

    @pl.when(nch * kc < nchb * kcb)
    def _():
        pad_rows = pl.ds(pl.multiple_of(nch * kc, kc), kcb - kc)
        score_ref[pad_rows, :] = jnp.full((kcb - kc, tq), -jnp.inf, F32)
        hi_ref[pad_rows, :] = jnp.full((kcb - kc, tq), I16_MIN, jnp.int16)
        lo_ref[pad_rows, :] = jnp.full((kcb - kc, tq), I16_MIN, jnp.int16)

    def count16(ref, pred):
        n_acc = 4
        rows = 2 * SUBLANES

        def body(c, accs):
            x = ref[pl.ds(pl.multiple_of(c * kcb, kcb), kcb), :]
            accs = list(accs)
            for r in range(kcb // rows):
                hit = jnp.where(pred(x[r * rows:(r + 1) * rows]), jnp.int16(1), jnp.int16(0))
                accs[r % n_acc] = accs[r % n_acc] + hit
            return tuple(accs)
        accs = lax.fori_loop(0, nchb, body, tuple(jnp.zeros((rows, tq), jnp.int16) for _ in range(n_acc)))
        tot = _tree_sum(list(accs)).astype(jnp.int32).astype(F32)
        return jnp.sum(tot, axis=0, keepdims=True)

    def bisect16(ref, target, start, count_at_start, bits):
        def body(it, carry):
            t, c_ge = carry
            cand = t + jnp.left_shift(jnp.int32(1), bits - 1 - it)
            cand16 = cand.astype(jnp.int16)
            cnt = count16(ref, lambda x: x >= cand16)
            ok = cnt >= target
            return jnp.where(ok, cand, t), jnp.where(ok, cnt, c_ge)
        return lax.fori_loop(0, bits, body, (start, count_at_start))

    i16_min_row = jnp.full((1, tq), I16_MIN, jnp.int32)
    rows_seen = jnp.full((1, tq), (nchb * kcb).astype(F32))
    t_hi, c_ge_hi = bisect16(hi_ref, n_sel_f, i16_min_row, rows_seen, 16)
    t_hi16 = t_hi.astype(jnp.int16)
    c_gt_hi = count16(hi_ref, lambda x: x > t_hi16)

    def low_body(c, carry):
        rows = pl.ds(pl.multiple_of(c * kcb, kcb), kcb)
        lo_ref[rows, :] = jnp.where(hi_ref[rows, :] == t_hi16, lo_ref[rows, :], jnp.int16(I16_MIN))
        return carry

    lax.fori_loop(0, nchb, low_body, 0)
    t_lo, c_ge_lo = bisect16(lo_ref, n_sel_f - c_gt_hi, i16_min_row, c_ge_hi - c_gt_hi, 16)
    n_ge = c_gt_hi + c_ge_lo
    key_t = t_hi * 65536 + (t_lo + 32768)
    short = key_t <= KEY_NEG_INF
    thr = jnp.where(short, -FLT_MAX, lax.bitcast_convert_type(_ordered_key(key_t), F32))
    has_tie = jnp.max(jnp.where(short, 0.0, n_ge - n_sel_f)) > 0.0

    @pl.when(has_tie)
    def _():
        lower = (lax.broadcasted_iota(jnp.int32, (kc, kc), 1)
                 < lax.broadcasted_iota(jnp.int32, (kc, kc), 0))
        lower = jnp.where(lower, 1.0, 0.0).astype(BF16)

        def gt_body(c, cnt):
            sc = score_ref[pl.ds(pl.multiple_of(c * kc, kc), kc), :]
            return cnt + jnp.sum(jnp.where(sc > thr, 1.0, 0.0), axis=0, keepdims=True)
        n_gt = lax.fori_loop(0, nch, gt_body, jnp.zeros((1, tq), F32))
        need = n_sel_f - n_gt

        def body(c, seen):
            start = pl.multiple_of(c * kc, kc)
            sc = score_ref[pl.ds(start, kc), :]
            eq = jnp.where(sc == thr, 1.0, 0.0)
            rank = seen + jnp.dot(lower, eq.astype(BF16), preferred_element_type=F32)
            keep = (sc > thr) | ((sc == thr) & (rank < need))
            keep = keep & (start + krow <= qpos)
            score_ref[pl.ds(start, kc), :] = jnp.where(keep, jnp.inf, -jnp.inf)
            return seen + jnp.sum(eq, axis=0, keepdims=True)
        lax.fori_loop(0, nch, body, jnp.zeros((1, tq), F32))

    for p in range(N_PAIRS):
        qop = _pair_operand(qt_ref, p)
        qop_ref[p, :LANES] = qop
        qop_ref[p, LANES:LANES + ONES_ROWS] = _offset_rows(qop, kb_ref[p])
        qop_ref[p, LANES + ONES_ROWS:] = jnp.zeros((LANES - ONES_ROWS, 2 * tq), BF16)

    def lanes(p):
        return slice(p * LANES, (p + 1) * LANES)

    def attend(fast):
        acc_ref[...] = jnp.zeros(acc_ref.shape, F32)
        if not fast:
            m_ref[...] = jnp.full(m_ref.shape, NEG_BIG, F32)
        ones = jnp.ones((kc, LANES), BF16)

        def att_body(c, carry):
            starts = [pl.multiple_of((c * DSA_UNROLL + u) * kc, kc) for u in range(DSA_UNROLL)]

            def qk(u, p):
                k_tile = k_ref[pl.ds(starts[u], kc), lanes(p)]
                if fast:
                    return jnp.dot(jnp.concatenate([k_tile, ones], axis=1), qop_ref[p],
                                   preferred_element_type=F32)
                return jnp.dot(k_tile, qop_ref[p, :LANES], preferred_element_type=F32)

            def add_bias(s, u, p):
                bias = jnp.where(score_ref[pl.ds(starts[u], kc), :] >= thr, 0.0, NEG_BIG)
                return jnp.concatenate([s[:, :tq] + bias, s[:, tq:] + bias], axis=1)

            def update(s, u, p):
                vt_tile = vt_ref[lanes(p), pl.ds(starts[u], kc)]
                if fast:
                    _offset_update(s, vt_tile, acc_ref, p)
                else:
                    _online_update(s, vt_tile, m_ref, acc_ref, p)

            _flash_tiles([(u, p) for u in range(DSA_UNROLL) for p in range(N_PAIRS)], qk, add_bias, update)
            return carry

        lax.fori_loop(0, ntrip, att_body, 0)

    attend(True)

    @pl.when(jnp.logical_not(_denominator_ok(acc_ref)))
    def _():
        attend(False)

    _flash_finish(o_ref, ot_ref, acc_ref, tq)


def _dsa(qt_all, k_all, vt_all, wt, batch, seq, tq, kc, kcb):
    nq = seq // tq
    w = BRANCH_WIDTH
    n_sel = min(IDX_TOPK_MAX, seq // 4)
    assert tq % kc == 0 and kcb == DSA_UNROLL * kc and seq % kcb == 0
    ki_col_blk = (2 * w) // LANES
    return pl.pallas_call(
        functools.partial(_dsa_kernel, tq=tq, kc=kc, kcb=kcb, n_sel=n_sel),
        grid=(batch, nq),
        in_specs=[
            pl.BlockSpec((w, tq), lambda b, i: (2, b * nq + i)),
            pl.BlockSpec((seq, LANES), lambda b, i: (b, ki_col_blk), pipeline_mode=pl.Buffered(1)),
            pl.BlockSpec((2 * SUBLANES, tq), lambda b, i: (0, b * nq + i)),
            pl.BlockSpec((w, tq), lambda b, i: (1, b * nq + i)),
            pl.BlockSpec((seq, w), lambda b, i: (b, 1), pipeline_mode=pl.Buffered(1)),
            pl.BlockSpec((w, seq), lambda b, i: (1, b), pipeline_mode=pl.Buffered(1)),
        ],
        out_specs=pl.BlockSpec((tq, w), lambda b, i: (b * nq + i, 0)),
        out_shape=jax.ShapeDtypeStruct((batch * seq, w), F32),
        scratch_shapes=[pltpu.VMEM((seq, tq), F32),
                        pltpu.VMEM((seq, tq), jnp.int16),
                        pltpu.VMEM((seq, tq), jnp.int16),
                        pltpu.VMEM((N_PAIRS, 1, 2 * tq), F32),
                        pltpu.VMEM((N_PAIRS, 2 * LANES, 2 * tq), BF16),
                        pltpu.VMEM((N_PAIRS, 1, 2 * tq), F32),
                        pltpu.VMEM((N_PAIRS, 2, HEAD_DIM + ONES_ROWS, tq), F32),
                        pltpu.VMEM((w, tq), F32)],
        compiler_params=_cparams(2),
        name="dsa",
    )(qt_all, k_all, wt, qt_all, k_all, vt_all)


def _merge_kernel(ya_ref, yb_ref, h_ref, kv_ref, wmq_ref, wg_ref, wmix_ref, wb_ref, wo_ref, x_ref, gn_ref,
                  *out_refs, last):
    w = BRANCH_WIDTH
    d = x_ref.shape[1]
    h = h_ref[...]
    mq = jnp.dot(h, wmq_ref[...], preferred_element_type=F32).astype(BF16)
    ym_heads = []
    for hd in range(M_HEADS):
        cols = slice(hd * M_HEAD_DIM, (hd + 1) * M_HEAD_DIM)
        km = kv_ref[:, cols]
        vm = kv_ref[:, w + hd * M_HEAD_DIM:w + (hd + 1) * M_HEAD_DIM]
        sc = lax.dot_general(mq[:, cols], km, (((1,), (1,)), ((), ())),
                             preferred_element_type=F32) * (M_HEAD_DIM ** -0.5)
        sc = sc - jnp.max(sc, axis=-1, keepdims=True)
        e = jnp.exp(sc)
        p = e / jnp.sum(e, axis=-1, keepdims=True)
        ym_heads.append(jnp.dot(p.astype(BF16), vm, preferred_element_type=F32))
    ym = jnp.concatenate(ym_heads, axis=-1)
    ys = (ya_ref[...], yb_ref[...], ym)
    merged = None
    for n in range(3):
        g = jnp.dot(h, wg_ref[:, n * w:(n + 1) * w], preferred_element_type=F32)
        y = (ys[n] * (g * jax.nn.sigmoid(g))).astype(BF16)
        up = jnp.dot(y, wb_ref[n], preferred_element_type=F32)
        mix = jax.nn.sigmoid(jnp.dot(h, wmix_ref[:, n * d:(n + 1) * d], preferred_element_type=F32))
        term = mix * up
        merged = term if merged is None else merged + term
    x_new = x_ref[...] + jnp.dot(merged.astype(BF16), wo_ref[...], preferred_element_type=F32)
    y = x_new * lax.rsqrt(jnp.mean(x_new * x_new, axis=-1, keepdims=True) + RMS_EPS) * gn_ref[...]
    if last:
        out_refs[0][...] = y
    else:
        out_refs[0][...] = x_new
        out_refs[1][...] = y.astype(BF16)


def _merge(ya, yb, h, kv, w_mq, w_g, w_mix, wb, wo, x2d, g_next, last, seq, n_mem, tm):
    m, d = x2d.shape
    w = BRANCH_WIDTH
    nt = seq // tm

    def resident(shape):
        return pl.BlockSpec(shape, lambda i: (0,) * len(shape), pipeline_mode=pl.Buffered(1))

    row_tile = pl.BlockSpec((tm, d), lambda i: (i, 0))
    out_f32 = jax.ShapeDtypeStruct((m, d), F32)
    return pl.pallas_call(
        functools.partial(_merge_kernel, last=last),
        grid=(m // tm,),
        in_specs=[
            pl.BlockSpec((tm, w), lambda i: (i, 0)),
            pl.BlockSpec((tm, w), lambda i: (i, 0)),
            pl.BlockSpec((tm, d), lambda i: (i, 0)),
            pl.BlockSpec((n_mem, 2 * w), lambda i: (i // nt, 0)),
            resident((d, w)),
            resident((d, 3 * w)),
            resident((d, 3 * d)),
            resident((3, w, d)),
            resident((d, d)),
            row_tile,
            resident((1, d)),
        ],
        out_specs=row_tile if last else (row_tile, row_tile),
        out_shape=out_f32 if last else (out_f32, jax.ShapeDtypeStruct((m, d), BF16)),
        compiler_params=_cparams(1),
        name="merge",
    )(ya, yb, h, kv, w_mq, w_g, w_mix, wb, wo, x2d, g_next.reshape(1, d))


def _rope_tables(seq):
    inv_freq = ROPE_THETA ** (-jnp.arange(ROT_HALF, dtype=F32) / ROT_HALF)
    ang = jnp.arange(seq, dtype=jnp.int32).astype(F32)[:, None] * inv_freq[None, :]
    cos, sin = jnp.cos(ang), jnp.sin(ang)
    r = np.arange(LANES) % HEAD_DIM
    f = r % ROT_HALF
    first = jnp.asarray(r < ROT_HALF)
    second = jnp.asarray((r >= ROT_HALF) & (r < 2 * ROT_HALF))
    rot = jnp.asarray(r < 2 * ROT_HALF)
    c_tok = jnp.where(rot[None, :], cos[:, f], 1.0)
    s1_tok = jnp.where(first[None, :], -sin[:, f], 0.0)
    s2_tok = jnp.where(second[None, :], sin[:, f], 0.0)
    return (c_tok, s1_tok, s2_tok), (cos.T, sin.T)


def kernel(x, mem, norm_g, w_in, mem_norm_g, w_mem_kv, w_branch, w_out, final_g):
    batch, seq, d = x.shape
    n_mem = mem.shape[1]
    depth = norm_g.shape[0]
    w = BRANCH_WIDTH
    m = batch * seq
    assert seq % MOBA_BLOCK == 0 and d % LANES == 0

    tm = min(1024, seq)
    tm_merge = min(512, seq)
    dsa_tq = 256
    dsa_kc = 256
    dsa_kcb = min(512, seq)

    tok_tabs, feat_tabs = _rope_tables(seq)
    offs = np.cumsum([0, w, w, w, w, w, w, w, w, N_HEADS * HEAD_DIM, HEAD_DIM, N_HEADS, w, w, 3 * d])
    (o_aq, o_ak, o_av, o_ag, o_bq, o_bk, o_bv, o_bg, o_iq, o_ik, o_iw, o_mq, o_mg, o_mix, o_end) = offs
    attn_scale = HEAD_DIM ** -0.5 * float(np.log2(np.e))
    idx_scale = HEAD_DIM ** -0.5

    x2d = x.reshape(m, d)
    mem2d = mem.reshape(batch * n_mem, d)
    h = _rmsnorm(x2d, norm_g[0], BF16, tm)
    for l in range(depth):
        wl = w_in[l]
        col = lambda a, b: wl[:, a:b]
        w_k = jnp.concatenate([col(o_ak, o_av), col(o_bk, o_bv), col(o_ik, o_iw), col(o_ik, o_iw)],
                              axis=1).astype(BF16)
        w_qt = jnp.concatenate([col(o_aq, o_ak) * attn_scale, col(o_bq, o_bk) * attn_scale,
                                col(o_iq, o_ik) * idx_scale], axis=1).T.astype(BF16)
        w_vt = jnp.concatenate([col(o_av, o_ag), col(o_bv, o_bg)], axis=1).T.astype(BF16)
        w_iwt = jnp.pad(col(o_iw, o_mq).T, ((0, 2 * SUBLANES - N_HEADS), (0, 0))).astype(BF16)
        w_mq = col(o_mq, o_mg).astype(BF16)
        w_g = jnp.concatenate([col(o_ag, o_bq), col(o_bg, o_iq), col(o_mg, o_mix)], axis=1).astype(BF16)
        w_mix = col(o_mix, o_end).astype(BF16)

        k_all, qt_all, vt_all, iwt = _in_proj(h, w_k, w_qt, w_vt, w_iwt, tok_tabs, feat_tabs, seq, tm)

        ya = _moba(qt_all, k_all, vt_all, batch, seq, 0, 0, 0)
        yb = _dsa(qt_all, k_all, vt_all, iwt, batch, seq, dsa_tq, dsa_kc, dsa_kcb)

        mn = _rmsnorm(mem2d, mem_norm_g[l], BF16, n_mem)
        kv = _mm_tok(mn, w_mem_kv[l].astype(BF16), BF16, n_mem, 2 * w)

        last = l == depth - 1
        res = _merge(ya, yb, h, kv, w_mq, w_g, w_mix, w_branch[l].astype(BF16), w_out[l].astype(BF16),
                     x2d, final_g if last else norm_g[l + 1], last, seq, n_mem, tm_merge)
        if not last:
            x2d, h = res

    return res.reshape(batch, seq, d)
```

```python
import functools

import jax
import jax.numpy as jnp
import numpy as np
from jax import lax
from jax.experimental import pallas as pl
from jax.experimental.pallas import tpu as pltpu

F32 = jnp.float32
BF16 = jnp.bfloat16

HEAD_DIM = 64
N_HEADS = 8
BRANCH_WIDTH = 512
M_HEADS = 4
M_HEAD_DIM = 128
MOBA_BLOCK = 256
MOBA_TOPK = 3
IDX_TOPK_MAX = 256
ROPE_THETA = 500000.0
ROT_HALF = HEAD_DIM // 4 // 2
RMS_EPS = 1e-6

LANES = 128
SUBLANES = 8
VMEM_LIMIT = 56 * 1024 * 1024
NEG_BIG = -1e30
FLT_MAX = float(np.finfo(np.float32).max)
INT_MIN = -(2 ** 31)
I16_MIN = -(2 ** 15)
KEY_NEG_INF = -0x7F800000


def _cparams(n_axes):
    return pltpu.CompilerParams(dimension_semantics=("arbitrary",) * n_axes,
                                vmem_limit_bytes=VMEM_LIMIT)


def _rmsnorm_kernel(x_ref, g_ref, o_ref):
    xf = x_ref[...]
    y = xf * lax.rsqrt(jnp.mean(xf * xf, axis=-1, keepdims=True) + RMS_EPS)
    o_ref[...] = (y * g_ref[...]).astype(o_ref.dtype)


def _rmsnorm(x2d, g, out_dtype, tm):
    m, d = x2d.shape
    return pl.pallas_call(
        _rmsnorm_kernel,
        grid=(m // tm,),
        in_specs=[pl.BlockSpec((tm, d), lambda i: (i, 0)),
                  pl.BlockSpec((1, d), lambda i: (0, 0))],
        out_specs=pl.BlockSpec((tm, d), lambda i: (i, 0)),
        out_shape=jax.ShapeDtypeStruct((m, d), out_dtype),
        compiler_params=_cparams(1),
        name="rmsnorm",
    )(x2d, g.reshape(1, d))


def _mm_tok_kernel(h_ref, w_ref, o_ref):
    o_ref[...] = jnp.dot(h_ref[...], w_ref[...], preferred_element_type=F32).astype(o_ref.dtype)


def _mm_tok(h, w, out_dtype, tm, tn):
    m, k = h.shape
    n = w.shape[1]
    return pl.pallas_call(
        _mm_tok_kernel,
        grid=(m // tm, n // tn),
        in_specs=[pl.BlockSpec((tm, k), lambda i, j: (i, 0)),
                  pl.BlockSpec((k, tn), lambda i, j: (0, j))],
        out_specs=pl.BlockSpec((tm, tn), lambda i, j: (i, j)),
        out_shape=jax.ShapeDtypeStruct((m, n), out_dtype),
        compiler_params=_cparams(2),
        name="proj_tok",
    )(h, w)


def _in_proj_kernel(h_ref, wk_ref, wqt_ref, wvt_ref, wiw_ref, c_ref, s1_ref, s2_ref, cos_ref, sin_ref,
                    k_ref, qt_ref, vt_ref, iw_ref):
    h = h_ref[...]
    nt_dims = (((1,), (1,)), ((), ()))

    k_acc = jnp.dot(h, wk_ref[...], preferred_element_type=F32)
    c, s1, s2 = c_ref[...], s1_ref[...], s2_ref[...]
    for j in range(k_acc.shape[1] // LANES):
        piece = k_acc[:, j * LANES:(j + 1) * LANES]
        up = pltpu.roll(piece, LANES - ROT_HALF, 1)
        down = pltpu.roll(piece, ROT_HALF, 1)
        k_ref[:, j * LANES:(j + 1) * LANES] = (piece * c + up * s1 + down * s2).astype(k_ref.dtype)

    cos, sin = cos_ref[...], sin_ref[...]
    w = BRANCH_WIDTH
    for blk in range(wqt_ref.shape[0] // w):
        acc = lax.dot_general(wqt_ref[blk * w:(blk + 1) * w, :], h, nt_dims, preferred_element_type=F32)
        pieces = []
        for hh in range(w // HEAD_DIM):
            base = hh * HEAD_DIM
            x1 = acc[base:base + ROT_HALF]
            x2 = acc[base + ROT_HALF:base + 2 * ROT_HALF]
            pieces += [x1 * cos - x2 * sin, x2 * cos + x1 * sin, acc[base + 2 * ROT_HALF:base + HEAD_DIM]]
        qt_ref[blk * w:(blk + 1) * w, :] = jnp.concatenate(pieces, axis=0).astype(qt_ref.dtype)

    vt_ref[...] = lax.dot_general(wvt_ref[...], h, nt_dims, preferred_element_type=F32).astype(vt_ref.dtype)
    iw_ref[...] = lax.dot_general(wiw_ref[...], h, nt_dims, preferred_element_type=F32)


def _in_proj(h, w_k, w_qt, w_vt, w_iwt, tok_tabs, feat_tabs, seq, tm):
    m, k = h.shape
    nt = seq // tm

    def resident(shape):
        return pl.BlockSpec(shape, lambda i: (0,) * len(shape), pipeline_mode=pl.Buffered(1))

    tok_tab = pl.BlockSpec((tm, LANES), lambda i: (i % nt, 0))
    feat_tab = pl.BlockSpec((SUBLANES, tm), lambda i: (0, i % nt))
    n_k, n_q, n_v, n_w = w_k.shape[1], w_qt.shape[0], w_vt.shape[0], w_iwt.shape[0]
    return pl.pallas_call(
        _in_proj_kernel,
        grid=(m // tm,),
        in_specs=[pl.BlockSpec((tm, k), lambda i: (i, 0)),
                  resident(w_k.shape), resident(w_qt.shape), resident(w_vt.shape), resident(w_iwt.shape),
                  tok_tab, tok_tab, tok_tab, feat_tab, feat_tab],
        out_specs=(pl.BlockSpec((tm, n_k), lambda i: (i, 0)),
                   pl.BlockSpec((n_q, tm), lambda i: (0, i)),
                   pl.BlockSpec((n_v, tm), lambda i: (0, i)),
                   pl.BlockSpec((n_w, tm), lambda i: (0, i))),
        out_shape=(jax.ShapeDtypeStruct((m, n_k), BF16),
                   jax.ShapeDtypeStruct((n_q, m), BF16),
                   jax.ShapeDtypeStruct((n_v, m), BF16),
                   jax.ShapeDtypeStruct((n_w, m), F32)),
        compiler_params=_cparams(1),
        name="in_proj",
    )(h, w_k, w_qt, w_vt, w_iwt, *tok_tabs, *feat_tabs)


N_PAIRS = N_HEADS // 2


def _pair_operand(qt_ref, pair):
    qp = qt_ref[pair * LANES:(pair + 1) * LANES, :].astype(F32)
    row = lax.broadcasted_iota(jnp.int32, qp.shape, 0)
    lo = jnp.where(row < HEAD_DIM, qp, 0.0)
    hi = jnp.where(row >= HEAD_DIM, qp, 0.0)
    return jnp.concatenate([lo, hi], axis=1).astype(BF16)


ONES_ROWS = 16
QK_LOOKAHEAD = 4


OFFSET_SLACK = 1.0 + 2.0 ** -5
L_FLOOR = 2.0 ** -80


def _pv_accumulate(pm, vt_tile, acc_ref, p, scale=None):
    keys = vt_tile.shape[1]
    tq = pm.shape[1] // 2
    ones = jnp.ones((ONES_ROWS, keys), BF16)
    for e in range(2):
        vt_ext = jnp.concatenate([vt_tile[e * HEAD_DIM:(e + 1) * HEAD_DIM], ones], axis=0)
        cols = slice(e * tq, (e + 1) * tq)
        upd = jnp.dot(vt_ext, pm[:, cols], preferred_element_type=F32)
        old = acc_ref[p, e] if scale is None else scale[:, cols] * acc_ref[p, e]
        acc_ref[p, e] = old + upd


def _online_update(s, vt_tile, m_ref, acc_ref, p):
    m = m_ref[p]
    m_new = jnp.maximum(m, jnp.max(s, axis=0, keepdims=True))
    _pv_accumulate(jnp.exp2(s - m_new).astype(BF16), vt_tile, acc_ref, p, scale=jnp.exp2(m - m_new))
    m_ref[p] = m_new


def _offset_update(s, vt_tile, acc_ref, p):
    _pv_accumulate(jnp.exp2(s).astype(BF16), vt_tile, acc_ref, p)


def _flash_tiles(tiles, qk, mask, update):
    ss = [qk(*t) for t in tiles[:QK_LOOKAHEAD]]
    for n, (u, p) in enumerate(tiles):
        if n + QK_LOOKAHEAD < len(tiles):
            ss.append(qk(*tiles[n + QK_LOOKAHEAD]))
        update(mask(ss[n], u, p), u, p)
        ss[n] = None


def _key_norm_bound(k_ref, kb_ref, seq, tq):
    chunk = min(512, seq)
    width = k_ref.shape[1]
    lane = lax.broadcasted_iota(jnp.int32, (width, LANES), 0)
    head = lax.broadcasted_iota(jnp.int32, (width, LANES), 1)
    group = jnp.where((lane >= head * HEAD_DIM) & (lane < (head + 1) * HEAD_DIM), 1.0, 0.0).astype(BF16)

    def body(c, mx):
        kk = k_ref[pl.ds(pl.multiple_of(c * chunk, chunk), chunk), :].astype(F32)
        n2 = jnp.dot((kk * kk).astype(BF16), group, preferred_element_type=F32)
        return jnp.maximum(mx, jnp.max(n2, axis=0, keepdims=True))
    mx = lax.fori_loop(0, seq // chunk, body, jnp.zeros((1, LANES), F32))
    for p in range(N_PAIRS):
        kb_ref[p] = jnp.concatenate([jnp.broadcast_to(mx[:, 2 * p:2 * p + 1], (1, tq)),
                                     jnp.broadcast_to(mx[:, 2 * p + 1:2 * p + 2], (1, tq))], axis=1)


def _offset_rows(qop, kb2):
    qf = qop.astype(F32)
    bound = jnp.sqrt(jnp.sum(qf * qf, axis=0, keepdims=True) * kb2) * OFFSET_SLACK
    row = lax.broadcasted_iota(jnp.int32, (ONES_ROWS, qop.shape[1]), 0)
    return jnp.where(row == 0, -bound, 0.0).astype(BF16)


def _denominator_ok(acc_ref):
    return jnp.min(acc_ref[:, :, HEAD_DIM:HEAD_DIM + 1, :]) > L_FLOOR


def _flash_finish(o_ref, ot_ref, acc_ref, tq):
    for p in range(N_PAIRS):
        for e in range(2):
            lo = p * LANES + e * HEAD_DIM
            ot_ref[lo:lo + HEAD_DIM, :] = acc_ref[p, e, :HEAD_DIM] / acc_ref[p, e, HEAD_DIM:HEAD_DIM + 1]
    o_ref[...] = ot_ref[...].T


MOBA_UNROLL = 2
DSA_UNROLL = 2
SCORE_UNROLL = 4
EARLY_EXIT_AFTER = 11


def _moba_kernel(qt_ref, k_ref, vt_ref, o_ref, kmean_ref, kb_ref, qop_ref, sel_ref, m_ref, acc_ref,
                 ot_ref, *, seq, topk):
    i = pl.program_id(1)
    blk = MOBA_BLOCK
    nb = seq // blk
    tq = blk

    @pl.when(i == 0)
    def _():
        r = lax.broadcasted_iota(jnp.int32, (nb, seq), 0)
        c = lax.broadcasted_iota(jnp.int32, (nb, seq), 1)
        member = jnp.where((c >= r * blk) & (c < (r + 1) * blk), 1.0, 0.0).astype(BF16)
        ksum = jnp.dot(member, k_ref[...], preferred_element_type=F32)
        kmean_ref[...] = (ksum * (1.0 / blk)).astype(BF16)
        _key_norm_bound(k_ref, kb_ref, seq, tq)

    blk_id = lax.broadcasted_iota(jnp.int32, (nb, 2 * tq), 0)
    past = blk_id < i
    own_start = pl.multiple_of(i * blk, blk)

    for p in range(N_PAIRS):
        lanes = slice(p * LANES, (p + 1) * LANES)
        qop = _pair_operand(qt_ref, p)
        qop_ref[p, :LANES] = qop
        qop_ref[p, LANES:LANES + ONES_ROWS] = _offset_rows(qop, kb_ref[p])
        gate = jnp.dot(kmean_ref[:, lanes], qop, preferred_element_type=F32)
        gate = jnp.where(past, gate, -jnp.inf)
        sel = jnp.zeros((nb, 2 * tq), jnp.bool_)
        for _ in range(topk):
            mx = jnp.max(gate, axis=0, keepdims=True)
            first = jnp.min(jnp.where(gate == mx, blk_id, nb), axis=0, keepdims=True)
            hit = blk_id == first
            sel = sel | hit
            gate = jnp.where(hit, -jnp.inf, gate)
        sel_ref[p] = jnp.where(sel & past, 1.0, 0.0)
        qop_ref[p, LANES + ONES_ROWS:LANES + ONES_ROWS + nb] = jnp.where(sel & past, 0.0, NEG_BIG).astype(BF16)
        qop_ref[p, LANES + ONES_ROWS + nb:] = jnp.zeros((LANES - ONES_ROWS - nb, 2 * tq), BF16)

    def lanes(p):
        return slice(p * LANES, (p + 1) * LANES)

    krow = lax.broadcasted_iota(jnp.int32, (blk, 2 * tq), 0)
    qcol = lax.broadcasted_iota(jnp.int32, (blk, 2 * tq), 1)
    causal = krow <= jnp.where(qcol < tq, qcol, qcol - tq)
    lane = lax.broadcasted_iota(jnp.int32, (blk, LANES), 1)
    ntrip = (i + MOBA_UNROLL - 1) // MOBA_UNROLL

    def attend(fast):
        acc_ref[...] = jnp.zeros(acc_ref.shape, F32)
        if not fast:
            m_ref[...] = jnp.full(m_ref.shape, NEG_BIG, F32)

        def key_operand(start, extra_lane):
            k_tile = k_ref[pl.ds(start, blk), :]
            if not fast:
                return lambda p: k_tile[:, lanes(p)]
            sel_lanes = (lane == 0) if extra_lane is None else ((lane == 0) | (lane == extra_lane))
            extra = jnp.where(sel_lanes, 1.0, 0.0).astype(BF16)
            return lambda p: jnp.concatenate([k_tile[:, lanes(p)], extra], axis=1)

        def query_operand(p):
            return qop_ref[p] if fast else qop_ref[p, :LANES]

        def update(vt_tile):
            if fast:
                return lambda s, u, p: _offset_update(s, vt_tile(u, p), acc_ref, p)
            return lambda s, u, p: _online_update(s, vt_tile(u, p), m_ref, acc_ref, p)

        own_keys = key_operand(own_start, None)
        _flash_tiles(
            [(0, p) for p in range(N_PAIRS)],
            lambda u, p: jnp.dot(own_keys(p), query_operand(p), preferred_element_type=F32),
            lambda s, u, p: jnp.where(causal, s, -jnp.inf),
            update(lambda u, p: vt_ref[lanes(p), pl.ds(own_start, blk)]))

        def body(c, carry):
            js = [c * MOBA_UNROLL + u for u in range(MOBA_UNROLL)]
            starts = [pl.multiple_of(j * blk, blk) for j in js]
            keys = [key_operand(starts[u], ONES_ROWS + js[u]) for u in range(MOBA_UNROLL)]
            _flash_tiles(
                [(u, p) for u in range(MOBA_UNROLL) for p in range(N_PAIRS)],
                lambda u, p: jnp.dot(keys[u](p), query_operand(p), preferred_element_type=F32),
                (lambda s, u, p: s) if fast else
                (lambda s, u, p: jnp.where(sel_ref[p, pl.ds(js[u], 1), :] > 0.5, s, -jnp.inf)),
                update(lambda u, p: vt_ref[lanes(p), pl.ds(starts[u], blk)]))
            return carry

        lax.fori_loop(0, ntrip, body, 0)

    attend(True)

    @pl.when(jnp.logical_not(_denominator_ok(acc_ref)))
    def _():
        attend(False)

    _flash_finish(o_ref, ot_ref, acc_ref, tq)


def _moba(qt_all, k_all, vt_all, batch, seq, q_row_blk, k_col_blk, v_row_blk):
    tq = MOBA_BLOCK
    nq = seq // tq
    nb = seq // MOBA_BLOCK
    topk = min(MOBA_TOPK, nb - 1)
    w = BRANCH_WIDTH
    return pl.pallas_call(
        functools.partial(_moba_kernel, seq=seq, topk=topk),
        grid=(batch, nq),
        in_specs=[
            pl.BlockSpec((w, tq), lambda b, i: (q_row_blk, b * nq + i)),
            pl.BlockSpec((seq, w), lambda b, i: (b, k_col_blk), pipeline_mode=pl.Buffered(1)),
            pl.BlockSpec((w, seq), lambda b, i: (v_row_blk, b), pipeline_mode=pl.Buffered(1)),
        ],
        out_specs=pl.BlockSpec((tq, w), lambda b, i: (b * nq + i, 0)),
        out_shape=jax.ShapeDtypeStruct((batch * seq, w), F32),
        scratch_shapes=[pltpu.VMEM((nb, w), BF16),
                        pltpu.VMEM((N_PAIRS, 1, 2 * tq), F32),
                        pltpu.VMEM((N_PAIRS, 2 * LANES, 2 * tq), BF16),
                        pltpu.VMEM((N_PAIRS, nb, 2 * tq), F32),
                        pltpu.VMEM((N_PAIRS, 1, 2 * tq), F32),
                        pltpu.VMEM((N_PAIRS, 2, HEAD_DIM + ONES_ROWS, tq), F32),
                        pltpu.VMEM((w, tq), F32)],
        compiler_params=_cparams(2),
        name="moba",
    )(qt_all, k_all, vt_all)


def _ordered_key(bits):
    return jnp.where(bits < 0, jnp.int32(INT_MIN) - bits, bits)


def _tree_sum(parts):
    while len(parts) > 1:
        parts = [parts[a] + parts[a + 1] for a in range(0, len(parts) - 1, 2)] + (
            [parts[-1]] if len(parts) % 2 else [])
    return parts[0]


def _dsa_kernel(qit_ref, ki_ref, wt_ref, qt_ref, k_ref, vt_ref, o_ref,
                score_ref, hi_ref, lo_ref, kb_ref, qop_ref, m_ref, acc_ref, ot_ref,
                *, tq, kc, kcb, n_sel):
    i = pl.program_id(1)
    nch = (i * tq + tq) // kc
    nchb = (i * tq + tq + kcb - 1) // kcb
    ntrip = (nch + DSA_UNROLL - 1) // DSA_UNROLL
    krow = lax.broadcasted_iota(jnp.int32, (kc, tq), 0)
    qpos = i * tq + lax.broadcasted_iota(jnp.int32, (kc, tq), 1)
    n_sel_f = float(n_sel)

    @pl.when(i == 0)
    def _():
        _key_norm_bound(k_ref, kb_ref, k_ref.shape[0], tq)

    for p in range(N_PAIRS):
        qop_ref[p, :LANES] = _pair_operand(qit_ref, p)
    w_rows = [wt_ref[h:h + 1, :] * (N_HEADS ** -0.5) for h in range(N_HEADS)]

    def score_body(c, carry, masked, unroll):
        starts = [pl.multiple_of((c * unroll + u) * kc, kc) for u in range(unroll)]
        rs = [[jnp.dot(ki_ref[pl.ds(st, kc), :], qop_ref[p, :LANES], preferred_element_type=F32)
               for p in range(N_PAIRS)] for st in starts]
        for u, start in enumerate(starts):
            acc = jnp.zeros((kc, tq), F32)
            for p in range(N_PAIRS):
                acc = acc + jnp.maximum(rs[u][p][:, :tq], 0.0) * w_rows[2 * p]
                acc = acc + jnp.maximum(rs[u][p][:, tq:], 0.0) * w_rows[2 * p + 1]
            sc = jnp.where(start + krow <= qpos, acc, -jnp.inf) if masked else acc
            score_ref[pl.ds(start, kc), :] = sc
            key = _ordered_key(lax.bitcast_convert_type(sc, jnp.int32))
            hi_ref[pl.ds(start, kc), :] = jnp.right_shift(key, 16).astype(jnp.int16)
            lo_ref[pl.ds(start, kc), :] = (key ^ jnp.int32(0x8000)).astype(jnp.int16)
        return carry

    n_full = (i * tq + 1) // kc
    n_fast = n_full // SCORE_UNROLL
    lax.fori_loop(0, n_fast, functools.partial(score_body, masked=False, unroll=SCORE_UNROLL), 0)
    lax.fori_loop(n_fast * SCORE_UNROLL, n_full, functools.partial(score_body, masked=False, unroll=1), 0)
    lax.fori_loop(n_full, nch, functools.partial(score_body, masked=True, unroll=1), 0)

    @pl.when(nch * kc < nchb * kcb)
    def _():
        pad_rows = pl.ds(pl.multiple_of(nch * kc, kc), kcb - kc)
        score_ref[pad_rows, :] = jnp.full((kcb - kc, tq), -jnp.inf, F32)
        hi_ref[pad_rows, :] = jnp.full((kcb - kc, tq), I16_MIN, jnp.int16)
        lo_ref[pad_rows, :] = jnp.full((kcb - kc, tq), I16_MIN, jnp.int16)

    def count16(ref, pred):
        n_acc = 4
        rows = 2 * SUBLANES

        def body(c, accs):
            x = ref[pl.ds(pl.multiple_of(c * kcb, kcb), kcb), :]
            accs = list(accs)
            for r in range(kcb // rows):
                hit = jnp.where(pred(x[r * rows:(r + 1) * rows]), jnp.int16(1), jnp.int16(0))
                accs[r % n_acc] = accs[r % n_acc] + hit
            return tuple(accs)
        accs = lax.fori_loop(0, nchb, body, tuple(jnp.zeros((rows, tq), jnp.int16) for _ in range(n_acc)))
        tot = _tree_sum(list(accs)).astype(jnp.int32).astype(F32)
        return jnp.sum(tot, axis=0, keepdims=True)

    def bisect16(ref, target, start, count_at_start, bits, exact=None):
        def body(it, carry):
            t, c_ge = carry
            cand = t + jnp.left_shift(jnp.int32(1), bits - 1 - it)
            cand16 = cand.astype(jnp.int16)
            cnt = count16(ref, lambda x: x >= cand16)
            ok = cnt >= target
            return jnp.where(ok, cand, t), jnp.where(ok, cnt, c_ge)

        if exact is None:
            return lax.fori_loop(0, bits, body, (start, count_at_start))
        head = lax.fori_loop(0, EARLY_EXIT_AFTER, body, (start, count_at_start))

        def unsettled(carry):
            it, _, c_ge = carry
            return (it < bits) & (jnp.max(jnp.where(exact, 1.0, c_ge - target)) > 0.0)

        def step(carry):
            it, t, c_ge = carry
            return (it + 1,) + body(it, (t, c_ge))

        return lax.while_loop(unsettled, step, (jnp.int32(EARLY_EXIT_AFTER),) + head)[1:]

    i16_min_row = jnp.full((1, tq), I16_MIN, jnp.int32)
    rows_seen = jnp.full((1, tq), (nchb * kcb).astype(F32))
    t_hi, c_ge_hi = bisect16(hi_ref, n_sel_f, i16_min_row, rows_seen, 16)
    t_hi16 = t_hi.astype(jnp.int16)
    c_gt_hi = count16(hi_ref, lambda x: x > t_hi16)

    def low_body(c, carry):
        rows = pl.ds(pl.multiple_of(c * kcb, kcb), kcb)
        lo_ref[rows, :] = jnp.where(hi_ref[rows, :] == t_hi16, lo_ref[rows, :], jnp.int16(I16_MIN))
        return carry

    lax.fori_loop(0, nchb, low_body, 0)
    t_lo, c_ge_lo = bisect16(lo_ref, n_sel_f - c_gt_hi, i16_min_row, c_ge_hi - c_gt_hi, 16,
                             exact=t_hi <= (KEY_NEG_INF >> 16))
    n_ge = c_gt_hi + c_ge_lo
    key_t = t_hi * 65536 + (t_lo + 32768)
    short = key_t <= KEY_NEG_INF
    thr = jnp.where(short, -FLT_MAX, lax.bitcast_convert_type(_ordered_key(key_t), F32))
    has_tie = jnp.max(jnp.where(short, 0.0, n_ge - n_sel_f)) > 0.0

    @pl.when(has_tie)
    def _():
        lower = (lax.broadcasted_iota(jnp.int32, (kc, kc), 1)
                 < lax.broadcasted_iota(jnp.int32, (kc, kc), 0))
        lower = jnp.where(lower, 1.0, 0.0).astype(BF16)

        def gt_body(c, cnt):
            sc = score_ref[pl.ds(pl.multiple_of(c * kc, kc), kc), :]
            return cnt + jnp.sum(jnp.where(sc > thr, 1.0, 0.0), axis=0, keepdims=True)
        n_gt = lax.fori_loop(0, nch, gt_body, jnp.zeros((1, tq), F32))
        need = n_sel_f - n_gt

        def body(c, seen):
            start = pl.multiple_of(c * kc, kc)
            sc = score_ref[pl.ds(start, kc), :]
            eq = jnp.where(sc == thr, 1.0, 0.0)
            rank = seen + jnp.dot(lower, eq.astype(BF16), preferred_element_type=F32)
            keep = (sc > thr) | ((sc == thr) & (rank < need))
            keep = keep & (start + krow <= qpos)
            score_ref[pl.ds(start, kc), :] = jnp.where(keep, jnp.inf, -jnp.inf)
            return seen + jnp.sum(eq, axis=0, keepdims=True)
        lax.fori_loop(0, nch, body, jnp.zeros((1, tq), F32))

    for p in range(N_PAIRS):
        qop = _pair_operand(qt_ref, p)
        qop_ref[p, :LANES] = qop
        qop_ref[p, LANES:LANES + ONES_ROWS] = _offset_rows(qop, kb_ref[p])
        qop_ref[p, LANES + ONES_ROWS:] = jnp.zeros((LANES - ONES_ROWS, 2 * tq), BF16)

    def lanes(p):
        return slice(p * LANES, (p + 1) * LANES)

    def attend(fast):
        acc_ref[...] = jnp.zeros(acc_ref.shape, F32)
        if not fast:
            m_ref[...] = jnp.full(m_ref.shape, NEG_BIG, F32)
        ones = jnp.ones((kc, LANES), BF16)

        def att_body(c, carry):
            starts = [pl.multiple_of((c * DSA_UNROLL + u) * kc, kc) for u in range(DSA_UNROLL)]

            def qk(u, p):
                k_tile = k_ref[pl.ds(starts[u], kc), lanes(p)]
                if fast:
                    return jnp.dot(jnp.concatenate([k_tile, ones], axis=1), qop_ref[p],
                                   preferred_element_type=F32)
                return jnp.dot(k_tile, qop_ref[p, :LANES], preferred_element_type=F32)

            def add_bias(s, u, p):
                bias = jnp.where(score_ref[pl.ds(starts[u], kc), :] >= thr, 0.0, NEG_BIG)
                return jnp.concatenate([s[:, :tq] + bias, s[:, tq:] + bias], axis=1)

            def update(s, u, p):
                vt_tile = vt_ref[lanes(p), pl.ds(starts[u], kc)]
                if fast:
                    _offset_update(s, vt_tile, acc_ref, p)
                else:
                    _online_update(s, vt_tile, m_ref, acc_ref, p)

            _flash_tiles([(u, p) for u in range(DSA_UNROLL) for p in range(N_PAIRS)], qk, add_bias, update)
            return carry

        lax.fori_loop(0, ntrip, att_body, 0)

    attend(True)

    @pl.when(jnp.logical_not(_denominator_ok(acc_ref)))
    def _():
        attend(False)

    _flash_finish(o_ref, ot_ref, acc_ref, tq)


def _dsa(qt_all, k_all, vt_all, wt, batch, seq, tq, kc, kcb):
    nq = seq // tq
    w = BRANCH_WIDTH
    n_sel = min(IDX_TOPK_MAX, seq // 4)
    assert tq % kc == 0 and kcb == DSA_UNROLL * kc and seq % kcb == 0
    ki_col_blk = (2 * w) // LANES
    return pl.pallas_call(
        functools.partial(_dsa_kernel, tq=tq, kc=kc, kcb=kcb, n_sel=n_sel),
        grid=(batch, nq),
        in_specs=[
            pl.BlockSpec((w, tq), lambda b, i: (2, b * nq + i)),
            pl.BlockSpec((seq, LANES), lambda b, i: (b, ki_col_blk), pipeline_mode=pl.Buffered(1)),
            pl.BlockSpec((2 * SUBLANES, tq), lambda b, i: (0, b * nq + i)),
            pl.BlockSpec((w, tq), lambda b, i: (1, b * nq + i)),
            pl.BlockSpec((seq, w), lambda b, i: (b, 1), pipeline_mode=pl.Buffered(1)),
            pl.BlockSpec((w, seq), lambda b, i: (1, b), pipeline_mode=pl.Buffered(1)),
        ],
        out_specs=pl.BlockSpec((tq, w), lambda b, i: (b * nq + i, 0)),
        out_shape=jax.ShapeDtypeStruct((batch * seq, w), F32),
        scratch_shapes=[pltpu.VMEM((seq, tq), F32),
                        pltpu.VMEM((seq, tq), jnp.int16),
                        pltpu.VMEM((seq, tq), jnp.int16),
                        pltpu.VMEM((N_PAIRS, 1, 2 * tq), F32),
                        pltpu.VMEM((N_PAIRS, 2 * LANES, 2 * tq), BF16),
                        pltpu.VMEM((N_PAIRS, 1, 2 * tq), F32),
                        pltpu.VMEM((N_PAIRS, 2, HEAD_DIM + ONES_ROWS, tq), F32),
                        pltpu.VMEM((w, tq), F32)],
        compiler_params=_cparams(2),
        name="dsa",
    )(qt_all, k_all, wt, qt_all, k_all, vt_all)


def _merge_kernel(ya_ref, yb_ref, h_ref, kv_ref, wmq_ref, wg_ref, wmix_ref, wb_ref, wo_ref, x_ref, gn_ref,
                  *out_refs, last):
    w = BRANCH_WIDTH
    d = x_ref.shape[1]
    h = h_ref[...]
    mq = jnp.dot(h, wmq_ref[...], preferred_element_type=F32).astype(BF16)
    ym_heads = []
    for hd in range(M_HEADS):
        cols = slice(hd * M_HEAD_DIM, (hd + 1) * M_HEAD_DIM)
        km = kv_ref[:, cols]
        vm = kv_ref[:, w + hd * M_HEAD_DIM:w + (hd + 1) * M_HEAD_DIM]
        sc = lax.dot_general(mq[:, cols], km, (((1,), (1,)), ((), ())),
                             preferred_element_type=F32) * (M_HEAD_DIM ** -0.5)
        sc = sc - jnp.max(sc, axis=-1, keepdims=True)
        e = jnp.exp(sc)
        p = e / jnp.sum(e, axis=-1, keepdims=True)
        ym_heads.append(jnp.dot(p.astype(BF16), vm, preferred_element_type=F32))
    ym = jnp.concatenate(ym_heads, axis=-1)
    ys = (ya_ref[...], yb_ref[...], ym)
    merged = None
    for n in range(3):
        g = jnp.dot(h, wg_ref[:, n * w:(n + 1) * w], preferred_element_type=F32)
        y = (ys[n] * (g * jax.nn.sigmoid(g))).astype(BF16)
        up = jnp.dot(y, wb_ref[n], preferred_element_type=F32)
        mix = jax.nn.sigmoid(jnp.dot(h, wmix_ref[:, n * d:(n + 1) * d], preferred_element_type=F32))
        term = mix * up
        merged = term if merged is None else merged + term
    x_new = x_ref[...] + jnp.dot(merged.astype(BF16), wo_ref[...], preferred_element_type=F32)
    y = x_new * lax.rsqrt(jnp.mean(x_new * x_new, axis=-1, keepdims=True) + RMS_EPS) * gn_ref[...]
    if last:
        out_refs[0][...] = y
    else:
        out_refs[0][...] = x_new
        out_refs[1][...] = y.astype(BF16)


def _merge(ya, yb, h, kv, w_mq, w_g, w_mix, wb, wo, x2d, g_next, last, seq, n_mem, tm):
    m, d = x2d.shape
    w = BRANCH_WIDTH
    nt = seq // tm

    def resident(shape):
        return pl.BlockSpec(shape, lambda i: (0,) * len(shape), pipeline_mode=pl.Buffered(1))

    row_tile = pl.BlockSpec((tm, d), lambda i: (i, 0))
    out_f32 = jax.ShapeDtypeStruct((m, d), F32)
    return pl.pallas_call(
        functools.partial(_merge_kernel, last=last),
        grid=(m // tm,),
        in_specs=[
            pl.BlockSpec((tm, w), lambda i: (i, 0)),
            pl.BlockSpec((tm, w), lambda i: (i, 0)),
            pl.BlockSpec((tm, d), lambda i: (i, 0)),
            pl.BlockSpec((n_mem, 2 * w), lambda i: (i // nt, 0)),
            resident((d, w)),
            resident((d, 3 * w)),
            resident((d, 3 * d)),
            resident((3, w, d)),
            resident((d, d)),
            row_tile,
            resident((1, d)),
        ],
        out_specs=row_tile if last else (row_tile, row_tile),
        out_shape=out_f32 if last else (out_f32, jax.ShapeDtypeStruct((m, d), BF16)),
        compiler_params=_cparams(1),
        name="merge",
    )(ya, yb, h, kv, w_mq, w_g, w_mix, wb, wo, x2d, g_next.reshape(1, d))


def _rope_tables(seq):
    inv_freq = ROPE_THETA ** (-jnp.arange(ROT_HALF, dtype=F32) / ROT_HALF)
    ang = jnp.arange(seq, dtype=jnp.int32).astype(F32)[:, None] * inv_freq[None, :]
    cos, sin = jnp.cos(ang), jnp.sin(ang)
    r = np.arange(LANES) % HEAD_DIM
    f = r % ROT_HALF
    first = jnp.asarray(r < ROT_HALF)
    second = jnp.asarray((r >= ROT_HALF) & (r < 2 * ROT_HALF))
    rot = jnp.asarray(r < 2 * ROT_HALF)
    c_tok = jnp.where(rot[None, :], cos[:, f], 1.0)
    s1_tok = jnp.where(first[None, :], -sin[:, f], 0.0)
    s2_tok = jnp.where(second[None, :], sin[:, f], 0.0)
    return (c_tok, s1_tok, s2_tok), (cos.T, sin.T)


def kernel(x, mem, norm_g, w_in, mem_norm_g, w_mem_kv, w_branch, w_out, final_g):
    batch, seq, d = x.shape
    n_mem = mem.shape[1]
    depth = norm_g.shape[0]
    w = BRANCH_WIDTH
    m = batch * seq
    assert seq % MOBA_BLOCK == 0 and d % LANES == 0

    tm = min(1024, seq)
    tm_merge = min(512, seq)
    dsa_tq = 256
    dsa_kc = 256
    dsa_kcb = min(512, seq)

    tok_tabs, feat_tabs = _rope_tables(seq)
    offs = np.cumsum([0, w, w, w, w, w, w, w, w, N_HEADS * HEAD_DIM, HEAD_DIM, N_HEADS, w, w, 3 * d])
    (o_aq, o_ak, o_av, o_ag, o_bq, o_bk, o_bv, o_bg, o_iq, o_ik, o_iw, o_mq, o_mg, o_mix, o_end) = offs
    attn_scale = HEAD_DIM ** -0.5 * float(np.log2(np.e))
    idx_scale = HEAD_DIM ** -0.5

    x2d = x.reshape(m, d)
    mem2d = mem.reshape(batch * n_mem, d)
    h = _rmsnorm(x2d, norm_g[0], BF16, tm)
    for l in range(depth):
        col = lambda a, b: lax.slice(w_in, (l, 0, a), (l + 1, d, b)).reshape(d, b - a)
        w_k = jnp.concatenate([col(o_ak, o_av), col(o_bk, o_bv), col(o_ik, o_iw), col(o_ik, o_iw)],
                              axis=1).astype(BF16)
        w_qt = jnp.concatenate([col(o_aq, o_ak) * attn_scale, col(o_bq, o_bk) * attn_scale,
                                col(o_iq, o_ik) * idx_scale], axis=1).T.astype(BF16)
        w_vt = jnp.concatenate([col(o_av, o_ag), col(o_bv, o_bg)], axis=1).T.astype(BF16)
        w_iwt = jnp.pad(col(o_iw, o_mq).T, ((0, 2 * SUBLANES - N_HEADS), (0, 0))).astype(BF16)
        w_mq = col(o_mq, o_mg).astype(BF16)
        w_g = jnp.concatenate([col(o_ag, o_bq), col(o_bg, o_iq), col(o_mg, o_mix)], axis=1).astype(BF16)
        w_mix = col(o_mix, o_end).astype(BF16)

        k_all, qt_all, vt_all, iwt = _in_proj(h, w_k, w_qt, w_vt, w_iwt, tok_tabs, feat_tabs, seq, tm)

        ya = _moba(qt_all, k_all, vt_all, batch, seq, 0, 0, 0)
        yb = _dsa(qt_all, k_all, vt_all, iwt, batch, seq, dsa_tq, dsa_kc, dsa_kcb)

        mn = _rmsnorm(mem2d, mem_norm_g[l], BF16, n_mem)
        kv = _mm_tok(mn, w_mem_kv[l].astype(BF16), BF16, n_mem, 2 * w)

        last = l == depth - 1
        res = _merge(ya, yb, h, kv, w_mq, w_g, w_mix, w_branch[l].astype(BF16), w_out[l].astype(BF16),
                     x2d, final_g if last else norm_g[l + 1], last, seq, n_mem, tm_merge)
        if not last:
            x2d, h = res

    return res.reshape(batch, seq, d)
```

```python
import functools

import jax
import jax.numpy as jnp
import numpy as np
from jax import lax
from jax.experimental import pallas as pl
from jax.experimental.pallas import tpu as pltpu

F32 = jnp.float32
BF16 = jnp.bfloat16

HEAD_DIM = 64
N_HEADS = 8
BRANCH_WIDTH = 512
M_HEADS = 4
M_HEAD_DIM = 128
MOBA_BLOCK = 256
MOBA_TOPK = 3
IDX_TOPK_MAX = 256
ROPE_THETA = 500000.0
ROT_HALF = HEAD_DIM // 4 // 2
RMS_EPS = 1e-6

LANES = 128
SUBLANES = 8
VMEM_LIMIT = 56 * 1024 * 1024
NEG_BIG = -1e30
FLT_MAX = float(np.finfo(np.float32).max)
INT_MIN = -(2 ** 31)
I16_MIN = -(2 ** 15)
KEY_NEG_INF = -0x7F800000


def _cparams(n_axes):
    return pltpu.CompilerParams(dimension_semantics=("arbitrary",) * n_axes,
                                vmem_limit_bytes=VMEM_LIMIT)


def _rmsnorm_kernel(x_ref, g_ref, o_ref):
    xf = x_ref[...]
    y = xf * lax.rsqrt(jnp.mean(xf * xf, axis=-1, keepdims=True) + RMS_EPS)
    o_ref[...] = (y * g_ref[...]).astype(o_ref.dtype)


def _rmsnorm(x2d, g, out_dtype, tm):
    m, d = x2d.shape
    return pl.pallas_call(
        _rmsnorm_kernel,
        grid=(m // tm,),
        in_specs=[pl.BlockSpec((tm, d), lambda i: (i, 0)),
                  pl.BlockSpec((1, d), lambda i: (0, 0))],
        out_specs=pl.BlockSpec((tm, d), lambda i: (i, 0)),
        out_shape=jax.ShapeDtypeStruct((m, d), out_dtype),
        compiler_params=_cparams(1),
        name="rmsnorm",
    )(x2d, g.reshape(1, d))


def _mm_tok_kernel(h_ref, w_ref, o_ref):
    o_ref[...] = jnp.dot(h_ref[...], w_ref[...], preferred_element_type=F32).astype(o_ref.dtype)


def _mm_tok(h, w, out_dtype, tm, tn):
    m, k = h.shape
    n = w.shape[1]
    return pl.pallas_call(
        _mm_tok_kernel,
        grid=(m // tm, n // tn),
        in_specs=[pl.BlockSpec((tm, k), lambda i, j: (i, 0)),
                  pl.BlockSpec((k, tn), lambda i, j: (0, j))],
        out_specs=pl.BlockSpec((tm, tn), lambda i, j: (i, j)),
        out_shape=jax.ShapeDtypeStruct((m, n), out_dtype),
        compiler_params=_cparams(2),
        name="proj_tok",
    )(h, w)


def _in_proj_kernel(h_ref, wk_ref, wqt_ref, wvt_ref, wiw_ref, c_ref, s1_ref, s2_ref, cos_ref, sin_ref,
                    k_ref, qt_ref, vt_ref, iw_ref):
    h = h_ref[...]
    nt_dims = (((1,), (1,)), ((), ()))

    k_acc = jnp.dot(h, wk_ref[...], preferred_element_type=F32)
    c, s1, s2 = c_ref[...], s1_ref[...], s2_ref[...]
    for j in range(k_acc.shape[1] // LANES):
        piece = k_acc[:, j * LANES:(j + 1) * LANES]
        up = pltpu.roll(piece, LANES - ROT_HALF, 1)
        down = pltpu.roll(piece, ROT_HALF, 1)
        k_ref[:, j * LANES:(j + 1) * LANES] = (piece * c + up * s1 + down * s2).astype(k_ref.dtype)

    cos, sin = cos_ref[...], sin_ref[...]
    w = BRANCH_WIDTH
    for blk in range(wqt_ref.shape[0] // w):
        acc = lax.dot_general(wqt_ref[blk * w:(blk + 1) * w, :], h, nt_dims, preferred_element_type=F32)
        pieces = []
        for hh in range(w // HEAD_DIM):
            base = hh * HEAD_DIM
            x1 = acc[base:base + ROT_HALF]
            x2 = acc[base + ROT_HALF:base + 2 * ROT_HALF]
            pieces += [x1 * cos - x2 * sin, x2 * cos + x1 * sin, acc[base + 2 * ROT_HALF:base + HEAD_DIM]]
        qt_ref[blk * w:(blk + 1) * w, :] = jnp.concatenate(pieces, axis=0).astype(qt_ref.dtype)

    vt_ref[...] = lax.dot_general(wvt_ref[...], h, nt_dims, preferred_element_type=F32).astype(vt_ref.dtype)
    iw_ref[...] = lax.dot_general(wiw_ref[...], h, nt_dims, preferred_element_type=F32)


def _in_proj(h, w_k, w_qt, w_vt, w_iwt, tok_tabs, feat_tabs, seq, tm):
    m, k = h.shape
    nt = seq // tm

    def resident(shape):
        return pl.BlockSpec(shape, lambda i: (0,) * len(shape), pipeline_mode=pl.Buffered(1))

    tok_tab = pl.BlockSpec((tm, LANES), lambda i: (i % nt, 0))
    feat_tab = pl.BlockSpec((SUBLANES, tm), lambda i: (0, i % nt))
    n_k, n_q, n_v, n_w = w_k.shape[1], w_qt.shape[0], w_vt.shape[0], w_iwt.shape[0]
    return pl.pallas_call(
        _in_proj_kernel,
        grid=(m // tm,),
        in_specs=[pl.BlockSpec((tm, k), lambda i: (i, 0)),
                  resident(w_k.shape), resident(w_qt.shape), resident(w_vt.shape), resident(w_iwt.shape),
                  tok_tab, tok_tab, tok_tab, feat_tab, feat_tab],
        out_specs=(pl.BlockSpec((tm, n_k), lambda i: (i, 0)),
                   pl.BlockSpec((n_q, tm), lambda i: (0, i)),
                   pl.BlockSpec((n_v, tm), lambda i: (0, i)),
                   pl.BlockSpec((n_w, tm), lambda i: (0, i))),
        out_shape=(jax.ShapeDtypeStruct((m, n_k), BF16),
                   jax.ShapeDtypeStruct((n_q, m), BF16),
                   jax.ShapeDtypeStruct((n_v, m), BF16),
                   jax.ShapeDtypeStruct((n_w, m), F32)),
        compiler_params=_cparams(1),
        name="in_proj",
    )(h, w_k, w_qt, w_vt, w_iwt, *tok_tabs, *feat_tabs)


N_PAIRS = N_HEADS // 2


def _pair_operand(qt_ref, pair):
    qp = qt_ref[pair * LANES:(pair + 1) * LANES, :].astype(F32)
    row = lax.broadcasted_iota(jnp.int32, qp.shape, 0)
    lo = jnp.where(row < HEAD_DIM, qp, 0.0)
    hi = jnp.where(row >= HEAD_DIM, qp, 0.0)
    return jnp.concatenate([lo, hi], axis=1).astype(BF16)


ONES_ROWS = 16
QK_LOOKAHEAD = 4


OFFSET_SLACK = 1.0 + 2.0 ** -5
L_FLOOR = 2.0 ** -80


def _pv_accumulate(pm, vt_tile, acc_ref, p, scale=None):
    keys = vt_tile.shape[1]
    tq = pm.shape[1] // 2
    ones = jnp.ones((ONES_ROWS, keys), BF16)
    for e in range(2):
        vt_ext = jnp.concatenate([vt_tile[e * HEAD_DIM:(e + 1) * HEAD_DIM], ones], axis=0)
        cols = slice(e * tq, (e + 1) * tq)
        upd = jnp.dot(vt_ext, pm[:, cols], preferred_element_type=F32)
        old = acc_ref[p, e] if scale is None else scale[:, cols] * acc_ref[p, e]
        acc_ref[p, e] = old + upd


def _online_update(s, vt_tile, m_ref, acc_ref, p):
    m = m_ref[p]
    m_new = jnp.maximum(m, jnp.max(s, axis=0, keepdims=True))
    _pv_accumulate(jnp.exp2(s - m_new).astype(BF16), vt_tile, acc_ref, p, scale=jnp.exp2(m - m_new))
    m_ref[p] = m_new


def _offset_update(s, vt_tile, acc_ref, p):
    _pv_accumulate(jnp.exp2(s).astype(BF16), vt_tile, acc_ref, p)


def _flash_tiles(tiles, qk, mask, update):
    ss = [qk(*t) for t in tiles[:QK_LOOKAHEAD]]
    for n, (u, p) in enumerate(tiles):
        if n + QK_LOOKAHEAD < len(tiles):
            ss.append(qk(*tiles[n + QK_LOOKAHEAD]))
        update(mask(ss[n], u, p), u, p)
        ss[n] = None


def _key_norm_bound(k_ref, kb_ref, seq, tq):
    chunk = min(512, seq)
    width = k_ref.shape[1]
    lane = lax.broadcasted_iota(jnp.int32, (width, LANES), 0)
    head = lax.broadcasted_iota(jnp.int32, (width, LANES), 1)
    group = jnp.where((lane >= head * HEAD_DIM) & (lane < (head + 1) * HEAD_DIM), 1.0, 0.0).astype(BF16)

    def body(c, mx):
        kk = k_ref[pl.ds(pl.multiple_of(c * chunk, chunk), chunk), :].astype(F32)
        n2 = jnp.dot((kk * kk).astype(BF16), group, preferred_element_type=F32)
        return jnp.maximum(mx, jnp.max(n2, axis=0, keepdims=True))
    mx = lax.fori_loop(0, seq // chunk, body, jnp.zeros((1, LANES), F32))
    for p in range(N_PAIRS):
        kb_ref[p] = jnp.concatenate([jnp.broadcast_to(mx[:, 2 * p:2 * p + 1], (1, tq)),
                                     jnp.broadcast_to(mx[:, 2 * p + 1:2 * p + 2], (1, tq))], axis=1)


def _offset_rows(qop, kb2):
    qf = qop.astype(F32)
    bound = jnp.sqrt(jnp.sum(qf * qf, axis=0, keepdims=True) * kb2) * OFFSET_SLACK
    row = lax.broadcasted_iota(jnp.int32, (ONES_ROWS, qop.shape[1]), 0)
    return jnp.where(row == 0, -bound, 0.0).astype(BF16)


def _denominator_ok(acc_ref):
    return jnp.min(acc_ref[:, :, HEAD_DIM:HEAD_DIM + 1, :]) > L_FLOOR


def _flash_finish(o_ref, ot_ref, acc_ref, tq):
    for p in range(N_PAIRS):
        for e in range(2):
            lo = p * LANES + e * HEAD_DIM
            ot_ref[lo:lo + HEAD_DIM, :] = acc_ref[p, e, :HEAD_DIM] / acc_ref[p, e, HEAD_DIM:HEAD_DIM + 1]
    o_ref[...] = ot_ref[...].T


MOBA_UNROLL = 2
DSA_UNROLL = 2
SCORE_UNROLL = 4


def _moba_kernel(qt_ref, k_ref, vt_ref, o_ref, kmean_ref, kb_ref, qop_ref, sel_ref, m_ref, acc_ref,
                 ot_ref, *, seq, topk):
    i = pl.program_id(1)
    blk = MOBA_BLOCK
    nb = seq // blk
    tq = blk

    @pl.when(i == 0)
    def _():
        r = lax.broadcasted_iota(jnp.int32, (nb, seq), 0)
        c = lax.broadcasted_iota(jnp.int32, (nb, seq), 1)
        member = jnp.where((c >= r * blk) & (c < (r + 1) * blk), 1.0, 0.0).astype(BF16)
        ksum = jnp.dot(member, k_ref[...], preferred_element_type=F32)
        kmean_ref[...] = (ksum * (1.0 / blk)).astype(BF16)
        _key_norm_bound(k_ref, kb_ref, seq, tq)

    blk_id = lax.broadcasted_iota(jnp.int32, (nb, 2 * tq), 0)
    past = blk_id < i
    own_start = pl.multiple_of(i * blk, blk)

    for p in range(N_PAIRS):
        lanes = slice(p * LANES, (p + 1) * LANES)
        qop = _pair_operand(qt_ref, p)
        qop_ref[p, :LANES] = qop
        qop_ref[p, LANES:LANES + ONES_ROWS] = _offset_rows(qop, kb_ref[p])
        gate = jnp.dot(kmean_ref[:, lanes], qop, preferred_element_type=F32)
        gate = jnp.where(past, gate, -jnp.inf)
        sel = jnp.zeros((nb, 2 * tq), jnp.bool_)
        for _ in range(topk):
            mx = jnp.max(gate, axis=0, keepdims=True)
            first = jnp.min(jnp.where(gate == mx, blk_id, nb), axis=0, keepdims=True)
            hit = blk_id == first
            sel = sel | hit
            gate = jnp.where(hit, -jnp.inf, gate)
        sel_ref[p] = jnp.where(sel & past, 1.0, 0.0)
        qop_ref[p, LANES + ONES_ROWS:LANES + ONES_ROWS + nb] = jnp.where(sel & past, 0.0, NEG_BIG).astype(BF16)
        qop_ref[p, LANES + ONES_ROWS + nb:] = jnp.zeros((LANES - ONES_ROWS - nb, 2 * tq), BF16)

    def lanes(p):
        return slice(p * LANES, (p + 1) * LANES)

    krow = lax.broadcasted_iota(jnp.int32, (blk, 2 * tq), 0)
    qcol = lax.broadcasted_iota(jnp.int32, (blk, 2 * tq), 1)
    causal = krow <= jnp.where(qcol < tq, qcol, qcol - tq)
    lane = lax.broadcasted_iota(jnp.int32, (blk, LANES), 1)
    ntrip = (i + MOBA_UNROLL - 1) // MOBA_UNROLL

    def attend(fast):
        acc_ref[...] = jnp.zeros(acc_ref.shape, F32)
        if not fast:
            m_ref[...] = jnp.full(m_ref.shape, NEG_BIG, F32)

        def key_operand(start, extra_lane):
            k_tile = k_ref[pl.ds(start, blk), :]
            if not fast:
                return lambda p: k_tile[:, lanes(p)]
            sel_lanes = (lane == 0) if extra_lane is None else ((lane == 0) | (lane == extra_lane))
            extra = jnp.where(sel_lanes, 1.0, 0.0).astype(BF16)
            return lambda p: jnp.concatenate([k_tile[:, lanes(p)], extra], axis=1)

        def query_operand(p):
            return qop_ref[p] if fast else qop_ref[p, :LANES]

        def update(vt_tile):
            if fast:
                return lambda s, u, p: _offset_update(s, vt_tile(u, p), acc_ref, p)
            return lambda s, u, p: _online_update(s, vt_tile(u, p), m_ref, acc_ref, p)

        own_keys = key_operand(own_start, None)
        _flash_tiles(
            [(0, p) for p in range(N_PAIRS)],
            lambda u, p: jnp.dot(own_keys(p), query_operand(p), preferred_element_type=F32),
            lambda s, u, p: jnp.where(causal, s, -jnp.inf),
            update(lambda u, p: vt_ref[lanes(p), pl.ds(own_start, blk)]))

        def body(c, carry):
            js = [c * MOBA_UNROLL + u for u in range(MOBA_UNROLL)]
            starts = [pl.multiple_of(j * blk, blk) for j in js]
            keys = [key_operand(starts[u], ONES_ROWS + js[u]) for u in range(MOBA_UNROLL)]
            _flash_tiles(
                [(u, p) for u in range(MOBA_UNROLL) for p in range(N_PAIRS)],
                lambda u, p: jnp.dot(keys[u](p), query_operand(p), preferred_element_type=F32),
                (lambda s, u, p: s) if fast else
                (lambda s, u, p: jnp.where(sel_ref[p, pl.ds(js[u], 1), :] > 0.5, s, -jnp.inf)),
                update(lambda u, p: vt_ref[lanes(p), pl.ds(starts[u], blk)]))
            return carry

        lax.fori_loop(0, ntrip, body, 0)

    attend(True)

    @pl.when(jnp.logical_not(_denominator_ok(acc_ref)))
    def _():
        attend(False)

    _flash_finish(o_ref, ot_ref, acc_ref, tq)


def _moba(qt_all, k_all, vt_all, batch, seq, q_row_blk, k_col_blk, v_row_blk):
    tq = MOBA_BLOCK
    nq = seq // tq
    nb = seq // MOBA_BLOCK
    topk = min(MOBA_TOPK, nb - 1)
    w = BRANCH_WIDTH
    return pl.pallas_call(
        functools.partial(_moba_kernel, seq=seq, topk=topk),
        grid=(batch, nq),
        in_specs=[
            pl.BlockSpec((w, tq), lambda b, i: (q_row_blk, b * nq + i)),
            pl.BlockSpec((seq, w), lambda b, i: (b, k_col_blk), pipeline_mode=pl.Buffered(1)),
            pl.BlockSpec((w, seq), lambda b, i: (v_row_blk, b), pipeline_mode=pl.Buffered(1)),
        ],
        out_specs=pl.BlockSpec((tq, w), lambda b, i: (b * nq + i, 0)),
        out_shape=jax.ShapeDtypeStruct((batch * seq, w), F32),
        scratch_shapes=[pltpu.VMEM((nb, w), BF16),
                        pltpu.VMEM((N_PAIRS, 1, 2 * tq), F32),
                        pltpu.VMEM((N_PAIRS, 2 * LANES, 2 * tq), BF16),
                        pltpu.VMEM((N_PAIRS, nb, 2 * tq), F32),
                        pltpu.VMEM((N_PAIRS, 1, 2 * tq), F32),
                        pltpu.VMEM((N_PAIRS, 2, HEAD_DIM + ONES_ROWS, tq), F32),
                        pltpu.VMEM((w, tq), F32)],
        compiler_params=_cparams(2),
        name="moba",
    )(qt_all, k_all, vt_all)


def _ordered_key(bits):
    return jnp.where(bits < 0, jnp.int32(INT_MIN) - bits, bits)


def _tree_sum(parts):
    while len(parts) > 1:
        parts = [parts[a] + parts[a + 1] for a in range(0, len(parts) - 1, 2)] + (
            [parts[-1]] if len(parts) % 2 else [])
    return parts[0]


def _dsa_kernel(qit_ref, ki_ref, wt_ref, qt_ref, k_ref, vt_ref, o_ref,
                score_ref, hi_ref, lo_ref, kb_ref, qop_ref, m_ref, acc_ref, ot_ref,
                *, tq, kc, kcb, n_sel):
    i = pl.program_id(1)
    nch = (i * tq + tq) // kc
    nchb = (i * tq + tq + kcb - 1) // kcb
    ntrip = (nch + DSA_UNROLL - 1) // DSA_UNROLL
    krow = lax.broadcasted_iota(jnp.int32, (kc, tq), 0)
    qpos = i * tq + lax.broadcasted_iota(jnp.int32, (kc, tq), 1)
    n_sel_f = float(n_sel)

    @pl.when(i == 0)
    def _():
        _key_norm_bound(k_ref, kb_ref, k_ref.shape[0], tq)

    for p in range(N_PAIRS):
        qop_ref[p, :LANES] = _pair_operand(qit_ref, p)
    w_rows = [wt_ref[h:h + 1, :] * (N_HEADS ** -0.5) for h in range(N_HEADS)]

    def score_body(c, carry, masked, unroll):
        starts = [pl.multiple_of((c * unroll + u) * kc, kc) for u in range(unroll)]
        rs = [[jnp.dot(ki_ref[pl.ds(st, kc), :], qop_ref[p, :LANES], preferred_element_type=F32)
               for p in range(N_PAIRS)] for st in starts]
        for u, start in enumerate(starts):
            acc = jnp.zeros((kc, tq), F32)
            for p in range(N_PAIRS):
                acc = acc + jnp.maximum(rs[u][p][:, :tq], 0.0) * w_rows[2 * p]
                acc = acc + jnp.maximum(rs[u][p][:, tq:], 0.0) * w_rows[2 * p + 1]
            sc = jnp.where(start + krow <= qpos, acc, -jnp.inf) if masked else acc
            score_ref[pl.ds(start, kc), :] = sc
            key = _ordered_key(lax.bitcast_convert_type(sc, jnp.int32))
            hi_ref[pl.ds(start, kc), :] = jnp.right_shift(key, 16).astype(jnp.int16)
            lo_ref[pl.ds(start, kc), :] = (key ^ jnp.int32(0x8000)).astype(jnp.int16)
        return carry

    n_full = (i * tq + 1) // kc
    n_fast = n_full // SCORE_UNROLL
    lax.fori_loop(0, n_fast, functools.partial(score_body, masked=False, unroll=SCORE_UNROLL), 0)
    lax.fori_loop(n_fast * SCORE_UNROLL, n_full, functools.partial(score_body, masked=False, unroll=1), 0)
    lax.fori_loop(n_full, nch, functools.partial(score_body, masked=True, unroll=1), 0)

    @pl.when(nch * kc < nchb * kcb)
    def _():
        pad_rows = pl.ds(pl.multiple_of(nch * kc, kc), kcb - kc)
        score_ref[pad_rows, :] = jnp.full((kcb - kc, tq), -jnp.inf, F32)
        hi_ref[pad_rows, :] = jnp.full((kcb - kc, tq), I16_MIN, jnp.int16)
        lo_ref[pad_rows, :] = jnp.full((kcb - kc, tq), I16_MIN, jnp.int16)

    def count16(ref, pred):
        n_acc = 4
        rows = 2 * SUBLANES

        def body(c, accs):
            x = ref[pl.ds(pl.multiple_of(c * kcb, kcb), kcb), :]
            accs = list(accs)
            for r in range(kcb // rows):
                hit = jnp.where(pred(x[r * rows:(r + 1) * rows]), jnp.int16(1), jnp.int16(0))
                accs[r % n_acc] = accs[r % n_acc] + hit
            return tuple(accs)
        accs = lax.fori_loop(0, nchb, body, tuple(jnp.zeros((rows, tq), jnp.int16) for _ in range(n_acc)))
        tot = _tree_sum(list(accs)).astype(jnp.int32).astype(F32)
        return jnp.sum(tot, axis=0, keepdims=True)

    def bisect16(ref, target, count_at_min):
        def body(it, carry):
            t, c_ge, c_gt = carry
            cand = t + jnp.left_shift(jnp.int32(1), 15 - it)
            cand16 = cand.astype(jnp.int16)
            cnt = count16(ref, lambda x: x >= cand16)
            ok = cnt >= target
            return jnp.where(ok, cand, t), jnp.where(ok, cnt, c_ge), jnp.where(ok, c_gt, cnt)
        start = jnp.full((1, tq), I16_MIN, jnp.int32)
        return lax.fori_loop(0, 16, body, (start, count_at_min, jnp.zeros((1, tq), F32)))

    rows_seen = jnp.full((1, tq), (nchb * kcb).astype(F32))
    t_hi, c_ge_hi, c_gt_hi = bisect16(hi_ref, n_sel_f, rows_seen)
    t_hi16 = t_hi.astype(jnp.int16)

    def low_body(c, carry):
        rows = pl.ds(pl.multiple_of(c * kcb, kcb), kcb)
        lo_ref[rows, :] = jnp.where(hi_ref[rows, :] == t_hi16, lo_ref[rows, :], jnp.int16(I16_MIN))
        return carry

    lax.fori_loop(0, nchb, low_body, 0)
    t_lo, c_ge_lo, _ = bisect16(lo_ref, n_sel_f - c_gt_hi, c_ge_hi - c_gt_hi)
    n_ge = c_gt_hi + c_ge_lo
    key_t = t_hi * 65536 + (t_lo + 32768)
    short = key_t <= KEY_NEG_INF
    thr = jnp.where(short, -FLT_MAX, lax.bitcast_convert_type(_ordered_key(key_t), F32))
    has_tie = jnp.max(jnp.where(short, 0.0, n_ge - n_sel_f)) > 0.0

    @pl.when(has_tie)
    def _():
        lower = (lax.broadcasted_iota(jnp.int32, (kc, kc), 1)
                 < lax.broadcasted_iota(jnp.int32, (kc, kc), 0))
        lower = jnp.where(lower, 1.0, 0.0).astype(BF16)

        def gt_body(c, cnt):
            sc = score_ref[pl.ds(pl.multiple_of(c * kc, kc), kc), :]
            return cnt + jnp.sum(jnp.where(sc > thr, 1.0, 0.0), axis=0, keepdims=True)
        n_gt = lax.fori_loop(0, nch, gt_body, jnp.zeros((1, tq), F32))
        need = n_sel_f - n_gt

        def body(c, seen):
            start = pl.multiple_of(c * kc, kc)
            sc = score_ref[pl.ds(start, kc), :]
            eq = jnp.where(sc == thr, 1.0, 0.0)
            rank = seen + jnp.dot(lower, eq.astype(BF16), preferred_element_type=F32)
            keep = (sc > thr) | ((sc == thr) & (rank < need))
            keep = keep & (start + krow <= qpos)
            score_ref[pl.ds(start, kc), :] = jnp.where(keep, jnp.inf, -jnp.inf)
            return seen + jnp.sum(eq, axis=0, keepdims=True)
        lax.fori_loop(0, nch, body, jnp.zeros((1, tq), F32))

    for p in range(N_PAIRS):
        qop = _pair_operand(qt_ref, p)
        qop_ref[p, :LANES] = qop
        qop_ref[p, LANES:LANES + ONES_ROWS] = _offset_rows(qop, kb_ref[p])
        qop_ref[p, LANES + ONES_ROWS:] = jnp.zeros((LANES - ONES_ROWS, 2 * tq), BF16)

    def lanes(p):
        return slice(p * LANES, (p + 1) * LANES)

    def attend(fast):
        acc_ref[...] = jnp.zeros(acc_ref.shape, F32)
        if not fast:
            m_ref[...] = jnp.full(m_ref.shape, NEG_BIG, F32)
        ones = jnp.ones((kc, LANES), BF16)

        def att_body(c, carry):
            starts = [pl.multiple_of((c * DSA_UNROLL + u) * kc, kc) for u in range(DSA_UNROLL)]

            def qk(u, p):
                k_tile = k_ref[pl.ds(starts[u], kc), lanes(p)]
                if fast:
                    return jnp.dot(jnp.concatenate([k_tile, ones], axis=1), qop_ref[p],
                                   preferred_element_type=F32)
                return jnp.dot(k_tile, qop_ref[p, :LANES], preferred_element_type=F32)

            def add_bias(s, u, p):
                bias = jnp.where(score_ref[pl.ds(starts[u], kc), :] >= thr, 0.0, NEG_BIG)
                return jnp.concatenate([s[:, :tq] + bias, s[:, tq:] + bias], axis=1)

            def update(s, u, p):
                vt_tile = vt_ref[lanes(p), pl.ds(starts[u], kc)]
                if fast:
                    _offset_update(s, vt_tile, acc_ref, p)
                else:
                    _online_update(s, vt_tile, m_ref, acc_ref, p)

            _flash_tiles([(u, p) for u in range(DSA_UNROLL) for p in range(N_PAIRS)], qk, add_bias, update)
            return carry

        lax.fori_loop(0, ntrip, att_body, 0)

    attend(True)

    @pl.when(jnp.logical_not(_denominator_ok(acc_ref)))
    def _():
        attend(False)

    _flash_finish(o_ref, ot_ref, acc_ref, tq)


def _dsa(qt_all, k_all, vt_all, wt, batch, seq, tq, kc, kcb):
    nq = seq // tq
    w = BRANCH_WIDTH
    n_sel = min(IDX_TOPK_MAX, seq // 4)
    assert tq % kc == 0 and kcb == DSA_UNROLL * kc and seq % kcb == 0
    ki_col_blk = (2 * w) // LANES
    return pl.pallas_call(
        functools.partial(_dsa_kernel, tq=tq, kc=kc, kcb=kcb, n_sel=n_sel),
        grid=(batch, nq),
        in_specs=[
            pl.BlockSpec((w, tq), lambda b, i: (2, b * nq + i)),
            pl.BlockSpec((seq, LANES), lambda b, i: (b, ki_col_blk), pipeline_mode=pl.Buffered(1)),
            pl.BlockSpec((2 * SUBLANES, tq), lambda b, i: (0, b * nq + i)),
            pl.BlockSpec((w, tq), lambda b, i: (1, b * nq + i)),
            pl.BlockSpec((seq, w), lambda b, i: (b, 1), pipeline_mode=pl.Buffered(1)),
            pl.BlockSpec((w, seq), lambda b, i: (1, b), pipeline_mode=pl.Buffered(1)),
        ],
        out_specs=pl.BlockSpec((tq, w), lambda b, i: (b * nq + i, 0)),
        out_shape=jax.ShapeDtypeStruct((batch * seq, w), F32),
        scratch_shapes=[pltpu.VMEM((seq, tq), F32),
                        pltpu.VMEM((seq, tq), jnp.int16),
                        pltpu.VMEM((seq, tq), jnp.int16),
                        pltpu.VMEM((N_PAIRS, 1, 2 * tq), F32),
                        pltpu.VMEM((N_PAIRS, 2 * LANES, 2 * tq), BF16),
                        pltpu.VMEM((N_PAIRS, 1, 2 * tq), F32),
                        pltpu.VMEM((N_PAIRS, 2, HEAD_DIM + ONES_ROWS, tq), F32),
                        pltpu.VMEM((w, tq), F32)],
        compiler_params=_cparams(2),
        name="dsa",
    )(qt_all, k_all, wt, qt_all, k_all, vt_all)


def _merge_kernel(ya_ref, yb_ref, h_ref, kv_ref, wmq_ref, wg_ref, wmix_ref, wb_ref, wo_ref, x_ref, gn_ref,
                  *out_refs, last):
    w = BRANCH_WIDTH
    d = x_ref.shape[1]
    h = h_ref[...]
    mq = jnp.dot(h, wmq_ref[...], preferred_element_type=F32).astype(BF16)
    ym_heads = []
    for hd in range(M_HEADS):
        cols = slice(hd * M_HEAD_DIM, (hd + 1) * M_HEAD_DIM)
        km = kv_ref[:, cols]
        vm = kv_ref[:, w + hd * M_HEAD_DIM:w + (hd + 1) * M_HEAD_DIM]
        sc = lax.dot_general(mq[:, cols], km, (((1,), (1,)), ((), ())),
                             preferred_element_type=F32) * (M_HEAD_DIM ** -0.5)
        sc = sc - jnp.max(sc, axis=-1, keepdims=True)
        e = jnp.exp(sc)
        p = e / jnp.sum(e, axis=-1, keepdims=True)
        ym_heads.append(jnp.dot(p.astype(BF16), vm, preferred_element_type=F32))
    ym = jnp.concatenate(ym_heads, axis=-1)
    ys = (ya_ref[...], yb_ref[...], ym)
    merged = None
    for n in range(3):
        g = jnp.dot(h, wg_ref[:, n * w:(n + 1) * w], preferred_element_type=F32)
        y = (ys[n] * (g * jax.nn.sigmoid(g))).astype(BF16)
        up = jnp.dot(y, wb_ref[n], preferred_element_type=F32)
        mix = jax.nn.sigmoid(jnp.dot(h, wmix_ref[:, n * d:(n + 1) * d], preferred_element_type=F32))
        term = mix * up
        merged = term if merged is None else merged + term
    x_new = x_ref[...] + jnp.dot(merged.astype(BF16), wo_ref[...], preferred_element_type=F32)
    y = x_new * lax.rsqrt(jnp.mean(x_new * x_new, axis=-1, keepdims=True) + RMS_EPS) * gn_ref[...]
    if last:
        out_refs[0][...] = y
    else:
        out_refs[0][...] = x_new
        out_refs[1][...] = y.astype(BF16)


def _merge(ya, yb, h, kv, w_mq, w_g, w_mix, wb, wo, x2d, g_next, last, seq, n_mem, tm):
    m, d = x2d.shape
    w = BRANCH_WIDTH
    nt = seq // tm

    def resident(shape):
        return pl.BlockSpec(shape, lambda i: (0,) * len(shape), pipeline_mode=pl.Buffered(1))

    row_tile = pl.BlockSpec((tm, d), lambda i: (i, 0))
    out_f32 = jax.ShapeDtypeStruct((m, d), F32)
    return pl.pallas_call(
        functools.partial(_merge_kernel, last=last),
        grid=(m // tm,),
        in_specs=[
            pl.BlockSpec((tm, w), lambda i: (i, 0)),
            pl.BlockSpec((tm, w), lambda i: (i, 0)),
            pl.BlockSpec((tm, d), lambda i: (i, 0)),
            pl.BlockSpec((n_mem, 2 * w), lambda i: (i // nt, 0)),
            resident((d, w)),
            resident((d, 3 * w)),
            resident((d, 3 * d)),
            resident((3, w, d)),
            resident((d, d)),
            row_tile,
            resident((1, d)),
        ],
        out_specs=row_tile if last else (row_tile, row_tile),
        out_shape=out_f32 if last else (out_f32, jax.ShapeDtypeStruct((m, d), BF16)),
        compiler_params=_cparams(1),
        name="merge",
    )(ya, yb, h, kv, w_mq, w_g, w_mix, wb, wo, x2d, g_next.reshape(1, d))


def _rope_tables(seq):
    inv_freq = ROPE_THETA ** (-jnp.arange(ROT_HALF, dtype=F32) / ROT_HALF)
    ang = jnp.arange(seq, dtype=jnp.int32).astype(F32)[:, None] * inv_freq[None, :]
    cos, sin = jnp.cos(ang), jnp.sin(ang)
    r = np.arange(LANES) % HEAD_DIM
    f = r % ROT_HALF
    first = jnp.asarray(r < ROT_HALF)
    second = jnp.asarray((r >= ROT_HALF) & (r < 2 * ROT_HALF))
    rot = jnp.asarray(r < 2 * ROT_HALF)
    c_tok = jnp.where(rot[None, :], cos[:, f], 1.0)
    s1_tok = jnp.where(first[None, :], -sin[:, f], 0.0)
    s2_tok = jnp.where(second[None, :], sin[:, f], 0.0)
    return (c_tok, s1_tok, s2_tok), (cos.T, sin.T)


def kernel(x, mem, norm_g, w_in, mem_norm_g, w_mem_kv, w_branch, w_out, final_g):
    batch, seq, d = x.shape
    n_mem = mem.shape[1]
    depth = norm_g.shape[0]
    w = BRANCH_WIDTH
    m = batch * seq
    assert seq % MOBA_BLOCK == 0 and d % LANES == 0

    tm = min(1024, seq)
    tm_merge = min(512, seq)
    dsa_tq = 256
    dsa_kc = 256
    dsa_kcb = min(512, seq)

    tok_tabs, feat_tabs = _rope_tables(seq)
    offs = np.cumsum([0, w, w, w, w, w, w, w, w, N_HEADS * HEAD_DIM, HEAD_DIM, N_HEADS, w, w, 3 * d])
    (o_aq, o_ak, o_av, o_ag, o_bq, o_bk, o_bv, o_bg, o_iq, o_ik, o_iw, o_mq, o_mg, o_mix, o_end) = offs
    attn_scale = HEAD_DIM ** -0.5 * float(np.log2(np.e))
    idx_scale = HEAD_DIM ** -0.5

    x2d = x.reshape(m, d)
    mem2d = mem.reshape(batch * n_mem, d)
    h = _rmsnorm(x2d, norm_g[0], BF16, tm)
    for l in range(depth):
        col = lambda a, b: lax.slice(w_in, (l, 0, a), (l + 1, d, b)).reshape(d, b - a)
        w_k = jnp.concatenate([col(o_ak, o_av), col(o_bk, o_bv), col(o_ik, o_iw), col(o_ik, o_iw)],
                              axis=1).astype(BF16)
        w_qt = jnp.concatenate([col(o_aq, o_ak) * attn_scale, col(o_bq, o_bk) * attn_scale,
                                col(o_iq, o_ik) * idx_scale], axis=1).T.astype(BF16)
        w_vt = jnp.concatenate([col(o_av, o_ag), col(o_bv, o_bg)], axis=1).T.astype(BF16)
        w_iwt = jnp.pad(col(o_iw, o_mq).T, ((0, 2 * SUBLANES - N_HEADS), (0, 0))).astype(BF16)
        w_mq = col(o_mq, o_mg).astype(BF16)
        w_g = jnp.concatenate([col(o_ag, o_bq), col(o_bg, o_iq), col(o_mg, o_mix)], axis=1).astype(BF16)
        w_mix = col(o_mix, o_end).astype(BF16)

        k_all, qt_all, vt_all, iwt = _in_proj(h, w_k, w_qt, w_vt, w_iwt, tok_tabs, feat_tabs, seq, tm)

        ya = _moba(qt_all, k_all, vt_all, batch, seq, 0, 0, 0)
        yb = _dsa(qt_all, k_all, vt_all, iwt, batch, seq, dsa_tq, dsa_kc, dsa_kcb)

        mn = _rmsnorm(mem2d, mem_norm_g[l], BF16, n_mem)
        kv = _mm_tok(mn, w_mem_kv[l].astype(BF16), BF16, n_mem, 2 * w)

        last = l == depth - 1
        res = _merge(ya, yb, h, kv, w_mq, w_g, w_mix, w_branch[l].astype(BF16), w_out[l].astype(BF16),
                     x2d, final_g if last else norm_g[l + 1], last, seq, n_mem, tm_merge)
        if not last:
            x2d, h = res

    return res.reshape(batch, seq, d)
```

```python
import functools

import jax
import jax.numpy as jnp
import numpy as np
from jax import lax
from jax.experimental import pallas as pl
from jax.experimental.pallas import tpu as pltpu

F32 = jnp.float32
BF16 = jnp.bfloat16

HEAD_DIM = 64
N_HEADS = 8
BRANCH_WIDTH = 512
M_HEADS = 4
M_HEAD_DIM = 128
MOBA_BLOCK = 256
MOBA_TOPK = 3
IDX_TOPK_MAX = 256
ROPE_THETA = 500000.0
ROT_HALF = HEAD_DIM // 4 // 2
RMS_EPS = 1e-6

LANES = 128
SUBLANES = 8
VMEM_LIMIT = 56 * 1024 * 1024
NEG_BIG = -1e30
FLT_MAX = float(np.finfo(np.float32).max)
INT_MIN = -(2 ** 31)
I16_MIN = -(2 ** 15)
KEY_NEG_INF = -0x7F800000


def _cparams(n_axes):
    return pltpu.CompilerParams(dimension_semantics=("arbitrary",) * n_axes,
                                vmem_limit_bytes=VMEM_LIMIT)


def _rmsnorm_kernel(x_ref, g_ref, o_ref):
    xf = x_ref[...]
    y = xf * lax.rsqrt(jnp.mean(xf * xf, axis=-1, keepdims=True) + RMS_EPS)
    o_ref[...] = (y * g_ref[...]).astype(o_ref.dtype)


def _rmsnorm(x2d, g, out_dtype, tm):
    m, d = x2d.shape
    return pl.pallas_call(
        _rmsnorm_kernel,
        grid=(m // tm,),
        in_specs=[pl.BlockSpec((tm, d), lambda i: (i, 0)),
                  pl.BlockSpec((1, d), lambda i: (0, 0))],
        out_specs=pl.BlockSpec((tm, d), lambda i: (i, 0)),
        out_shape=jax.ShapeDtypeStruct((m, d), out_dtype),
        compiler_params=_cparams(1),
        name="rmsnorm",
    )(x2d, g.reshape(1, d))


def _mm_tok_kernel(h_ref, w_ref, o_ref):
    o_ref[...] = jnp.dot(h_ref[...], w_ref[...], preferred_element_type=F32).astype(o_ref.dtype)


def _mm_tok(h, w, out_dtype, tm, tn):
    m, k = h.shape
    n = w.shape[1]
    return pl.pallas_call(
        _mm_tok_kernel,
        grid=(m // tm, n // tn),
        in_specs=[pl.BlockSpec((tm, k), lambda i, j: (i, 0)),
                  pl.BlockSpec((k, tn), lambda i, j: (0, j))],
        out_specs=pl.BlockSpec((tm, tn), lambda i, j: (i, j)),
        out_shape=jax.ShapeDtypeStruct((m, n), out_dtype),
        compiler_params=_cparams(2),
        name="proj_tok",
    )(h, w)


def _in_proj_kernel(h_ref, wk_ref, wqt_ref, wvt_ref, wiw_ref, c_ref, s1_ref, s2_ref, cos_ref, sin_ref,
                    k_ref, qt_ref, vt_ref, iw_ref):
    h = h_ref[...]
    nt_dims = (((1,), (1,)), ((), ()))

    k_acc = jnp.dot(h, wk_ref[...], preferred_element_type=F32)
    c, s1, s2 = c_ref[...], s1_ref[...], s2_ref[...]
    for j in range(k_acc.shape[1] // LANES):
        piece = k_acc[:, j * LANES:(j + 1) * LANES]
        up = pltpu.roll(piece, LANES - ROT_HALF, 1)
        down = pltpu.roll(piece, ROT_HALF, 1)
        k_ref[:, j * LANES:(j + 1) * LANES] = (piece * c + up * s1 + down * s2).astype(k_ref.dtype)

    cos, sin = cos_ref[...], sin_ref[...]
    w = BRANCH_WIDTH
    for blk in range(wqt_ref.shape[0] // w):
        acc = lax.dot_general(wqt_ref[blk * w:(blk + 1) * w, :], h, nt_dims, preferred_element_type=F32)
        pieces = []
        for hh in range(w // HEAD_DIM):
            base = hh * HEAD_DIM
            x1 = acc[base:base + ROT_HALF]
            x2 = acc[base + ROT_HALF:base + 2 * ROT_HALF]
            pieces += [x1 * cos - x2 * sin, x2 * cos + x1 * sin, acc[base + 2 * ROT_HALF:base + HEAD_DIM]]
        qt_ref[blk * w:(blk + 1) * w, :] = jnp.concatenate(pieces, axis=0).astype(qt_ref.dtype)

    vt_ref[...] = lax.dot_general(wvt_ref[...], h, nt_dims, preferred_element_type=F32).astype(vt_ref.dtype)
    iw_ref[...] = lax.dot_general(wiw_ref[...], h, nt_dims, preferred_element_type=F32)


def _in_proj(h, w_k, w_qt, w_vt, w_iwt, tok_tabs, feat_tabs, seq, tm):
    m, k = h.shape
    nt = seq // tm

    def resident(shape):
        return pl.BlockSpec(shape, lambda i: (0,) * len(shape), pipeline_mode=pl.Buffered(1))

    tok_tab = pl.BlockSpec((tm, LANES), lambda i: (i % nt, 0))
    feat_tab = pl.BlockSpec((SUBLANES, tm), lambda i: (0, i % nt))
    n_k, n_q, n_v, n_w = w_k.shape[1], w_qt.shape[0], w_vt.shape[0], w_iwt.shape[0]
    return pl.pallas_call(
        _in_proj_kernel,
        grid=(m // tm,),
        in_specs=[pl.BlockSpec((tm, k), lambda i: (i, 0)),
                  resident(w_k.shape), resident(w_qt.shape), resident(w_vt.shape), resident(w_iwt.shape),
                  tok_tab, tok_tab, tok_tab, feat_tab, feat_tab],
        out_specs=(pl.BlockSpec((tm, n_k), lambda i: (i, 0)),
                   pl.BlockSpec((n_q, tm), lambda i: (0, i)),
                   pl.BlockSpec((n_v, tm), lambda i: (0, i)),
                   pl.BlockSpec((n_w, tm), lambda i: (0, i))),
        out_shape=(jax.ShapeDtypeStruct((m, n_k), BF16),
                   jax.ShapeDtypeStruct((n_q, m), BF16),
                   jax.ShapeDtypeStruct((n_v, m), BF16),
                   jax.ShapeDtypeStruct((n_w, m), F32)),
        compiler_params=_cparams(1),
        name="in_proj",
    )(h, w_k, w_qt, w_vt, w_iwt, *tok_tabs, *feat_tabs)


N_PAIRS = N_HEADS // 2


def _pair_operand(qt_ref, pair):
    qp = qt_ref[pair * LANES:(pair + 1) * LANES, :].astype(F32)
    row = lax.broadcasted_iota(jnp.int32, qp.shape, 0)
    lo = jnp.where(row < HEAD_DIM, qp, 0.0)
    hi = jnp.where(row >= HEAD_DIM, qp, 0.0)
    return jnp.concatenate([lo, hi], axis=1).astype(BF16)


ONES_ROWS = 16
QK_LOOKAHEAD = 4


OFFSET_SLACK = 1.0 + 2.0 ** -5
L_FLOOR = 2.0 ** -80


def _pv_accumulate(pm, vt_tile, acc_ref, p, scale=None):
    keys = vt_tile.shape[1]
    tq = pm.shape[1] // 2
    ones = jnp.ones((ONES_ROWS, keys), BF16)
    for e in range(2):
        vt_ext = jnp.concatenate([vt_tile[e * HEAD_DIM:(e + 1) * HEAD_DIM], ones], axis=0)
        cols = slice(e * tq, (e + 1) * tq)
        upd = jnp.dot(vt_ext, pm[:, cols], preferred_element_type=F32)
        old = acc_ref[p, e] if scale is None else scale[:, cols] * acc_ref[p, e]
        acc_ref[p, e] = old + upd


def _online_update(s, vt_tile, m_ref, acc_ref, p):
    m = m_ref[p]
    m_new = jnp.maximum(m, jnp.max(s, axis=0, keepdims=True))
    _pv_accumulate(jnp.exp2(s - m_new).astype(BF16), vt_tile, acc_ref, p, scale=jnp.exp2(m - m_new))
    m_ref[p] = m_new


def _offset_update(s, vt_tile, acc_ref, p):
    _pv_accumulate(jnp.exp2(s).astype(BF16), vt_tile, acc_ref, p)


def _flash_tiles(tiles, qk, mask, update, lookahead=QK_LOOKAHEAD):
    ss = [qk(*t) for t in tiles[:lookahead]]
    for n, (u, p) in enumerate(tiles):
        if n + lookahead < len(tiles):
            ss.append(qk(*tiles[n + lookahead]))
        update(mask(ss[n], u, p), u, p)
        ss[n] = None


def _key_norm_bound(k_ref, kb_ref, seq, tq):
    chunk = min(512, seq)
    width = k_ref.shape[1]
    lane = lax.broadcasted_iota(jnp.int32, (width, LANES), 0)
    head = lax.broadcasted_iota(jnp.int32, (width, LANES), 1)
    group = jnp.where((lane >= head * HEAD_DIM) & (lane < (head + 1) * HEAD_DIM), 1.0, 0.0).astype(BF16)

    def body(c, mx):
        kk = k_ref[pl.ds(pl.multiple_of(c * chunk, chunk), chunk), :].astype(F32)
        n2 = jnp.dot((kk * kk).astype(BF16), group, preferred_element_type=F32)
        return jnp.maximum(mx, jnp.max(n2, axis=0, keepdims=True))
    mx = lax.fori_loop(0, seq // chunk, body, jnp.zeros((1, LANES), F32))
    for p in range(N_PAIRS):
        kb_ref[p] = jnp.concatenate([jnp.broadcast_to(mx[:, 2 * p:2 * p + 1], (1, tq)),
                                     jnp.broadcast_to(mx[:, 2 * p + 1:2 * p + 2], (1, tq))], axis=1)


def _offset_rows(qop, kb2):
    qf = qop.astype(F32)
    bound = jnp.sqrt(jnp.sum(qf * qf, axis=0, keepdims=True) * kb2) * OFFSET_SLACK
    row = lax.broadcasted_iota(jnp.int32, (ONES_ROWS, qop.shape[1]), 0)
    return jnp.where(row == 0, -bound, 0.0).astype(BF16)


def _denominator_ok(acc_ref):
    return jnp.min(acc_ref[:, :, HEAD_DIM:HEAD_DIM + 1, :]) > L_FLOOR


def _flash_finish(o_ref, ot_ref, acc_ref, tq):
    for p in range(N_PAIRS):
        for e in range(2):
            lo = p * LANES + e * HEAD_DIM
            ot_ref[lo:lo + HEAD_DIM, :] = acc_ref[p, e, :HEAD_DIM] / acc_ref[p, e, HEAD_DIM:HEAD_DIM + 1]
    o_ref[...] = ot_ref[...].T


MOBA_UNROLL = 2
MOBA_LOOKAHEAD = 2
DSA_UNROLL = 2
SCORE_UNROLL = 4


def _moba_kernel(qt_ref, k_ref, vt_ref, o_ref, kmean_ref, kb_ref, qop_ref, sel_ref, m_ref, acc_ref,
                 ot_ref, *, seq, topk):
    i = pl.program_id(1)
    blk = MOBA_BLOCK
    nb = seq // blk
    tq = blk

    @pl.when(i == 0)
    def _():
        r = lax.broadcasted_iota(jnp.int32, (nb, seq), 0)
        c = lax.broadcasted_iota(jnp.int32, (nb, seq), 1)
        member = jnp.where((c >= r * blk) & (c < (r + 1) * blk), 1.0, 0.0).astype(BF16)
        ksum = jnp.dot(member, k_ref[...], preferred_element_type=F32)
        kmean_ref[...] = (ksum * (1.0 / blk)).astype(BF16)
        _key_norm_bound(k_ref, kb_ref, seq, tq)

    blk_id = lax.broadcasted_iota(jnp.int32, (nb, 2 * tq), 0)
    past = blk_id < i
    own_start = pl.multiple_of(i * blk, blk)

    for p in range(N_PAIRS):
        lanes = slice(p * LANES, (p + 1) * LANES)
        qop = _pair_operand(qt_ref, p)
        qop_ref[p, :LANES] = qop
        qop_ref[p, LANES:LANES + ONES_ROWS] = _offset_rows(qop, kb_ref[p])
        gate = jnp.dot(kmean_ref[:, lanes], qop, preferred_element_type=F32)
        gate = jnp.where(past, gate, -jnp.inf)
        sel = jnp.zeros((nb, 2 * tq), jnp.bool_)
        for _ in range(topk):
            mx = jnp.max(gate, axis=0, keepdims=True)
            first = jnp.min(jnp.where(gate == mx, blk_id, nb), axis=0, keepdims=True)
            hit = blk_id == first
            sel = sel | hit
            gate = jnp.where(hit, -jnp.inf, gate)
        sel_ref[p] = jnp.where(sel & past, 1.0, 0.0)
        qop_ref[p, LANES + ONES_ROWS:LANES + ONES_ROWS + nb] = jnp.where(sel & past, 0.0, NEG_BIG).astype(BF16)
        qop_ref[p, LANES + ONES_ROWS + nb:] = jnp.zeros((LANES - ONES_ROWS - nb, 2 * tq), BF16)

    def lanes(p):
        return slice(p * LANES, (p + 1) * LANES)

    krow = lax.broadcasted_iota(jnp.int32, (blk, 2 * tq), 0)
    qcol = lax.broadcasted_iota(jnp.int32, (blk, 2 * tq), 1)
    causal = krow <= jnp.where(qcol < tq, qcol, qcol - tq)
    lane = lax.broadcasted_iota(jnp.int32, (blk, LANES), 1)
    ntrip = (i + MOBA_UNROLL - 1) // MOBA_UNROLL

    def attend(fast):
        acc_ref[...] = jnp.zeros(acc_ref.shape, F32)
        if not fast:
            m_ref[...] = jnp.full(m_ref.shape, NEG_BIG, F32)

        def key_operand(start, extra_lane):
            k_tile = k_ref[pl.ds(start, blk), :]
            if not fast:
                return lambda p: k_tile[:, lanes(p)]
            sel_lanes = (lane == 0) if extra_lane is None else ((lane == 0) | (lane == extra_lane))
            extra = jnp.where(sel_lanes, 1.0, 0.0).astype(BF16)
            return lambda p: jnp.concatenate([k_tile[:, lanes(p)], extra], axis=1)

        def query_operand(p):
            return qop_ref[p] if fast else qop_ref[p, :LANES]

        def update(vt_tile):
            if fast:
                return lambda s, u, p: _offset_update(s, vt_tile(u, p), acc_ref, p)
            return lambda s, u, p: _online_update(s, vt_tile(u, p), m_ref, acc_ref, p)

        own_keys = key_operand(own_start, None)
        _flash_tiles(
            [(0, p) for p in range(N_PAIRS)],
            lambda u, p: jnp.dot(own_keys(p), query_operand(p), preferred_element_type=F32),
            lambda s, u, p: jnp.where(causal, s, -jnp.inf),
            update(lambda u, p: vt_ref[lanes(p), pl.ds(own_start, blk)]))

        def body(c, carry):
            js = [c * MOBA_UNROLL + u for u in range(MOBA_UNROLL)]
            starts = [pl.multiple_of(j * blk, blk) for j in js]
            keys = [key_operand(starts[u], ONES_ROWS + js[u]) for u in range(MOBA_UNROLL)]
            _flash_tiles(
                [(u, p) for u in range(MOBA_UNROLL) for p in range(N_PAIRS)],
                lambda u, p: jnp.dot(keys[u](p), query_operand(p), preferred_element_type=F32),
                (lambda s, u, p: s) if fast else
                (lambda s, u, p: jnp.where(sel_ref[p, pl.ds(js[u], 1), :] > 0.5, s, -jnp.inf)),
                update(lambda u, p: vt_ref[lanes(p), pl.ds(starts[u], blk)]),
                lookahead=MOBA_LOOKAHEAD)
            return carry

        lax.fori_loop(0, ntrip, body, 0)

    attend(True)

    @pl.when(jnp.logical_not(_denominator_ok(acc_ref)))
    def _():
        attend(False)

    _flash_finish(o_ref, ot_ref, acc_ref, tq)


def _moba(qt_all, k_all, vt_all, batch, seq, q_row_blk, k_col_blk, v_row_blk):
    tq = MOBA_BLOCK
    nq = seq // tq
    nb = seq // MOBA_BLOCK
    topk = min(MOBA_TOPK, nb - 1)
    w = BRANCH_WIDTH
    return pl.pallas_call(
        functools.partial(_moba_kernel, seq=seq, topk=topk),
        grid=(batch, nq),
        in_specs=[
            pl.BlockSpec((w, tq), lambda b, i: (q_row_blk, b * nq + i)),
            pl.BlockSpec((seq, w), lambda b, i: (b, k_col_blk), pipeline_mode=pl.Buffered(1)),
            pl.BlockSpec((w, seq), lambda b, i: (v_row_blk, b), pipeline_mode=pl.Buffered(1)),
        ],
        out_specs=pl.BlockSpec((tq, w), lambda b, i: (b * nq + i, 0)),
        out_shape=jax.ShapeDtypeStruct((batch * seq, w), F32),
        scratch_shapes=[pltpu.VMEM((nb, w), BF16),
                        pltpu.VMEM((N_PAIRS, 1, 2 * tq), F32),
                        pltpu.VMEM((N_PAIRS, 2 * LANES, 2 * tq), BF16),
                        pltpu.VMEM((N_PAIRS, nb, 2 * tq), F32),
                        pltpu.VMEM((N_PAIRS, 1, 2 * tq), F32),
                        pltpu.VMEM((N_PAIRS, 2, HEAD_DIM + ONES_ROWS, tq), F32),
                        pltpu.VMEM((w, tq), F32)],
        compiler_params=_cparams(2),
        name="moba",
    )(qt_all, k_all, vt_all)


def _ordered_key(bits):
    return jnp.where(bits < 0, jnp.int32(INT_MIN) - bits, bits)


def _tree_sum(parts):
    while len(parts) > 1:
        parts = [parts[a] + parts[a + 1] for a in range(0, len(parts) - 1, 2)] + (
            [parts[-1]] if len(parts) % 2 else [])
    return parts[0]


def _dsa_kernel(qit_ref, ki_ref, wt_ref, qt_ref, k_ref, vt_ref, o_ref,
                score_ref, hi_ref, lo_ref, kb_ref, qop_ref, m_ref, acc_ref, ot_ref,
                *, tq, kc, kcb, n_sel):
    i = pl.program_id(1)
    nch = (i * tq + tq) // kc
    nchb = (i * tq + tq + kcb - 1) // kcb
    ntrip = (nch + DSA_UNROLL - 1) // DSA_UNROLL
    krow = lax.broadcasted_iota(jnp.int32, (kc, tq), 0)
    qpos = i * tq + lax.broadcasted_iota(jnp.int32, (kc, tq), 1)
    n_sel_f = float(n_sel)

    @pl.when(i == 0)
    def _():
        _key_norm_bound(k_ref, kb_ref, k_ref.shape[0], tq)

    for p in range(N_PAIRS):
        qop_ref[p, :LANES] = _pair_operand(qit_ref, p)
    w_rows = [wt_ref[h:h + 1, :] * (N_HEADS ** -0.5) for h in range(N_HEADS)]

    def score_body(c, carry, masked, unroll):
        starts = [pl.multiple_of((c * unroll + u) * kc, kc) for u in range(unroll)]
        rs = [[jnp.dot(ki_ref[pl.ds(st, kc), :], qop_ref[p, :LANES], preferred_element_type=F32)
               for p in range(N_PAIRS)] for st in starts]
        for u, start in enumerate(starts):
            acc = jnp.zeros((kc, tq), F32)
            for p in range(N_PAIRS):
                acc = acc + jnp.maximum(rs[u][p][:, :tq], 0.0) * w_rows[2 * p]
                acc = acc + jnp.maximum(rs[u][p][:, tq:], 0.0) * w_rows[2 * p + 1]
            sc = jnp.where(start + krow <= qpos, acc, -jnp.inf) if masked else acc
            score_ref[pl.ds(start, kc), :] = sc
            key = _ordered_key(lax.bitcast_convert_type(sc, jnp.int32))
            hi_ref[pl.ds(start, kc), :] = jnp.right_shift(key, 16).astype(jnp.int16)
            lo_ref[pl.ds(start, kc), :] = (key ^ jnp.int32(0x8000)).astype(jnp.int16)
        return carry

    n_full = (i * tq + 1) // kc
    n_fast = n_full // SCORE_UNROLL
    lax.fori_loop(0, n_fast, functools.partial(score_body, masked=False, unroll=SCORE_UNROLL), 0)
    lax.fori_loop(n_fast * SCORE_UNROLL, n_full, functools.partial(score_body, masked=False, unroll=1), 0)
    lax.fori_loop(n_full, nch, functools.partial(score_body, masked=True, unroll=1), 0)

    @pl.when(nch * kc < nchb * kcb)
    def _():
        pad_rows = pl.ds(pl.multiple_of(nch * kc, kc), kcb - kc)
        score_ref[pad_rows, :] = jnp.full((kcb - kc, tq), -jnp.inf, F32)
        hi_ref[pad_rows, :] = jnp.full((kcb - kc, tq), I16_MIN, jnp.int16)
        lo_ref[pad_rows, :] = jnp.full((kcb - kc, tq), I16_MIN, jnp.int16)

    def count16(ref, pred):
        n_acc = 4
        rows = 2 * SUBLANES

        def body(c, accs):
            x = ref[pl.ds(pl.multiple_of(c * kcb, kcb), kcb), :]
            accs = list(accs)
            for r in range(kcb // rows):
                hit = jnp.where(pred(x[r * rows:(r + 1) * rows]), jnp.int16(1), jnp.int16(0))
                accs[r % n_acc] = accs[r % n_acc] + hit
            return tuple(accs)
        accs = lax.fori_loop(0, nchb, body, tuple(jnp.zeros((rows, tq), jnp.int16) for _ in range(n_acc)))
        tot = _tree_sum(list(accs)).astype(jnp.int32).astype(F32)
        return jnp.sum(tot, axis=0, keepdims=True)

    def bisect16(ref, target, count_at_min):
        def body(it, carry):
            t, c_ge, c_gt = carry
            cand = t + jnp.left_shift(jnp.int32(1), 15 - it)
            cand16 = cand.astype(jnp.int16)
            cnt = count16(ref, lambda x: x >= cand16)
            ok = cnt >= target
            return jnp.where(ok, cand, t), jnp.where(ok, cnt, c_ge), jnp.where(ok, c_gt, cnt)
        start = jnp.full((1, tq), I16_MIN, jnp.int32)
        return lax.fori_loop(0, 16, body, (start, count_at_min, jnp.zeros((1, tq), F32)))

    rows_seen = jnp.full((1, tq), (nchb * kcb).astype(F32))
    t_hi, c_ge_hi, c_gt_hi = bisect16(hi_ref, n_sel_f, rows_seen)
    t_hi16 = t_hi.astype(jnp.int16)

    def low_body(c, carry):
        rows = pl.ds(pl.multiple_of(c * kcb, kcb), kcb)
        lo_ref[rows, :] = jnp.where(hi_ref[rows, :] == t_hi16, lo_ref[rows, :], jnp.int16(I16_MIN))
        return carry

    lax.fori_loop(0, nchb, low_body, 0)
    t_lo, c_ge_lo, _ = bisect16(lo_ref, n_sel_f - c_gt_hi, c_ge_hi - c_gt_hi)
    n_ge = c_gt_hi + c_ge_lo
    key_t = t_hi * 65536 + (t_lo + 32768)
    short = key_t <= KEY_NEG_INF
    thr = jnp.where(short, -FLT_MAX, lax.bitcast_convert_type(_ordered_key(key_t), F32))
    has_tie = jnp.max(jnp.where(short, 0.0, n_ge - n_sel_f)) > 0.0

    @pl.when(has_tie)
    def _():
        lower = (lax.broadcasted_iota(jnp.int32, (kc, kc), 1)
                 < lax.broadcasted_iota(jnp.int32, (kc, kc), 0))
        lower = jnp.where(lower, 1.0, 0.0).astype(BF16)

        def gt_body(c, cnt):
            sc = score_ref[pl.ds(pl.multiple_of(c * kc, kc), kc), :]
            return cnt + jnp.sum(jnp.where(sc > thr, 1.0, 0.0), axis=0, keepdims=True)
        n_gt = lax.fori_loop(0, nch, gt_body, jnp.zeros((1, tq), F32))
        need = n_sel_f - n_gt

        def body(c, seen):
            start = pl.multiple_of(c * kc, kc)
            sc = score_ref[pl.ds(start, kc), :]
            eq = jnp.where(sc == thr, 1.0, 0.0)
            rank = seen + jnp.dot(lower, eq.astype(BF16), preferred_element_type=F32)
            keep = (sc > thr) | ((sc == thr) & (rank < need))
            keep = keep & (start + krow <= qpos)
            score_ref[pl.ds(start, kc), :] = jnp.where(keep, jnp.inf, -jnp.inf)
            return seen + jnp.sum(eq, axis=0, keepdims=True)
        lax.fori_loop(0, nch, body, jnp.zeros((1, tq), F32))

    for p in range(N_PAIRS):
        qop = _pair_operand(qt_ref, p)
        qop_ref[p, :LANES] = qop
        qop_ref[p, LANES:LANES + ONES_ROWS] = _offset_rows(qop, kb_ref[p])
        qop_ref[p, LANES + ONES_ROWS:] = jnp.zeros((LANES - ONES_ROWS, 2 * tq), BF16)

    def lanes(p):
        return slice(p * LANES, (p + 1) * LANES)

    def attend(fast):
        acc_ref[...] = jnp.zeros(acc_ref.shape, F32)
        if not fast:
            m_ref[...] = jnp.full(m_ref.shape, NEG_BIG, F32)
        ones = jnp.ones((kc, LANES), BF16)

        def att_body(c, carry):
            starts = [pl.multiple_of((c * DSA_UNROLL + u) * kc, kc) for u in range(DSA_UNROLL)]

            def qk(u, p):
                k_tile = k_ref[pl.ds(starts[u], kc), lanes(p)]
                if fast:
                    return jnp.dot(jnp.concatenate([k_tile, ones], axis=1), qop_ref[p],
                                   preferred_element_type=F32)
                return jnp.dot(k_tile, qop_ref[p, :LANES], preferred_element_type=F32)

            def add_bias(s, u, p):
                bias = jnp.where(score_ref[pl.ds(starts[u], kc), :] >= thr, 0.0, NEG_BIG)
                return jnp.concatenate([s[:, :tq] + bias, s[:, tq:] + bias], axis=1)

            def update(s, u, p):
                vt_tile = vt_ref[lanes(p), pl.ds(starts[u], kc)]
                if fast:
                    _offset_update(s, vt_tile, acc_ref, p)
                else:
                    _online_update(s, vt_tile, m_ref, acc_ref, p)

            _flash_tiles([(u, p) for u in range(DSA_UNROLL) for p in range(N_PAIRS)], qk, add_bias, update)
            return carry

        lax.fori_loop(0, ntrip, att_body, 0)

    attend(True)

    @pl.when(jnp.logical_not(_denominator_ok(acc_ref)))
    def _():
        attend(False)

    _flash_finish(o_ref, ot_ref, acc_ref, tq)


def _dsa(qt_all, k_all, vt_all, wt, batch, seq, tq, kc, kcb):
    nq = seq // tq
    w = BRANCH_WIDTH
    n_sel = min(IDX_TOPK_MAX, seq // 4)
    assert tq % kc == 0 and kcb == DSA_UNROLL * kc and seq % kcb == 0
    ki_col_blk = (2 * w) // LANES
    return pl.pallas_call(
        functools.partial(_dsa_kernel, tq=tq, kc=kc, kcb=kcb, n_sel=n_sel),
        grid=(batch, nq),
        in_specs=[
            pl.BlockSpec((w, tq), lambda b, i: (2, b * nq + i)),
            pl.BlockSpec((seq, LANES), lambda b, i: (b, ki_col_blk), pipeline_mode=pl.Buffered(1)),
            pl.BlockSpec((2 * SUBLANES, tq), lambda b, i: (0, b * nq + i)),
            pl.BlockSpec((w, tq), lambda b, i: (1, b * nq + i)),
            pl.BlockSpec((seq, w), lambda b, i: (b, 1), pipeline_mode=pl.Buffered(1)),
            pl.BlockSpec((w, seq), lambda b, i: (1, b), pipeline_mode=pl.Buffered(1)),
        ],
        out_specs=pl.BlockSpec((tq, w), lambda b, i: (b * nq + i, 0)),
        out_shape=jax.ShapeDtypeStruct((batch * seq, w), F32),
        scratch_shapes=[pltpu.VMEM((seq, tq), F32),
                        pltpu.VMEM((seq, tq), jnp.int16),
                        pltpu.VMEM((seq, tq), jnp.int16),
                        pltpu.VMEM((N_PAIRS, 1, 2 * tq), F32),
                        pltpu.VMEM((N_PAIRS, 2 * LANES, 2 * tq), BF16),
                        pltpu.VMEM((N_PAIRS, 1, 2 * tq), F32),
                        pltpu.VMEM((N_PAIRS, 2, HEAD_DIM + ONES_ROWS, tq), F32),
                        pltpu.VMEM((w, tq), F32)],
        compiler_params=_cparams(2),
        name="dsa",
    )(qt_all, k_all, wt, qt_all, k_all, vt_all)


def _merge_kernel(ya_ref, yb_ref, h_ref, kv_ref, wmq_ref, wg_ref, wmix_ref, wb_ref, wo_ref, x_ref, gn_ref,
                  *out_refs, last):
    w = BRANCH_WIDTH
    d = x_ref.shape[1]
    h = h_ref[...]
    mq = jnp.dot(h, wmq_ref[...], preferred_element_type=F32).astype(BF16)
    ym_heads = []
    for hd in range(M_HEADS):
        cols = slice(hd * M_HEAD_DIM, (hd + 1) * M_HEAD_DIM)
        km = kv_ref[:, cols]
        vm = kv_ref[:, w + hd * M_HEAD_DIM:w + (hd + 1) * M_HEAD_DIM]
        sc = lax.dot_general(mq[:, cols], km, (((1,), (1,)), ((), ())),
                             preferred_element_type=F32) * (M_HEAD_DIM ** -0.5)
        sc = sc - jnp.max(sc, axis=-1, keepdims=True)
        e = jnp.exp(sc)
        p = e / jnp.sum(e, axis=-1, keepdims=True)
        ym_heads.append(jnp.dot(p.astype(BF16), vm, preferred_element_type=F32))
    ym = jnp.concatenate(ym_heads, axis=-1)
    ys = (ya_ref[...], yb_ref[...], ym)
    merged = None
    for n in range(3):
        g = jnp.dot(h, wg_ref[:, n * w:(n + 1) * w], preferred_element_type=F32)
        y = (ys[n] * (g * jax.nn.sigmoid(g))).astype(BF16)
        up = jnp.dot(y, wb_ref[n], preferred_element_type=F32)
        mix = jax.nn.sigmoid(jnp.dot(h, wmix_ref[:, n * d:(n + 1) * d], preferred_element_type=F32))
        term = mix * up
        merged = term if merged is None else merged + term
    x_new = x_ref[...] + jnp.dot(merged.astype(BF16), wo_ref[...], preferred_element_type=F32)
    y = x_new * lax.rsqrt(jnp.mean(x_new * x_new, axis=-1, keepdims=True) + RMS_EPS) * gn_ref[...]
    if last:
        out_refs[0][...] = y
    else:
        out_refs[0][...] = x_new
        out_refs[1][...] = y.astype(BF16)


def _merge(ya, yb, h, kv, w_mq, w_g, w_mix, wb, wo, x2d, g_next, last, seq, n_mem, tm):
    m, d = x2d.shape
    w = BRANCH_WIDTH
    nt = seq // tm

    def resident(shape):
        return pl.BlockSpec(shape, lambda i: (0,) * len(shape), pipeline_mode=pl.Buffered(1))

    row_tile = pl.BlockSpec((tm, d), lambda i: (i, 0))
    out_f32 = jax.ShapeDtypeStruct((m, d), F32)
    return pl.pallas_call(
        functools.partial(_merge_kernel, last=last),
        grid=(m // tm,),
        in_specs=[
            pl.BlockSpec((tm, w), lambda i: (i, 0)),
            pl.BlockSpec((tm, w), lambda i: (i, 0)),
            pl.BlockSpec((tm, d), lambda i: (i, 0)),
            pl.BlockSpec((n_mem, 2 * w), lambda i: (i // nt, 0)),
            resident((d, w)),
            resident((d, 3 * w)),
            resident((d, 3 * d)),
            resident((3, w, d)),
            resident((d, d)),
            row_tile,
            resident((1, d)),
        ],
        out_specs=row_tile if last else (row_tile, row_tile),
        out_shape=out_f32 if last else (out_f32, jax.ShapeDtypeStruct((m, d), BF16)),
        compiler_params=_cparams(1),
        name="merge",
    )(ya, yb, h, kv, w_mq, w_g, w_mix, wb, wo, x2d, g_next.reshape(1, d))


def _rope_tables(seq):
    inv_freq = ROPE_THETA ** (-jnp.arange(ROT_HALF, dtype=F32) / ROT_HALF)
    ang = jnp.arange(seq, dtype=jnp.int32).astype(F32)[:, None] * inv_freq[None, :]
    cos, sin = jnp.cos(ang), jnp.sin(ang)
    r = np.arange(LANES) % HEAD_DIM
    f = r % ROT_HALF
    first = jnp.asarray(r < ROT_HALF)
    second = jnp.asarray((r >= ROT_HALF) & (r < 2 * ROT_HALF))
    rot = jnp.asarray(r < 2 * ROT_HALF)
    c_tok = jnp.where(rot[None, :], cos[:, f], 1.0)
    s1_tok = jnp.where(first[None, :], -sin[:, f], 0.0)
    s2_tok = jnp.where(second[None, :], sin[:, f], 0.0)
    return (c_tok, s1_tok, s2_tok), (cos.T, sin.T)


def kernel(x, mem, norm_g, w_in, mem_norm_g, w_mem_kv, w_branch, w_out, final_g):
    batch, seq, d = x.shape
    n_mem = mem.shape[1]
    depth = norm_g.shape[0]
    w = BRANCH_WIDTH
    m = batch * seq
    assert seq % MOBA_BLOCK == 0 and d % LANES == 0

    tm = min(1024, seq)
    tm_merge = min(512, seq)
    dsa_tq = 256
    dsa_kc = 256
    dsa_kcb = min(512, seq)

    tok_tabs, feat_tabs = _rope_tables(seq)
    offs = np.cumsum([0, w, w, w, w, w, w, w, w, N_HEADS * HEAD_DIM, HEAD_DIM, N_HEADS, w, w, 3 * d])
    (o_aq, o_ak, o_av, o_ag, o_bq, o_bk, o_bv, o_bg, o_iq, o_ik, o_iw, o_mq, o_mg, o_mix, o_end) = offs
    attn_scale = HEAD_DIM ** -0.5 * float(np.log2(np.e))
    idx_scale = HEAD_DIM ** -0.5

    x2d = x.reshape(m, d)
    mem2d = mem.reshape(batch * n_mem, d)
    h = _rmsnorm(x2d, norm_g[0], BF16, tm)
    for l in range(depth):
        col = lambda a, b: lax.slice(w_in, (l, 0, a), (l + 1, d, b)).reshape(d, b - a)
        w_k = jnp.concatenate([col(o_ak, o_av), col(o_bk, o_bv), col(o_ik, o_iw), col(o_ik, o_iw)],
                              axis=1).astype(BF16)
        w_qt = jnp.concatenate([col(o_aq, o_ak) * attn_scale, col(o_bq, o_bk) * attn_scale,
                                col(o_iq, o_ik) * idx_scale], axis=1).T.astype(BF16)
        w_vt = jnp.concatenate([col(o_av, o_ag), col(o_bv, o_bg)], axis=1).T.astype(BF16)
        w_iwt = jnp.pad(col(o_iw, o_mq).T, ((0, 2 * SUBLANES - N_HEADS), (0, 0))).astype(BF16)
        w_mq = col(o_mq, o_mg).astype(BF16)
        w_g = jnp.concatenate([col(o_ag, o_bq), col(o_bg, o_iq), col(o_mg, o_mix)], axis=1).astype(BF16)
        w_mix = col(o_mix, o_end).astype(BF16)

        k_all, qt_all, vt_all, iwt = _in_proj(h, w_k, w_qt, w_vt, w_iwt, tok_tabs, feat_tabs, seq, tm)

        ya = _moba(qt_all, k_all, vt_all, batch, seq, 0, 0, 0)
        yb = _dsa(qt_all, k_all, vt_all, iwt, batch, seq, dsa_tq, dsa_kc, dsa_kcb)

        mn = _rmsnorm(mem2d, mem_norm_g[l], BF16, n_mem)
        kv = _mm_tok(mn, w_mem_kv[l].astype(BF16), BF16, n_mem, 2 * w)

        last = l == depth - 1
        res = _merge(ya, yb, h, kv, w_mq, w_g, w_mix, w_branch[l].astype(BF16), w_out[l].astype(BF16),
                     x2d, final_g if last else norm_g[l + 1], last, seq, n_mem, tm_merge)
        if not last:
            x2d, h = res

    return res.reshape(batch, seq, d)
```

```python
import functools

import jax
import jax.numpy as jnp
import numpy as np
from jax import lax
from jax.experimental import pallas as pl
from jax.experimental.pallas import tpu as pltpu

F32 = jnp.float32
BF16 = jnp.bfloat16

HEAD_DIM = 64
N_HEADS = 8
BRANCH_WIDTH = 512
M_HEADS = 4
M_HEAD_DIM = 128
MOBA_BLOCK = 256
MOBA_TOPK = 3
IDX_TOPK_MAX = 256
ROPE_THETA = 500000.0
ROT_HALF = HEAD_DIM // 4 // 2
RMS_EPS = 1e-6

LANES = 128
SUBLANES = 8
VMEM_LIMIT = 56 * 1024 * 1024
NEG_BIG = -1e30
FLT_MAX = float(np.finfo(np.float32).max)
INT_MIN = -(2 ** 31)
I16_MIN = -(2 ** 15)
KEY_NEG_INF = -0x7F800000


def _cparams(n_axes):
    return pltpu.CompilerParams(dimension_semantics=("arbitrary",) * n_axes,
                                vmem_limit_bytes=VMEM_LIMIT)


def _rmsnorm_kernel(x_ref, g_ref, o_ref):
    xf = x_ref[...]
    y = xf * lax.rsqrt(jnp.mean(xf * xf, axis=-1, keepdims=True) + RMS_EPS)
    o_ref[...] = (y * g_ref[...]).astype(o_ref.dtype)


def _rmsnorm(x2d, g, out_dtype, tm):
    m, d = x2d.shape
    return pl.pallas_call(
        _rmsnorm_kernel,
        grid=(m // tm,),
        in_specs=[pl.BlockSpec((tm, d), lambda i: (i, 0)),
                  pl.BlockSpec((1, d), lambda i: (0, 0))],
        out_specs=pl.BlockSpec((tm, d), lambda i: (i, 0)),
        out_shape=jax.ShapeDtypeStruct((m, d), out_dtype),
        compiler_params=_cparams(1),
        name="rmsnorm",
    )(x2d, g.reshape(1, d))


def _mm_tok_kernel(h_ref, w_ref, o_ref):
    o_ref[...] = jnp.dot(h_ref[...], w_ref[...], preferred_element_type=F32).astype(o_ref.dtype)


def _mm_tok(h, w, out_dtype, tm, tn):
    m, k = h.shape
    n = w.shape[1]
    return pl.pallas_call(
        _mm_tok_kernel,
        grid=(m // tm, n // tn),
        in_specs=[pl.BlockSpec((tm, k), lambda i, j: (i, 0)),
                  pl.BlockSpec((k, tn), lambda i, j: (0, j))],
        out_specs=pl.BlockSpec((tm, tn), lambda i, j: (i, j)),
        out_shape=jax.ShapeDtypeStruct((m, n), out_dtype),
        compiler_params=_cparams(2),
        name="proj_tok",
    )(h, w)


def _in_proj_kernel(h_ref, wk_ref, wqt_ref, wvt_ref, wiw_ref, c_ref, s1_ref, s2_ref, cos_ref, sin_ref,
                    k_ref, qt_ref, vt_ref, iw_ref):
    h = h_ref[...]
    nt_dims = (((1,), (1,)), ((), ()))

    k_acc = jnp.dot(h, wk_ref[...], preferred_element_type=F32)
    c, s1, s2 = c_ref[...], s1_ref[...], s2_ref[...]
    for j in range(k_acc.shape[1] // LANES):
        piece = k_acc[:, j * LANES:(j + 1) * LANES]
        up = pltpu.roll(piece, LANES - ROT_HALF, 1)
        down = pltpu.roll(piece, ROT_HALF, 1)
        k_ref[:, j * LANES:(j + 1) * LANES] = (piece * c + up * s1 + down * s2).astype(k_ref.dtype)

    cos, sin = cos_ref[...], sin_ref[...]
    w = BRANCH_WIDTH
    for blk in range(wqt_ref.shape[0] // w):
        acc = lax.dot_general(wqt_ref[blk * w:(blk + 1) * w, :], h, nt_dims, preferred_element_type=F32)
        pieces = []
        for hh in range(w // HEAD_DIM):
            base = hh * HEAD_DIM
            x1 = acc[base:base + ROT_HALF]
            x2 = acc[base + ROT_HALF:base + 2 * ROT_HALF]
            pieces += [x1 * cos - x2 * sin, x2 * cos + x1 * sin, acc[base + 2 * ROT_HALF:base + HEAD_DIM]]
        qt_ref[blk * w:(blk + 1) * w, :] = jnp.concatenate(pieces, axis=0).astype(qt_ref.dtype)

    vt_ref[...] = lax.dot_general(wvt_ref[...], h, nt_dims, preferred_element_type=F32).astype(vt_ref.dtype)
    iw_ref[...] = lax.dot_general(wiw_ref[...], h, nt_dims, preferred_element_type=F32)


def _in_proj(h, w_k, w_qt, w_vt, w_iwt, tok_tabs, feat_tabs, seq, tm):
    m, k = h.shape
    nt = seq // tm

    def resident(shape):
        return pl.BlockSpec(shape, lambda i: (0,) * len(shape), pipeline_mode=pl.Buffered(1))

    tok_tab = pl.BlockSpec((tm, LANES), lambda i: (i % nt, 0))
    feat_tab = pl.BlockSpec((SUBLANES, tm), lambda i: (0, i % nt))
    n_k, n_q, n_v, n_w = w_k.shape[1], w_qt.shape[0], w_vt.shape[0], w_iwt.shape[0]
    return pl.pallas_call(
        _in_proj_kernel,
        grid=(m // tm,),
        in_specs=[pl.BlockSpec((tm, k), lambda i: (i, 0)),
                  resident(w_k.shape), resident(w_qt.shape), resident(w_vt.shape), resident(w_iwt.shape),
                  tok_tab, tok_tab, tok_tab, feat_tab, feat_tab],
        out_specs=(pl.BlockSpec((tm, n_k), lambda i: (i, 0)),
                   pl.BlockSpec((n_q, tm), lambda i: (0, i)),
                   pl.BlockSpec((n_v, tm), lambda i: (0, i)),
                   pl.BlockSpec((n_w, tm), lambda i: (0, i))),
        out_shape=(jax.ShapeDtypeStruct((m, n_k), BF16),
                   jax.ShapeDtypeStruct((n_q, m), BF16),
                   jax.ShapeDtypeStruct((n_v, m), BF16),
                   jax.ShapeDtypeStruct((n_w, m), F32)),
        compiler_params=_cparams(1),
        name="in_proj",
    )(h, w_k, w_qt, w_vt, w_iwt, *tok_tabs, *feat_tabs)


N_PAIRS = N_HEADS // 2


def _pair_operand(qt_ref, pair):
    qp = qt_ref[pair * LANES:(pair + 1) * LANES, :].astype(F32)
    row = lax.broadcasted_iota(jnp.int32, qp.shape, 0)
    lo = jnp.where(row < HEAD_DIM, qp, 0.0)
    hi = jnp.where(row >= HEAD_DIM, qp, 0.0)
    return jnp.concatenate([lo, hi], axis=1).astype(BF16)


ONES_ROWS = 16
QK_LOOKAHEAD = 4


OFFSET_SLACK = 1.0 + 2.0 ** -5
L_FLOOR = 2.0 ** -80


def _pv_accumulate(pm, vt_tile, acc_ref, p, scale=None):
    keys = vt_tile.shape[1]
    tq = pm.shape[1] // 2
    ones = jnp.ones((ONES_ROWS, keys), BF16)
    for e in range(2):
        vt_ext = jnp.concatenate([vt_tile[e * HEAD_DIM:(e + 1) * HEAD_DIM], ones], axis=0)
        cols = slice(e * tq, (e + 1) * tq)
        upd = jnp.dot(vt_ext, pm[:, cols], preferred_element_type=F32)
        old = acc_ref[p, e] if scale is None else scale[:, cols] * acc_ref[p, e]
        acc_ref[p, e] = old + upd


def _online_update(s, vt_tile, m_ref, acc_ref, p):
    m = m_ref[p]
    m_new = jnp.maximum(m, jnp.max(s, axis=0, keepdims=True))
    _pv_accumulate(jnp.exp2(s - m_new).astype(BF16), vt_tile, acc_ref, p, scale=jnp.exp2(m - m_new))
    m_ref[p] = m_new


def _offset_update(s, vt_tile, acc_ref, p):
    _pv_accumulate(jnp.exp2(s).astype(BF16), vt_tile, acc_ref, p)


def _flash_tiles(tiles, qk, mask, update):
    ss = [qk(*t) for t in tiles[:QK_LOOKAHEAD]]
    for n, (u, p) in enumerate(tiles):
        if n + QK_LOOKAHEAD < len(tiles):
            ss.append(qk(*tiles[n + QK_LOOKAHEAD]))
        update(mask(ss[n], u, p), u, p)
        ss[n] = None


def _key_norm_bound(k_ref, kb_ref, seq, tq):
    chunk = min(512, seq)
    width = k_ref.shape[1]
    lane = lax.broadcasted_iota(jnp.int32, (width, LANES), 0)
    head = lax.broadcasted_iota(jnp.int32, (width, LANES), 1)
    group = jnp.where((lane >= head * HEAD_DIM) & (lane < (head + 1) * HEAD_DIM), 1.0, 0.0).astype(BF16)

    def body(c, mx):
        kk = k_ref[pl.ds(pl.multiple_of(c * chunk, chunk), chunk), :].astype(F32)
        n2 = jnp.dot((kk * kk).astype(BF16), group, preferred_element_type=F32)
        return jnp.maximum(mx, jnp.max(n2, axis=0, keepdims=True))
    mx = lax.fori_loop(0, seq // chunk, body, jnp.zeros((1, LANES), F32))
    for p in range(N_PAIRS):
        kb_ref[p] = jnp.concatenate([jnp.broadcast_to(mx[:, 2 * p:2 * p + 1], (1, tq)),
                                     jnp.broadcast_to(mx[:, 2 * p + 1:2 * p + 2], (1, tq))], axis=1)


def _offset_rows(qop, kb2):
    qf = qop.astype(F32)
    bound = jnp.sqrt(jnp.sum(qf * qf, axis=0, keepdims=True) * kb2) * OFFSET_SLACK
    row = lax.broadcasted_iota(jnp.int32, (ONES_ROWS, qop.shape[1]), 0)
    return jnp.where(row == 0, -bound, 0.0).astype(BF16)


def _denominator_ok(acc_ref):
    return jnp.min(acc_ref[:, :, HEAD_DIM:HEAD_DIM + 1, :]) > L_FLOOR


def _flash_finish(o_ref, ot_ref, acc_ref, tq):
    for p in range(N_PAIRS):
        for e in range(2):
            lo = p * LANES + e * HEAD_DIM
            ot_ref[lo:lo + HEAD_DIM, :] = acc_ref[p, e, :HEAD_DIM] / acc_ref[p, e, HEAD_DIM:HEAD_DIM + 1]
    o_ref[...] = ot_ref[...].T


MOBA_UNROLL = 2
DSA_UNROLL = 2
LONG_UNROLL = 4
SCORE_UNROLL = 4


def _moba_kernel(qt_ref, k_ref, vt_ref, o_ref, kmean_ref, kb_ref, qop_ref, sel_ref, m_ref, acc_ref,
                 ot_ref, *, seq, topk):
    i = pl.program_id(1)
    blk = MOBA_BLOCK
    nb = seq // blk
    tq = blk

    @pl.when(i == 0)
    def _():
        r = lax.broadcasted_iota(jnp.int32, (nb, seq), 0)
        c = lax.broadcasted_iota(jnp.int32, (nb, seq), 1)
        member = jnp.where((c >= r * blk) & (c < (r + 1) * blk), 1.0, 0.0).astype(BF16)
        ksum = jnp.dot(member, k_ref[...], preferred_element_type=F32)
        kmean_ref[...] = (ksum * (1.0 / blk)).astype(BF16)
        _key_norm_bound(k_ref, kb_ref, seq, tq)

    blk_id = lax.broadcasted_iota(jnp.int32, (nb, 2 * tq), 0)
    past = blk_id < i
    own_start = pl.multiple_of(i * blk, blk)

    for p in range(N_PAIRS):
        lanes = slice(p * LANES, (p + 1) * LANES)
        qop = _pair_operand(qt_ref, p)
        qop_ref[p, :LANES] = qop
        qop_ref[p, LANES:LANES + ONES_ROWS] = _offset_rows(qop, kb_ref[p])
        gate = jnp.dot(kmean_ref[:, lanes], qop, preferred_element_type=F32)
        gate = jnp.where(past, gate, -jnp.inf)
        sel = jnp.zeros((nb, 2 * tq), jnp.bool_)
        for _ in range(topk):
            mx = jnp.max(gate, axis=0, keepdims=True)
            first = jnp.min(jnp.where(gate == mx, blk_id, nb), axis=0, keepdims=True)
            hit = blk_id == first
            sel = sel | hit
            gate = jnp.where(hit, -jnp.inf, gate)
        sel_ref[p] = jnp.where(sel & past, 1.0, 0.0)
        qop_ref[p, LANES + ONES_ROWS:LANES + ONES_ROWS + nb] = jnp.where(sel & past, 0.0, NEG_BIG).astype(BF16)
        qop_ref[p, LANES + ONES_ROWS + nb:] = jnp.zeros((LANES - ONES_ROWS - nb, 2 * tq), BF16)

    def lanes(p):
        return slice(p * LANES, (p + 1) * LANES)

    krow = lax.broadcasted_iota(jnp.int32, (blk, 2 * tq), 0)
    qcol = lax.broadcasted_iota(jnp.int32, (blk, 2 * tq), 1)
    causal = krow <= jnp.where(qcol < tq, qcol, qcol - tq)
    lane = lax.broadcasted_iota(jnp.int32, (blk, LANES), 1)

    def attend(fast):
        acc_ref[...] = jnp.zeros(acc_ref.shape, F32)
        if not fast:
            m_ref[...] = jnp.full(m_ref.shape, NEG_BIG, F32)

        def key_operand(start, extra_lane):
            k_tile = k_ref[pl.ds(start, blk), :]
            if not fast:
                return lambda p: k_tile[:, lanes(p)]
            sel_lanes = (lane == 0) if extra_lane is None else ((lane == 0) | (lane == extra_lane))
            extra = jnp.where(sel_lanes, 1.0, 0.0).astype(BF16)
            return lambda p: jnp.concatenate([k_tile[:, lanes(p)], extra], axis=1)

        def query_operand(p):
            return qop_ref[p] if fast else qop_ref[p, :LANES]

        def update(vt_tile):
            if fast:
                return lambda s, u, p: _offset_update(s, vt_tile(u, p), acc_ref, p)
            return lambda s, u, p: _online_update(s, vt_tile(u, p), m_ref, acc_ref, p)

        own_keys = key_operand(own_start, None)
        _flash_tiles(
            [(0, p) for p in range(N_PAIRS)],
            lambda u, p: jnp.dot(own_keys(p), query_operand(p), preferred_element_type=F32),
            lambda s, u, p: jnp.where(causal, s, -jnp.inf),
            update(lambda u, p: vt_ref[lanes(p), pl.ds(own_start, blk)]))

        def body(c, carry, unroll, first):
            js = [first + c * unroll + u for u in range(unroll)]
            starts = [pl.multiple_of(j * blk, blk) for j in js]
            keys = [key_operand(starts[u], ONES_ROWS + js[u]) for u in range(unroll)]
            _flash_tiles(
                [(u, p) for u in range(unroll) for p in range(N_PAIRS)],
                lambda u, p: jnp.dot(keys[u](p), query_operand(p), preferred_element_type=F32),
                (lambda s, u, p: s) if fast else
                (lambda s, u, p: jnp.where(sel_ref[p, pl.ds(js[u], 1), :] > 0.5, s, -jnp.inf)),
                update(lambda u, p: vt_ref[lanes(p), pl.ds(starts[u], blk)]))
            return carry

        n_long = i // LONG_UNROLL if fast else 0
        done = n_long * LONG_UNROLL
        if fast:
            lax.fori_loop(0, n_long, functools.partial(body, unroll=LONG_UNROLL, first=0), 0)
        lax.fori_loop(0, (i - done + MOBA_UNROLL - 1) // MOBA_UNROLL,
                      functools.partial(body, unroll=MOBA_UNROLL, first=done), 0)

    attend(True)

    @pl.when(jnp.logical_not(_denominator_ok(acc_ref)))
    def _():
        attend(False)

    _flash_finish(o_ref, ot_ref, acc_ref, tq)


def _moba(qt_all, k_all, vt_all, batch, seq, q_row_blk, k_col_blk, v_row_blk):
    tq = MOBA_BLOCK
    nq = seq // tq
    nb = seq // MOBA_BLOCK
    topk = min(MOBA_TOPK, nb - 1)
    w = BRANCH_WIDTH
    return pl.pallas_call(
        functools.partial(_moba_kernel, seq=seq, topk=topk),
        grid=(batch, nq),
        in_specs=[
            pl.BlockSpec((w, tq), lambda b, i: (q_row_blk, b * nq + i)),
            pl.BlockSpec((seq, w), lambda b, i: (b, k_col_blk), pipeline_mode=pl.Buffered(1)),
            pl.BlockSpec((w, seq), lambda b, i: (v_row_blk, b), pipeline_mode=pl.Buffered(1)),
        ],
        out_specs=pl.BlockSpec((tq, w), lambda b, i: (b * nq + i, 0)),
        out_shape=jax.ShapeDtypeStruct((batch * seq, w), F32),
        scratch_shapes=[pltpu.VMEM((nb, w), BF16),
                        pltpu.VMEM((N_PAIRS, 1, 2 * tq), F32),
                        pltpu.VMEM((N_PAIRS, 2 * LANES, 2 * tq), BF16),
                        pltpu.VMEM((N_PAIRS, nb, 2 * tq), F32),
                        pltpu.VMEM((N_PAIRS, 1, 2 * tq), F32),
                        pltpu.VMEM((N_PAIRS, 2, HEAD_DIM + ONES_ROWS, tq), F32),
                        pltpu.VMEM((w, tq), F32)],
        compiler_params=_cparams(2),
        name="moba",
    )(qt_all, k_all, vt_all)


def _ordered_key(bits):
    return jnp.where(bits < 0, jnp.int32(INT_MIN) - bits, bits)


def _tree_sum(parts):
    while len(parts) > 1:
        parts = [parts[a] + parts[a + 1] for a in range(0, len(parts) - 1, 2)] + (
            [parts[-1]] if len(parts) % 2 else [])
    return parts[0]


def _dsa_kernel(qit_ref, ki_ref, wt_ref, qt_ref, k_ref, vt_ref, o_ref,
                score_ref, hi_ref, lo_ref, kb_ref, qop_ref, m_ref, acc_ref, ot_ref,
                *, tq, kc, kcb, n_sel):
    i = pl.program_id(1)
    nch = (i * tq + tq) // kc
    nchb = (i * tq + tq + kcb - 1) // kcb
    krow = lax.broadcasted_iota(jnp.int32, (kc, tq), 0)
    qpos = i * tq + lax.broadcasted_iota(jnp.int32, (kc, tq), 1)
    n_sel_f = float(n_sel)

    @pl.when(i == 0)
    def _():
        _key_norm_bound(k_ref, kb_ref, k_ref.shape[0], tq)

    for p in range(N_PAIRS):
        qop_ref[p, :LANES] = _pair_operand(qit_ref, p)
    w_rows = [wt_ref[h:h + 1, :] * (N_HEADS ** -0.5) for h in range(N_HEADS)]

    def score_body(c, carry, masked, unroll):
        starts = [pl.multiple_of((c * unroll + u) * kc, kc) for u in range(unroll)]
        rs = [[jnp.dot(ki_ref[pl.ds(st, kc), :], qop_ref[p, :LANES], preferred_element_type=F32)
               for p in range(N_PAIRS)] for st in starts]
        for u, start in enumerate(starts):
            acc = jnp.zeros((kc, tq), F32)
            for p in range(N_PAIRS):
                acc = acc + jnp.maximum(rs[u][p][:, :tq], 0.0) * w_rows[2 * p]
                acc = acc + jnp.maximum(rs[u][p][:, tq:], 0.0) * w_rows[2 * p + 1]
            sc = jnp.where(start + krow <= qpos, acc, -jnp.inf) if masked else acc
            score_ref[pl.ds(start, kc), :] = sc
            key = _ordered_key(lax.bitcast_convert_type(sc, jnp.int32))
            hi_ref[pl.ds(start, kc), :] = jnp.right_shift(key, 16).astype(jnp.int16)
            lo_ref[pl.ds(start, kc), :] = (key ^ jnp.int32(0x8000)).astype(jnp.int16)
        return carry

    n_full = (i * tq + 1) // kc
    n_fast = n_full // SCORE_UNROLL
    lax.fori_loop(0, n_fast, functools.partial(score_body, masked=False, unroll=SCORE_UNROLL), 0)
    lax.fori_loop(n_fast * SCORE_UNROLL, n_full, functools.partial(score_body, masked=False, unroll=1), 0)
    lax.fori_loop(n_full, nch, functools.partial(score_body, masked=True, unroll=1), 0)

    @pl.when(nch * kc < nchb * kcb)
    def _():
        pad_rows = pl.ds(pl.multiple_of(nch * kc, kc), kcb - kc)
        score_ref[pad_rows, :] = jnp.full((kcb - kc, tq), -jnp.inf, F32)
        hi_ref[pad_rows, :] = jnp.full((kcb - kc, tq), I16_MIN, jnp.int16)
        lo_ref[pad_rows, :] = jnp.full((kcb - kc, tq), I16_MIN, jnp.int16)

    def count16(ref, pred):
        n_acc = 4
        rows = 2 * SUBLANES

        def body(c, accs):
            x = ref[pl.ds(pl.multiple_of(c * kcb, kcb), kcb), :]
            accs = list(accs)
            for r in range(kcb // rows):
                hit = jnp.where(pred(x[r * rows:(r + 1) * rows]), jnp.int16(1), jnp.int16(0))
                accs[r % n_acc] = accs[r % n_acc] + hit
            return tuple(accs)
        accs = lax.fori_loop(0, nchb, body, tuple(jnp.zeros((rows, tq), jnp.int16) for _ in range(n_acc)))
        tot = _tree_sum(list(accs)).astype(jnp.int32).astype(F32)
        return jnp.sum(tot, axis=0, keepdims=True)

    def bisect16(ref, target, count_at_min):
        def body(it, carry):
            t, c_ge, c_gt = carry
            cand = t + jnp.left_shift(jnp.int32(1), 15 - it)
            cand16 = cand.astype(jnp.int16)
            cnt = count16(ref, lambda x: x >= cand16)
            ok = cnt >= target
            return jnp.where(ok, cand, t), jnp.where(ok, cnt, c_ge), jnp.where(ok, c_gt, cnt)
        start = jnp.full((1, tq), I16_MIN, jnp.int32)
        return lax.fori_loop(0, 16, body, (start, count_at_min, jnp.zeros((1, tq), F32)))

    rows_seen = jnp.full((1, tq), (nchb * kcb).astype(F32))
    t_hi, c_ge_hi, c_gt_hi = bisect16(hi_ref, n_sel_f, rows_seen)
    t_hi16 = t_hi.astype(jnp.int16)

    def low_body(c, carry):
        rows = pl.ds(pl.multiple_of(c * kcb, kcb), kcb)
        lo_ref[rows, :] = jnp.where(hi_ref[rows, :] == t_hi16, lo_ref[rows, :], jnp.int16(I16_MIN))
        return carry

    lax.fori_loop(0, nchb, low_body, 0)
    t_lo, c_ge_lo, _ = bisect16(lo_ref, n_sel_f - c_gt_hi, c_ge_hi - c_gt_hi)
    n_ge = c_gt_hi + c_ge_lo
    key_t = t_hi * 65536 + (t_lo + 32768)
    short = key_t <= KEY_NEG_INF
    thr = jnp.where(short, -FLT_MAX, lax.bitcast_convert_type(_ordered_key(key_t), F32))
    has_tie = jnp.max(jnp.where(short, 0.0, n_ge - n_sel_f)) > 0.0

    @pl.when(has_tie)
    def _():
        lower = (lax.broadcasted_iota(jnp.int32, (kc, kc), 1)
                 < lax.broadcasted_iota(jnp.int32, (kc, kc), 0))
        lower = jnp.where(lower, 1.0, 0.0).astype(BF16)

        def gt_body(c, cnt):
            sc = score_ref[pl.ds(pl.multiple_of(c * kc, kc), kc), :]
            return cnt + jnp.sum(jnp.where(sc > thr, 1.0, 0.0), axis=0, keepdims=True)
        n_gt = lax.fori_loop(0, nch, gt_body, jnp.zeros((1, tq), F32))
        need = n_sel_f - n_gt

        def body(c, seen):
            start = pl.multiple_of(c * kc, kc)
            sc = score_ref[pl.ds(start, kc), :]
            eq = jnp.where(sc == thr, 1.0, 0.0)
            rank = seen + jnp.dot(lower, eq.astype(BF16), preferred_element_type=F32)
            keep = (sc > thr) | ((sc == thr) & (rank < need))
            keep = keep & (start + krow <= qpos)
            score_ref[pl.ds(start, kc), :] = jnp.where(keep, jnp.inf, -jnp.inf)
            return seen + jnp.sum(eq, axis=0, keepdims=True)
        lax.fori_loop(0, nch, body, jnp.zeros((1, tq), F32))

    for p in range(N_PAIRS):
        qop = _pair_operand(qt_ref, p)
        qop_ref[p, :LANES] = qop
        qop_ref[p, LANES:LANES + ONES_ROWS] = _offset_rows(qop, kb_ref[p])
        qop_ref[p, LANES + ONES_ROWS:] = jnp.zeros((LANES - ONES_ROWS, 2 * tq), BF16)

    def lanes(p):
        return slice(p * LANES, (p + 1) * LANES)

    def attend(fast):
        acc_ref[...] = jnp.zeros(acc_ref.shape, F32)
        if not fast:
            m_ref[...] = jnp.full(m_ref.shape, NEG_BIG, F32)
        ones = jnp.ones((kc, LANES), BF16)

        def att_body(c, carry, unroll, first):
            starts = [pl.multiple_of((first + c * unroll + u) * kc, kc) for u in range(unroll)]

            def qk(u, p):
                k_tile = k_ref[pl.ds(starts[u], kc), lanes(p)]
                if fast:
                    return jnp.dot(jnp.concatenate([k_tile, ones], axis=1), qop_ref[p],
                                   preferred_element_type=F32)
                return jnp.dot(k_tile, qop_ref[p, :LANES], preferred_element_type=F32)

            def add_bias(s, u, p):
                bias = jnp.where(score_ref[pl.ds(starts[u], kc), :] >= thr, 0.0, NEG_BIG)
                return jnp.concatenate([s[:, :tq] + bias, s[:, tq:] + bias], axis=1)

            def update(s, u, p):
                vt_tile = vt_ref[lanes(p), pl.ds(starts[u], kc)]
                if fast:
                    _offset_update(s, vt_tile, acc_ref, p)
                else:
                    _online_update(s, vt_tile, m_ref, acc_ref, p)

            _flash_tiles([(u, p) for u in range(unroll) for p in range(N_PAIRS)], qk, add_bias, update)
            return carry

        n_long = nch // LONG_UNROLL if fast else 0
        done = n_long * LONG_UNROLL
        if fast:
            lax.fori_loop(0, n_long, functools.partial(att_body, unroll=LONG_UNROLL, first=0), 0)
        lax.fori_loop(0, (nch - done + DSA_UNROLL - 1) // DSA_UNROLL,
                      functools.partial(att_body, unroll=DSA_UNROLL, first=done), 0)

    attend(True)

    @pl.when(jnp.logical_not(_denominator_ok(acc_ref)))
    def _():
        attend(False)

    _flash_finish(o_ref, ot_ref, acc_ref, tq)


def _dsa(qt_all, k_all, vt_all, wt, batch, seq, tq, kc, kcb):
    nq = seq // tq
    w = BRANCH_WIDTH
    n_sel = min(IDX_TOPK_MAX, seq // 4)
    assert tq % kc == 0 and kcb == DSA_UNROLL * kc and seq % kcb == 0
    ki_col_blk = (2 * w) // LANES
    return pl.pallas_call(
        functools.partial(_dsa_kernel, tq=tq, kc=kc, kcb=kcb, n_sel=n_sel),
        grid=(batch, nq),
        in_specs=[
            pl.BlockSpec((w, tq), lambda b, i: (2, b * nq + i)),
            pl.BlockSpec((seq, LANES), lambda b, i: (b, ki_col_blk), pipeline_mode=pl.Buffered(1)),
            pl.BlockSpec((2 * SUBLANES, tq), lambda b, i: (0, b * nq + i)),
            pl.BlockSpec((w, tq), lambda b, i: (1, b * nq + i)),
            pl.BlockSpec((seq, w), lambda b, i: (b, 1), pipeline_mode=pl.Buffered(1)),
            pl.BlockSpec((w, seq), lambda b, i: (1, b), pipeline_mode=pl.Buffered(1)),
        ],
        out_specs=pl.BlockSpec((tq, w), lambda b, i: (b * nq + i, 0)),
        out_shape=jax.ShapeDtypeStruct((batch * seq, w), F32),
        scratch_shapes=[pltpu.VMEM((seq, tq), F32),
                        pltpu.VMEM((seq, tq), jnp.int16),
                        pltpu.VMEM((seq, tq), jnp.int16),
                        pltpu.VMEM((N_PAIRS, 1, 2 * tq), F32),
                        pltpu.VMEM((N_PAIRS, 2 * LANES, 2 * tq), BF16),
                        pltpu.VMEM((N_PAIRS, 1, 2 * tq), F32),
                        pltpu.VMEM((N_PAIRS, 2, HEAD_DIM + ONES_ROWS, tq), F32),
                        pltpu.VMEM((w, tq), F32)],
        compiler_params=_cparams(2),
        name="dsa",
    )(qt_all, k_all, wt, qt_all, k_all, vt_all)


def _merge_kernel(ya_ref, yb_ref, h_ref, kv_ref, wmq_ref, wg_ref, wmix_ref, wb_ref, wo_ref, x_ref, gn_ref,
                  *out_refs, last):
    w = BRANCH_WIDTH
    d = x_ref.shape[1]
    h = h_ref[...]
    mq = jnp.dot(h, wmq_ref[...], preferred_element_type=F32).astype(BF16)
    ym_heads = []
    for hd in range(M_HEADS):
        cols = slice(hd * M_HEAD_DIM, (hd + 1) * M_HEAD_DIM)
        km = kv_ref[:, cols]
        vm = kv_ref[:, w + hd * M_HEAD_DIM:w + (hd + 1) * M_HEAD_DIM]
        sc = lax.dot_general(mq[:, cols], km, (((1,), (1,)), ((), ())),
                             preferred_element_type=F32) * (M_HEAD_DIM ** -0.5)
        sc = sc - jnp.max(sc, axis=-1, keepdims=True)
        e = jnp.exp(sc)
        p = e / jnp.sum(e, axis=-1, keepdims=True)
        ym_heads.append(jnp.dot(p.astype(BF16), vm, preferred_element_type=F32))
    ym = jnp.concatenate(ym_heads, axis=-1)
    ys = (ya_ref[...], yb_ref[...], ym)
    merged = None
    for n in range(3):
        g = jnp.dot(h, wg_ref[:, n * w:(n + 1) * w], preferred_element_type=F32)
        y = (ys[n] * (g * jax.nn.sigmoid(g))).astype(BF16)
        up = jnp.dot(y, wb_ref[n], preferred_element_type=F32)
        mix = jax.nn.sigmoid(jnp.dot(h, wmix_ref[:, n * d:(n + 1) * d], preferred_element_type=F32))
        term = mix * up
        merged = term if merged is None else merged + term
    x_new = x_ref[...] + jnp.dot(merged.astype(BF16), wo_ref[...], preferred_element_type=F32)
    y = x_new * lax.rsqrt(jnp.mean(x_new * x_new, axis=-1, keepdims=True) + RMS_EPS) * gn_ref[...]
    if last:
        out_refs[0][...] = y
    else:
        out_refs[0][...] = x_new
        out_refs[1][...] = y.astype(BF16)


def _merge(ya, yb, h, kv, w_mq, w_g, w_mix, wb, wo, x2d, g_next, last, seq, n_mem, tm):
    m, d = x2d.shape
    w = BRANCH_WIDTH
    nt = seq // tm

    def resident(shape):
        return pl.BlockSpec(shape, lambda i: (0,) * len(shape), pipeline_mode=pl.Buffered(1))

    row_tile = pl.BlockSpec((tm, d), lambda i: (i, 0))
    out_f32 = jax.ShapeDtypeStruct((m, d), F32)
    return pl.pallas_call(
        functools.partial(_merge_kernel, last=last),
        grid=(m // tm,),
        in_specs=[
            pl.BlockSpec((tm, w), lambda i: (i, 0)),
            pl.BlockSpec((tm, w), lambda i: (i, 0)),
            pl.BlockSpec((tm, d), lambda i: (i, 0)),
            pl.BlockSpec((n_mem, 2 * w), lambda i: (i // nt, 0)),
            resident((d, w)),
            resident((d, 3 * w)),
            resident((d, 3 * d)),
            resident((3, w, d)),
            resident((d, d)),
            row_tile,
            resident((1, d)),
        ],
        out_specs=row_tile if last else (row_tile, row_tile),
        out_shape=out_f32 if last else (out_f32, jax.ShapeDtypeStruct((m, d), BF16)),
        compiler_params=_cparams(1),
        name="merge",
    )(ya, yb, h, kv, w_mq, w_g, w_mix, wb, wo, x2d, g_next.reshape(1, d))


def _rope_tables(seq):
    inv_freq = ROPE_THETA ** (-jnp.arange(ROT_HALF, dtype=F32) / ROT_HALF)
    ang = jnp.arange(seq, dtype=jnp.int32).astype(F32)[:, None] * inv_freq[None, :]
    cos, sin = jnp.cos(ang), jnp.sin(ang)
    r = np.arange(LANES) % HEAD_DIM
    f = r % ROT_HALF
    first = jnp.asarray(r < ROT_HALF)
    second = jnp.asarray((r >= ROT_HALF) & (r < 2 * ROT_HALF))
    rot = jnp.asarray(r < 2 * ROT_HALF)
    c_tok = jnp.where(rot[None, :], cos[:, f], 1.0)
    s1_tok = jnp.where(first[None, :], -sin[:, f], 0.0)
    s2_tok = jnp.where(second[None, :], sin[:, f], 0.0)
    return (c_tok, s1_tok, s2_tok), (cos.T, sin.T)


def kernel(x, mem, norm_g, w_in, mem_norm_g, w_mem_kv, w_branch, w_out, final_g):
    batch, seq, d = x.shape
    n_mem = mem.shape[1]
    depth = norm_g.shape[0]
    w = BRANCH_WIDTH
    m = batch * seq
    assert seq % MOBA_BLOCK == 0 and d % LANES == 0

    tm = min(1024, seq)
    tm_merge = min(512, seq)
    dsa_tq = 256
    dsa_kc = 256
    dsa_kcb = min(512, seq)

    tok_tabs, feat_tabs = _rope_tables(seq)
    offs = np.cumsum([0, w, w, w, w, w, w, w, w, N_HEADS * HEAD_DIM, HEAD_DIM, N_HEADS, w, w, 3 * d])
    (o_aq, o_ak, o_av, o_ag, o_bq, o_bk, o_bv, o_bg, o_iq, o_ik, o_iw, o_mq, o_mg, o_mix, o_end) = offs
    attn_scale = HEAD_DIM ** -0.5 * float(np.log2(np.e))
    idx_scale = HEAD_DIM ** -0.5

    x2d = x.reshape(m, d)
    mem2d = mem.reshape(batch * n_mem, d)
    h = _rmsnorm(x2d, norm_g[0], BF16, tm)
    for l in range(depth):
        col = lambda a, b: lax.slice(w_in, (l, 0, a), (l + 1, d, b)).reshape(d, b - a)
        w_k = jnp.concatenate([col(o_ak, o_av), col(o_bk, o_bv), col(o_ik, o_iw), col(o_ik, o_iw)],
                              axis=1).astype(BF16)
        w_qt = jnp.concatenate([col(o_aq, o_ak) * attn_scale, col(o_bq, o_bk) * attn_scale,
                                col(o_iq, o_ik) * idx_scale], axis=1).T.astype(BF16)
        w_vt = jnp.concatenate([col(o_av, o_ag), col(o_bv, o_bg)], axis=1).T.astype(BF16)
        w_iwt = jnp.pad(col(o_iw, o_mq).T, ((0, 2 * SUBLANES - N_HEADS), (0, 0))).astype(BF16)
        w_mq = col(o_mq, o_mg).astype(BF16)
        w_g = jnp.concatenate([col(o_ag, o_bq), col(o_bg, o_iq), col(o_mg, o_mix)], axis=1).astype(BF16)
        w_mix = col(o_mix, o_end).astype(BF16)

        k_all, qt_all, vt_all, iwt = _in_proj(h, w_k, w_qt, w_vt, w_iwt, tok_tabs, feat_tabs, seq, tm)

        ya = _moba(qt_all, k_all, vt_all, batch, seq, 0, 0, 0)
        yb = _dsa(qt_all, k_all, vt_all, iwt, batch, seq, dsa_tq, dsa_kc, dsa_kcb)

        mn = _rmsnorm(mem2d, mem_norm_g[l], BF16, n_mem)
        kv = _mm_tok(mn, w_mem_kv[l].astype(BF16), BF16, n_mem, 2 * w)

        last = l == depth - 1
        res = _merge(ya, yb, h, kv, w_mq, w_g, w_mix, w_branch[l].astype(BF16), w_out[l].astype(BF16),
                     x2d, final_g if last else norm_g[l + 1], last, seq, n_mem, tm_merge)
        if not last:
            x2d, h = res

    return res.reshape(batch, seq, d)
```

```python
import functools

import jax
import jax.numpy as jnp
import numpy as np
from jax import lax
from jax.experimental import pallas as pl
from jax.experimental.pallas import tpu as pltpu

F32 = jnp.float32
BF16 = jnp.bfloat16

HEAD_DIM = 64
N_HEADS = 8
BRANCH_WIDTH = 512
M_HEADS = 4
M_HEAD_DIM = 128
MOBA_BLOCK = 256
MOBA_TOPK = 3
IDX_TOPK_MAX = 256
ROPE_THETA = 500000.0
ROT_HALF = HEAD_DIM // 4 // 2
RMS_EPS = 1e-6

LANES = 128
SUBLANES = 8
VMEM_LIMIT = 56 * 1024 * 1024
NEG_BIG = -1e30
FLT_MAX = float(np.finfo(np.float32).max)
INT_MIN = -(2 ** 31)
I16_MIN = -(2 ** 15)
KEY_NEG_INF = -0x7F800000


def _cparams(n_axes):
    return pltpu.CompilerParams(dimension_semantics=("arbitrary",) * n_axes,
                                vmem_limit_bytes=VMEM_LIMIT)


def _rmsnorm_kernel(x_ref, g_ref, o_ref):
    xf = x_ref[...]
    y = xf * lax.rsqrt(jnp.mean(xf * xf, axis=-1, keepdims=True) + RMS_EPS)
    o_ref[...] = (y * g_ref[...]).astype(o_ref.dtype)


def _rmsnorm(x2d, g, out_dtype, tm):
    m, d = x2d.shape
    return pl.pallas_call(
        _rmsnorm_kernel,
        grid=(m // tm,),
        in_specs=[pl.BlockSpec((tm, d), lambda i: (i, 0)),
                  pl.BlockSpec((1, d), lambda i: (0, 0))],
        out_specs=pl.BlockSpec((tm, d), lambda i: (i, 0)),
        out_shape=jax.ShapeDtypeStruct((m, d), out_dtype),
        compiler_params=_cparams(1),
        name="rmsnorm",
    )(x2d, g.reshape(1, d))


def _mm_tok_kernel(h_ref, w_ref, o_ref):
    o_ref[...] = jnp.dot(h_ref[...], w_ref[...], preferred_element_type=F32).astype(o_ref.dtype)


def _mm_tok(h, w, out_dtype, tm, tn):
    m, k = h.shape
    n = w.shape[1]
    return pl.pallas_call(
        _mm_tok_kernel,
        grid=(m // tm, n // tn),
        in_specs=[pl.BlockSpec((tm, k), lambda i, j: (i, 0)),
                  pl.BlockSpec((k, tn), lambda i, j: (0, j))],
        out_specs=pl.BlockSpec((tm, tn), lambda i, j: (i, j)),
        out_shape=jax.ShapeDtypeStruct((m, n), out_dtype),
        compiler_params=_cparams(2),
        name="proj_tok",
    )(h, w)


def _in_proj_kernel(h_ref, wk_ref, wqt_ref, wvt_ref, wiw_ref, c_ref, s1_ref, s2_ref, cos_ref, sin_ref,
                    k_ref, qt_ref, vt_ref, iw_ref):
    h = h_ref[...]
    nt_dims = (((1,), (1,)), ((), ()))

    k_acc = jnp.dot(h, wk_ref[...], preferred_element_type=F32)
    c, s1, s2 = c_ref[...], s1_ref[...], s2_ref[...]
    for j in range(k_acc.shape[1] // LANES):
        piece = k_acc[:, j * LANES:(j + 1) * LANES]
        up = pltpu.roll(piece, LANES - ROT_HALF, 1)
        down = pltpu.roll(piece, ROT_HALF, 1)
        k_ref[:, j * LANES:(j + 1) * LANES] = (piece * c + up * s1 + down * s2).astype(k_ref.dtype)

    cos, sin = cos_ref[...], sin_ref[...]
    w = BRANCH_WIDTH
    for blk in range(wqt_ref.shape[0] // w):
        acc = lax.dot_general(wqt_ref[blk * w:(blk + 1) * w, :], h, nt_dims, preferred_element_type=F32)
        pieces = []
        for hh in range(w // HEAD_DIM):
            base = hh * HEAD_DIM
            x1 = acc[base:base + ROT_HALF]
            x2 = acc[base + ROT_HALF:base + 2 * ROT_HALF]
            pieces += [x1 * cos - x2 * sin, x2 * cos + x1 * sin, acc[base + 2 * ROT_HALF:base + HEAD_DIM]]
        qt_ref[blk * w:(blk + 1) * w, :] = jnp.concatenate(pieces, axis=0).astype(qt_ref.dtype)

    vt_ref[...] = lax.dot_general(wvt_ref[...], h, nt_dims, preferred_element_type=F32).astype(vt_ref.dtype)
    iw_ref[...] = lax.dot_general(wiw_ref[...], h, nt_dims, preferred_element_type=F32)


def _in_proj(h, w_k, w_qt, w_vt, w_iwt, tok_tabs, feat_tabs, seq, tm):
    m, k = h.shape
    nt = seq // tm

    def resident(shape):
        return pl.BlockSpec(shape, lambda i: (0,) * len(shape), pipeline_mode=pl.Buffered(1))

    tok_tab = pl.BlockSpec((tm, LANES), lambda i: (i % nt, 0))
    feat_tab = pl.BlockSpec((SUBLANES, tm), lambda i: (0, i % nt))
    n_k, n_q, n_v, n_w = w_k.shape[1], w_qt.shape[0], w_vt.shape[0], w_iwt.shape[0]
    return pl.pallas_call(
        _in_proj_kernel,
        grid=(m // tm,),
        in_specs=[pl.BlockSpec((tm, k), lambda i: (i, 0)),
                  resident(w_k.shape), resident(w_qt.shape), resident(w_vt.shape), resident(w_iwt.shape),
                  tok_tab, tok_tab, tok_tab, feat_tab, feat_tab],
        out_specs=(pl.BlockSpec((tm, n_k), lambda i: (i, 0)),
                   pl.BlockSpec((n_q, tm), lambda i: (0, i)),
                   pl.BlockSpec((n_v, tm), lambda i: (0, i)),
                   pl.BlockSpec((n_w, tm), lambda i: (0, i))),
        out_shape=(jax.ShapeDtypeStruct((m, n_k), BF16),
                   jax.ShapeDtypeStruct((n_q, m), BF16),
                   jax.ShapeDtypeStruct((n_v, m), BF16),
                   jax.ShapeDtypeStruct((n_w, m), F32)),
        compiler_params=_cparams(1),
        name="in_proj",
    )(h, w_k, w_qt, w_vt, w_iwt, *tok_tabs, *feat_tabs)


N_PAIRS = N_HEADS // 2


def _pair_operand(qt_ref, pair):
    qp = qt_ref[pair * LANES:(pair + 1) * LANES, :].astype(F32)
    row = lax.broadcasted_iota(jnp.int32, qp.shape, 0)
    lo = jnp.where(row < HEAD_DIM, qp, 0.0)
    hi = jnp.where(row >= HEAD_DIM, qp, 0.0)
    return jnp.concatenate([lo, hi], axis=1).astype(BF16)


ONES_ROWS = 16
QK_LOOKAHEAD = 4


OFFSET_SLACK = 1.0 + 2.0 ** -5
L_FLOOR = 2.0 ** -80


def _pv_accumulate(pm, vt_tile, acc_ref, p, scale=None):
    keys = vt_tile.shape[1]
    tq = pm.shape[1] // 2
    ones = jnp.ones((ONES_ROWS, keys), BF16)
    for e in range(2):
        vt_ext = jnp.concatenate([vt_tile[e * HEAD_DIM:(e + 1) * HEAD_DIM], ones], axis=0)
        cols = slice(e * tq, (e + 1) * tq)
        upd = jnp.dot(vt_ext, pm[:, cols], preferred_element_type=F32)
        old = acc_ref[p, e] if scale is None else scale[:, cols] * acc_ref[p, e]
        acc_ref[p, e] = old + upd


def _online_update(s, vt_tile, m_ref, acc_ref, p):
    m = m_ref[p]
    m_new = jnp.maximum(m, jnp.max(s, axis=0, keepdims=True))
    _pv_accumulate(jnp.exp2(s - m_new).astype(BF16), vt_tile, acc_ref, p, scale=jnp.exp2(m - m_new))
    m_ref[p] = m_new


def _offset_update(s, vt_tile, acc_ref, p):
    _pv_accumulate(jnp.exp2(s).astype(BF16), vt_tile, acc_ref, p)


def _flash_tiles(tiles, qk, mask, update):
    ss = [qk(*t) for t in tiles[:QK_LOOKAHEAD]]
    for n, (u, p) in enumerate(tiles):
        if n + QK_LOOKAHEAD < len(tiles):
            ss.append(qk(*tiles[n + QK_LOOKAHEAD]))
        update(mask(ss[n], u, p), u, p)
        ss[n] = None


def _key_norm_bound(k_ref, kb_ref, seq, tq):
    chunk = min(512, seq)
    width = k_ref.shape[1]
    lane = lax.broadcasted_iota(jnp.int32, (width, LANES), 0)
    head = lax.broadcasted_iota(jnp.int32, (width, LANES), 1)
    group = jnp.where((lane >= head * HEAD_DIM) & (lane < (head + 1) * HEAD_DIM), 1.0, 0.0).astype(BF16)

    def body(c, mx):
        kk = k_ref[pl.ds(pl.multiple_of(c * chunk, chunk), chunk), :].astype(F32)
        n2 = jnp.dot((kk * kk).astype(BF16), group, preferred_element_type=F32)
        return jnp.maximum(mx, jnp.max(n2, axis=0, keepdims=True))
    mx = lax.fori_loop(0, seq // chunk, body, jnp.zeros((1, LANES), F32))
    for p in range(N_PAIRS):
        kb_ref[p] = jnp.concatenate([jnp.broadcast_to(mx[:, 2 * p:2 * p + 1], (1, tq)),
                                     jnp.broadcast_to(mx[:, 2 * p + 1:2 * p + 2], (1, tq))], axis=1)


def _offset_rows(qop, kb2):
    qf = qop.astype(F32)
    bound = jnp.sqrt(jnp.sum(qf * qf, axis=0, keepdims=True) * kb2) * OFFSET_SLACK
    row = lax.broadcasted_iota(jnp.int32, (ONES_ROWS, qop.shape[1]), 0)
    return jnp.where(row == 0, -bound, 0.0).astype(BF16)


def _denominator_ok(acc_ref):
    return jnp.min(acc_ref[:, :, HEAD_DIM:HEAD_DIM + 1, :]) > L_FLOOR


def _flash_finish(o_ref, ot_ref, acc_ref, tq):
    for p in range(N_PAIRS):
        for e in range(2):
            lo = p * LANES + e * HEAD_DIM
            ot_ref[lo:lo + HEAD_DIM, :] = acc_ref[p, e, :HEAD_DIM] / acc_ref[p, e, HEAD_DIM:HEAD_DIM + 1]
    o_ref[...] = ot_ref[...].T


MOBA_UNROLL = 2
DSA_UNROLL = 2
LONG_UNROLLS = (8, 4)
SCORE_UNROLL = 4


def _moba_kernel(qt_ref, k_ref, vt_ref, o_ref, kmean_ref, kb_ref, qop_ref, sel_ref, m_ref, acc_ref,
                 ot_ref, *, seq, topk):
    i = pl.program_id(1)
    blk = MOBA_BLOCK
    nb = seq // blk
    tq = blk

    @pl.when(i == 0)
    def _():
        r = lax.broadcasted_iota(jnp.int32, (nb, seq), 0)
        c = lax.broadcasted_iota(jnp.int32, (nb, seq), 1)
        member = jnp.where((c >= r * blk) & (c < (r + 1) * blk), 1.0, 0.0).astype(BF16)
        ksum = jnp.dot(member, k_ref[...], preferred_element_type=F32)
        kmean_ref[...] = (ksum * (1.0 / blk)).astype(BF16)
        _key_norm_bound(k_ref, kb_ref, seq, tq)

    blk_id = lax.broadcasted_iota(jnp.int32, (nb, 2 * tq), 0)
    past = blk_id < i
    own_start = pl.multiple_of(i * blk, blk)

    for p in range(N_PAIRS):
        lanes = slice(p * LANES, (p + 1) * LANES)
        qop = _pair_operand(qt_ref, p)
        qop_ref[p, :LANES] = qop
        qop_ref[p, LANES:LANES + ONES_ROWS] = _offset_rows(qop, kb_ref[p])
        gate = jnp.dot(kmean_ref[:, lanes], qop, preferred_element_type=F32)
        gate = jnp.where(past, gate, -jnp.inf)
        sel = jnp.zeros((nb, 2 * tq), jnp.bool_)
        for _ in range(topk):
            mx = jnp.max(gate, axis=0, keepdims=True)
            first = jnp.min(jnp.where(gate == mx, blk_id, nb), axis=0, keepdims=True)
            hit = blk_id == first
            sel = sel | hit
            gate = jnp.where(hit, -jnp.inf, gate)
        sel_ref[p] = jnp.where(sel & past, 1.0, 0.0)
        qop_ref[p, LANES + ONES_ROWS:LANES + ONES_ROWS + nb] = jnp.where(sel & past, 0.0, NEG_BIG).astype(BF16)
        qop_ref[p, LANES + ONES_ROWS + nb:] = jnp.zeros((LANES - ONES_ROWS - nb, 2 * tq), BF16)

    def lanes(p):
        return slice(p * LANES, (p + 1) * LANES)

    krow = lax.broadcasted_iota(jnp.int32, (blk, 2 * tq), 0)
    qcol = lax.broadcasted_iota(jnp.int32, (blk, 2 * tq), 1)
    causal = krow <= jnp.where(qcol < tq, qcol, qcol - tq)
    lane = lax.broadcasted_iota(jnp.int32, (blk, LANES), 1)

    def attend(fast):
        acc_ref[...] = jnp.zeros(acc_ref.shape, F32)
        if not fast:
            m_ref[...] = jnp.full(m_ref.shape, NEG_BIG, F32)

        def key_operand(start, extra_lane):
            k_tile = k_ref[pl.ds(start, blk), :]
            if not fast:
                return lambda p: k_tile[:, lanes(p)]
            sel_lanes = (lane == 0) if extra_lane is None else ((lane == 0) | (lane == extra_lane))
            extra = jnp.where(sel_lanes, 1.0, 0.0).astype(BF16)
            return lambda p: jnp.concatenate([k_tile[:, lanes(p)], extra], axis=1)

        def query_operand(p):
            return qop_ref[p] if fast else qop_ref[p, :LANES]

        def update(vt_tile):
            if fast:
                return lambda s, u, p: _offset_update(s, vt_tile(u, p), acc_ref, p)
            return lambda s, u, p: _online_update(s, vt_tile(u, p), m_ref, acc_ref, p)

        own_keys = key_operand(own_start, None)
        _flash_tiles(
            [(0, p) for p in range(N_PAIRS)],
            lambda u, p: jnp.dot(own_keys(p), query_operand(p), preferred_element_type=F32),
            lambda s, u, p: jnp.where(causal, s, -jnp.inf),
            update(lambda u, p: vt_ref[lanes(p), pl.ds(own_start, blk)]))

        def body(c, carry, unroll, first):
            js = [first + c * unroll + u for u in range(unroll)]
            starts = [pl.multiple_of(j * blk, blk) for j in js]
            keys = [key_operand(starts[u], ONES_ROWS + js[u]) for u in range(unroll)]
            _flash_tiles(
                [(u, p) for u in range(unroll) for p in range(N_PAIRS)],
                lambda u, p: jnp.dot(keys[u](p), query_operand(p), preferred_element_type=F32),
                (lambda s, u, p: s) if fast else
                (lambda s, u, p: jnp.where(sel_ref[p, pl.ds(js[u], 1), :] > 0.5, s, -jnp.inf)),
                update(lambda u, p: vt_ref[lanes(p), pl.ds(starts[u], blk)]))
            return carry

        done = 0
        for unroll in (LONG_UNROLLS if fast else ()):
            n_long = (i - done) // unroll
            lax.fori_loop(0, n_long, functools.partial(body, unroll=unroll, first=done), 0)
            done = done + n_long * unroll
        lax.fori_loop(0, (i - done + MOBA_UNROLL - 1) // MOBA_UNROLL,
                      functools.partial(body, unroll=MOBA_UNROLL, first=done), 0)

    attend(True)

    @pl.when(jnp.logical_not(_denominator_ok(acc_ref)))
    def _():
        attend(False)

    _flash_finish(o_ref, ot_ref, acc_ref, tq)


def _moba(qt_all, k_all, vt_all, batch, seq, q_row_blk, k_col_blk, v_row_blk):
    tq = MOBA_BLOCK
    nq = seq // tq
    nb = seq // MOBA_BLOCK
    topk = min(MOBA_TOPK, nb - 1)
    w = BRANCH_WIDTH
    return pl.pallas_call(
        functools.partial(_moba_kernel, seq=seq, topk=topk),
        grid=(batch, nq),
        in_specs=[
            pl.BlockSpec((w, tq), lambda b, i: (q_row_blk, b * nq + i)),
            pl.BlockSpec((seq, w), lambda b, i: (b, k_col_blk), pipeline_mode=pl.Buffered(1)),
            pl.BlockSpec((w, seq), lambda b, i: (v_row_blk, b), pipeline_mode=pl.Buffered(1)),
        ],
        out_specs=pl.BlockSpec((tq, w), lambda b, i: (b * nq + i, 0)),
        out_shape=jax.ShapeDtypeStruct((batch * seq, w), F32),
        scratch_shapes=[pltpu.VMEM((nb, w), BF16),
                        pltpu.VMEM((N_PAIRS, 1, 2 * tq), F32),
                        pltpu.VMEM((N_PAIRS, 2 * LANES, 2 * tq), BF16),
                        pltpu.VMEM((N_PAIRS, nb, 2 * tq), F32),
                        pltpu.VMEM((N_PAIRS, 1, 2 * tq), F32),
                        pltpu.VMEM((N_PAIRS, 2, HEAD_DIM + ONES_ROWS, tq), F32),
                        pltpu.VMEM((w, tq), F32)],
        compiler_params=_cparams(2),
        name="moba",
    )(qt_all, k_all, vt_all)


def _ordered_key(bits):
    return jnp.where(bits < 0, jnp.int32(INT_MIN) - bits, bits)


def _tree_sum(parts):
    while len(parts) > 1:
        parts = [parts[a] + parts[a + 1] for a in range(0, len(parts) - 1, 2)] + (
            [parts[-1]] if len(parts) % 2 else [])
    return parts[0]


def _dsa_kernel(qit_ref, ki_ref, wt_ref, qt_ref, k_ref, vt_ref, o_ref,
                score_ref, hi_ref, lo_ref, kb_ref, qop_ref, m_ref, acc_ref, ot_ref,
                *, tq, kc, kcb, n_sel):
    i = pl.program_id(1)
    nch = (i * tq + tq) // kc
    nchb = (i * tq + tq + kcb - 1) // kcb
    krow = lax.broadcasted_iota(jnp.int32, (kc, tq), 0)
    qpos = i * tq + lax.broadcasted_iota(jnp.int32, (kc, tq), 1)
    n_sel_f = float(n_sel)

    @pl.when(i == 0)
    def _():
        _key_norm_bound(k_ref, kb_ref, k_ref.shape[0], tq)

    for p in range(N_PAIRS):
        qop_ref[p, :LANES] = _pair_operand(qit_ref, p)
    w_rows = [wt_ref[h:h + 1, :] * (N_HEADS ** -0.5) for h in range(N_HEADS)]

    def score_body(c, carry, masked, unroll):
        starts = [pl.multiple_of((c * unroll + u) * kc, kc) for u in range(unroll)]
        rs = [[jnp.dot(ki_ref[pl.ds(st, kc), :], qop_ref[p, :LANES], preferred_element_type=F32)
               for p in range(N_PAIRS)] for st in starts]
        for u, start in enumerate(starts):
            acc = jnp.zeros((kc, tq), F32)
            for p in range(N_PAIRS):
                acc = acc + jnp.maximum(rs[u][p][:, :tq], 0.0) * w_rows[2 * p]
                acc = acc + jnp.maximum(rs[u][p][:, tq:], 0.0) * w_rows[2 * p + 1]
            sc = jnp.where(start + krow <= qpos, acc, -jnp.inf) if masked else acc
            score_ref[pl.ds(start, kc), :] = sc
            key = _ordered_key(lax.bitcast_convert_type(sc, jnp.int32))
            hi_ref[pl.ds(start, kc), :] = jnp.right_shift(key, 16).astype(jnp.int16)
            lo_ref[pl.ds(start, kc), :] = (key ^ jnp.int32(0x8000)).astype(jnp.int16)
        return carry

    n_full = (i * tq + 1) // kc
    n_fast = n_full // SCORE_UNROLL
    lax.fori_loop(0, n_fast, functools.partial(score_body, masked=False, unroll=SCORE_UNROLL), 0)
    lax.fori_loop(n_fast * SCORE_UNROLL, n_full, functools.partial(score_body, masked=False, unroll=1), 0)
    lax.fori_loop(n_full, nch, functools.partial(score_body, masked=True, unroll=1), 0)

    @pl.when(nch * kc < nchb * kcb)
    def _():
        pad_rows = pl.ds(pl.multiple_of(nch * kc, kc), kcb - kc)
        score_ref[pad_rows, :] = jnp.full((kcb - kc, tq), -jnp.inf, F32)
        hi_ref[pad_rows, :] = jnp.full((kcb - kc, tq), I16_MIN, jnp.int16)
        lo_ref[pad_rows, :] = jnp.full((kcb - kc, tq), I16_MIN, jnp.int16)

    def count16(ref, pred):
        n_acc = 4
        rows = 2 * SUBLANES

        def body(c, accs):
            x = ref[pl.ds(pl.multiple_of(c * kcb, kcb), kcb), :]
            accs = list(accs)
            for r in range(kcb // rows):
                hit = jnp.where(pred(x[r * rows:(r + 1) * rows]), jnp.int16(1), jnp.int16(0))
                accs[r % n_acc] = accs[r % n_acc] + hit
            return tuple(accs)
        accs = lax.fori_loop(0, nchb, body, tuple(jnp.zeros((rows, tq), jnp.int16) for _ in range(n_acc)))
        tot = _tree_sum(list(accs)).astype(jnp.int32).astype(F32)
        return jnp.sum(tot, axis=0, keepdims=True)

    def bisect16(ref, target, count_at_min):
        def body(it, carry):
            t, c_ge, c_gt = carry
            cand = t + jnp.left_shift(jnp.int32(1), 15 - it)
            cand16 = cand.astype(jnp.int16)
            cnt = count16(ref, lambda x: x >= cand16)
            ok = cnt >= target
            return jnp.where(ok, cand, t), jnp.where(ok, cnt, c_ge), jnp.where(ok, c_gt, cnt)
        start = jnp.full((1, tq), I16_MIN, jnp.int32)
        return lax.fori_loop(0, 16, body, (start, count_at_min, jnp.zeros((1, tq), F32)))

    rows_seen = jnp.full((1, tq), (nchb * kcb).astype(F32))
    t_hi, c_ge_hi, c_gt_hi = bisect16(hi_ref, n_sel_f, rows_seen)
    t_hi16 = t_hi.astype(jnp.int16)

    def low_body(c, carry):
        rows = pl.ds(pl.multiple_of(c * kcb, kcb), kcb)
        lo_ref[rows, :] = jnp.where(hi_ref[rows, :] == t_hi16, lo_ref[rows, :], jnp.int16(I16_MIN))
        return carry

    lax.fori_loop(0, nchb, low_body, 0)
    t_lo, c_ge_lo, _ = bisect16(lo_ref, n_sel_f - c_gt_hi, c_ge_hi - c_gt_hi)
    n_ge = c_gt_hi + c_ge_lo
    key_t = t_hi * 65536 + (t_lo + 32768)
    short = key_t <= KEY_NEG_INF
    thr = jnp.where(short, -FLT_MAX, lax.bitcast_convert_type(_ordered_key(key_t), F32))
    has_tie = jnp.max(jnp.where(short, 0.0, n_ge - n_sel_f)) > 0.0

    @pl.when(has_tie)
    def _():
        lower = (lax.broadcasted_iota(jnp.int32, (kc, kc), 1)
                 < lax.broadcasted_iota(jnp.int32, (kc, kc), 0))
        lower = jnp.where(lower, 1.0, 0.0).astype(BF16)

        def gt_body(c, cnt):
            sc = score_ref[pl.ds(pl.multiple_of(c * kc, kc), kc), :]
            return cnt + jnp.sum(jnp.where(sc > thr, 1.0, 0.0), axis=0, keepdims=True)
        n_gt = lax.fori_loop(0, nch, gt_body, jnp.zeros((1, tq), F32))
        need = n_sel_f - n_gt

        def body(c, seen):
            start = pl.multiple_of(c * kc, kc)
            sc = score_ref[pl.ds(start, kc), :]
            eq = jnp.where(sc == thr, 1.0, 0.0)
            rank = seen + jnp.dot(lower, eq.astype(BF16), preferred_element_type=F32)
            keep = (sc > thr) | ((sc == thr) & (rank < need))
            keep = keep & (start + krow <= qpos)
            score_ref[pl.ds(start, kc), :] = jnp.where(keep, jnp.inf, -jnp.inf)
            return seen + jnp.sum(eq, axis=0, keepdims=True)
        lax.fori_loop(0, nch, body, jnp.zeros((1, tq), F32))

    for p in range(N_PAIRS):
        qop = _pair_operand(qt_ref, p)
        qop_ref[p, :LANES] = qop
        qop_ref[p, LANES:LANES + ONES_ROWS] = _offset_rows(qop, kb_ref[p])
        qop_ref[p, LANES + ONES_ROWS:] = jnp.zeros((LANES - ONES_ROWS, 2 * tq), BF16)

    def lanes(p):
        return slice(p * LANES, (p + 1) * LANES)

    def attend(fast):
        acc_ref[...] = jnp.zeros(acc_ref.shape, F32)
        if not fast:
            m_ref[...] = jnp.full(m_ref.shape, NEG_BIG, F32)
        ones = jnp.ones((kc, LANES), BF16)

        def att_body(c, carry, unroll, first):
            starts = [pl.multiple_of((first + c * unroll + u) * kc, kc) for u in range(unroll)]

            def qk(u, p):
                k_tile = k_ref[pl.ds(starts[u], kc), lanes(p)]
                if fast:
                    return jnp.dot(jnp.concatenate([k_tile, ones], axis=1), qop_ref[p],
                                   preferred_element_type=F32)
                return jnp.dot(k_tile, qop_ref[p, :LANES], preferred_element_type=F32)

            def add_bias(s, u, p):
                bias = jnp.where(score_ref[pl.ds(starts[u], kc), :] >= thr, 0.0, NEG_BIG)
                return jnp.concatenate([s[:, :tq] + bias, s[:, tq:] + bias], axis=1)

            def update(s, u, p):
                vt_tile = vt_ref[lanes(p), pl.ds(starts[u], kc)]
                if fast:
                    _offset_update(s, vt_tile, acc_ref, p)
                else:
                    _online_update(s, vt_tile, m_ref, acc_ref, p)

            _flash_tiles([(u, p) for u in range(unroll) for p in range(N_PAIRS)], qk, add_bias, update)
            return carry

        done = 0
        for unroll in (LONG_UNROLLS if fast else ()):
            n_long = (nch - done) // unroll
            lax.fori_loop(0, n_long, functools.partial(att_body, unroll=unroll, first=done), 0)
            done = done + n_long * unroll
        lax.fori_loop(0, (nch - done + DSA_UNROLL - 1) // DSA_UNROLL,
                      functools.partial(att_body, unroll=DSA_UNROLL, first=done), 0)

    attend(True)

    @pl.when(jnp.logical_not(_denominator_ok(acc_ref)))
    def _():
        attend(False)

    _flash_finish(o_ref, ot_ref, acc_ref, tq)


def _dsa(qt_all, k_all, vt_all, wt, batch, seq, tq, kc, kcb):
    nq = seq // tq
    w = BRANCH_WIDTH
    n_sel = min(IDX_TOPK_MAX, seq // 4)
    assert tq % kc == 0 and kcb == DSA_UNROLL * kc and seq % kcb == 0
    ki_col_blk = (2 * w) // LANES
    return pl.pallas_call(
        functools.partial(_dsa_kernel, tq=tq, kc=kc, kcb=kcb, n_sel=n_sel),
        grid=(batch, nq),
        in_specs=[
            pl.BlockSpec((w, tq), lambda b, i: (2, b * nq + i)),
            pl.BlockSpec((seq, LANES), lambda b, i: (b, ki_col_blk), pipeline_mode=pl.Buffered(1)),
            pl.BlockSpec((2 * SUBLANES, tq), lambda b, i: (0, b * nq + i)),
            pl.BlockSpec((w, tq), lambda b, i: (1, b * nq + i)),
            pl.BlockSpec((seq, w), lambda b, i: (b, 1), pipeline_mode=pl.Buffered(1)),
            pl.BlockSpec((w, seq), lambda b, i: (1, b), pipeline_mode=pl.Buffered(1)),
        ],
        out_specs=pl.BlockSpec((tq, w), lambda b, i: (b * nq + i, 0)),
        out_shape=jax.ShapeDtypeStruct((batch * seq, w), F32),
        scratch_shapes=[pltpu.VMEM((seq, tq), F32),
                        pltpu.VMEM((seq, tq), jnp.int16),
                        pltpu.VMEM((seq, tq), jnp.int16),
                        pltpu.VMEM((N_PAIRS, 1, 2 * tq), F32),
                        pltpu.VMEM((N_PAIRS, 2 * LANES, 2 * tq), BF16),
                        pltpu.VMEM((N_PAIRS, 1, 2 * tq), F32),
                        pltpu.VMEM((N_PAIRS, 2, HEAD_DIM + ONES_ROWS, tq), F32),
                        pltpu.VMEM((w, tq), F32)],
        compiler_params=_cparams(2),
        name="dsa",
    )(qt_all, k_all, wt, qt_all, k_all, vt_all)


def _merge_kernel(ya_ref, yb_ref, h_ref, kv_ref, wmq_ref, wg_ref, wmix_ref, wb_ref, wo_ref, x_ref, gn_ref,
                  *out_refs, last):
    w = BRANCH_WIDTH
    d = x_ref.shape[1]
    h = h_ref[...]
    mq = jnp.dot(h, wmq_ref[...], preferred_element_type=F32).astype(BF16)
    ym_heads = []
    for hd in range(M_HEADS):
        cols = slice(hd * M_HEAD_DIM, (hd + 1) * M_HEAD_DIM)
        km = kv_ref[:, cols]
        vm = kv_ref[:, w + hd * M_HEAD_DIM:w + (hd + 1) * M_HEAD_DIM]
        sc = lax.dot_general(mq[:, cols], km, (((1,), (1,)), ((), ())),
                             preferred_element_type=F32) * (M_HEAD_DIM ** -0.5)
        sc = sc - jnp.max(sc, axis=-1, keepdims=True)
        e = jnp.exp(sc)
        p = e / jnp.sum(e, axis=-1, keepdims=True)
        ym_heads.append(jnp.dot(p.astype(BF16), vm, preferred_element_type=F32))
    ym = jnp.concatenate(ym_heads, axis=-1)
    ys = (ya_ref[...], yb_ref[...], ym)
    merged = None
    for n in range(3):
        g = jnp.dot(h, wg_ref[:, n * w:(n + 1) * w], preferred_element_type=F32)
        y = (ys[n] * (g * jax.nn.sigmoid(g))).astype(BF16)
        up = jnp.dot(y, wb_ref[n], preferred_element_type=F32)
        mix = jax.nn.sigmoid(jnp.dot(h, wmix_ref[:, n * d:(n + 1) * d], preferred_element_type=F32))
        term = mix * up
        merged = term if merged is None else merged + term
    x_new = x_ref[...] + jnp.dot(merged.astype(BF16), wo_ref[...], preferred_element_type=F32)
    y = x_new * lax.rsqrt(jnp.mean(x_new * x_new, axis=-1, keepdims=True) + RMS_EPS) * gn_ref[...]
    if last:
        out_refs[0][...] = y
    else:
        out_refs[0][...] = x_new
        out_refs[1][...] = y.astype(BF16)


def _merge(ya, yb, h, kv, w_mq, w_g, w_mix, wb, wo, x2d, g_next, last, seq, n_mem, tm):
    m, d = x2d.shape
    w = BRANCH_WIDTH
    nt = seq // tm

    def resident(shape):
        return pl.BlockSpec(shape, lambda i: (0,) * len(shape), pipeline_mode=pl.Buffered(1))

    row_tile = pl.BlockSpec((tm, d), lambda i: (i, 0))
    out_f32 = jax.ShapeDtypeStruct((m, d), F32)
    return pl.pallas_call(
        functools.partial(_merge_kernel, last=last),
        grid=(m // tm,),
        in_specs=[
            pl.BlockSpec((tm, w), lambda i: (i, 0)),
            pl.BlockSpec((tm, w), lambda i: (i, 0)),
            pl.BlockSpec((tm, d), lambda i: (i, 0)),
            pl.BlockSpec((n_mem, 2 * w), lambda i: (i // nt, 0)),
            resident((d, w)),
            resident((d, 3 * w)),
            resident((d, 3 * d)),
            resident((3, w, d)),
            resident((d, d)),
            row_tile,
            resident((1, d)),
        ],
        out_specs=row_tile if last else (row_tile, row_tile),
        out_shape=out_f32 if last else (out_f32, jax.ShapeDtypeStruct((m, d), BF16)),
        compiler_params=_cparams(1),
        name="merge",
    )(ya, yb, h, kv, w_mq, w_g, w_mix, wb, wo, x2d, g_next.reshape(1, d))


def _rope_tables(seq):
    inv_freq = ROPE_THETA ** (-jnp.arange(ROT_HALF, dtype=F32) / ROT_HALF)
    ang = jnp.arange(seq, dtype=jnp.int32).astype(F32)[:, None] * inv_freq[None, :]
    cos, sin = jnp.cos(ang), jnp.sin(ang)
    r = np.arange(LANES) % HEAD_DIM
    f = r % ROT_HALF
    first = jnp.asarray(r < ROT_HALF)
    second = jnp.asarray((r >= ROT_HALF) & (r < 2 * ROT_HALF))
    rot = jnp.asarray(r < 2 * ROT_HALF)
    c_tok = jnp.where(rot[None, :], cos[:, f], 1.0)
    s1_tok = jnp.where(first[None, :], -sin[:, f], 0.0)
    s2_tok = jnp.where(second[None, :], sin[:, f], 0.0)
    return (c_tok, s1_tok, s2_tok), (cos.T, sin.T)


def kernel(x, mem, norm_g, w_in, mem_norm_g, w_mem_kv, w_branch, w_out, final_g):
    batch, seq, d = x.shape
    n_mem = mem.shape[1]
    depth = norm_g.shape[0]
    w = BRANCH_WIDTH
    m = batch * seq
    assert seq % MOBA_BLOCK == 0 and d % LANES == 0

    tm = min(1024, seq)
    tm_merge = min(512, seq)
    dsa_tq = 256
    dsa_kc = 256
    dsa_kcb = min(512, seq)

    tok_tabs, feat_tabs = _rope_tables(seq)
    offs = np.cumsum([0, w, w, w, w, w, w, w, w, N_HEADS * HEAD_DIM, HEAD_DIM, N_HEADS, w, w, 3 * d])
    (o_aq, o_ak, o_av, o_ag, o_bq, o_bk, o_bv, o_bg, o_iq, o_ik, o_iw, o_mq, o_mg, o_mix, o_end) = offs
    attn_scale = HEAD_DIM ** -0.5 * float(np.log2(np.e))
    idx_scale = HEAD_DIM ** -0.5

    x2d = x.reshape(m, d)
    mem2d = mem.reshape(batch * n_mem, d)
    h = _rmsnorm(x2d, norm_g[0], BF16, tm)
    for l in range(depth):
        col = lambda a, b: lax.slice(w_in, (l, 0, a), (l + 1, d, b)).reshape(d, b - a)
        w_k = jnp.concatenate([col(o_ak, o_av), col(o_bk, o_bv), col(o_ik, o_iw), col(o_ik, o_iw)],
                              axis=1).astype(BF16)
        w_qt = jnp.concatenate([col(o_aq, o_ak) * attn_scale, col(o_bq, o_bk) * attn_scale,
                                col(o_iq, o_ik) * idx_scale], axis=1).T.astype(BF16)
        w_vt = jnp.concatenate([col(o_av, o_ag), col(o_bv, o_bg)], axis=1).T.astype(BF16)
        w_iwt = jnp.pad(col(o_iw, o_mq).T, ((0, 2 * SUBLANES - N_HEADS), (0, 0))).astype(BF16)
        w_mq = col(o_mq, o_mg).astype(BF16)
        w_g = jnp.concatenate([col(o_ag, o_bq), col(o_bg, o_iq), col(o_mg, o_mix)], axis=1).astype(BF16)
        w_mix = col(o_mix, o_end).astype(BF16)

        k_all, qt_all, vt_all, iwt = _in_proj(h, w_k, w_qt, w_vt, w_iwt, tok_tabs, feat_tabs, seq, tm)

        ya = _moba(qt_all, k_all, vt_all, batch, seq, 0, 0, 0)
        yb = _dsa(qt_all, k_all, vt_all, iwt, batch, seq, dsa_tq, dsa_kc, dsa_kcb)

        mn = _rmsnorm(mem2d, mem_norm_g[l], BF16, n_mem)
        kv = _mm_tok(mn, w_mem_kv[l].astype(BF16), BF16, n_mem, 2 * w)

        last = l == depth - 1
        res = _merge(ya, yb, h, kv, w_mq, w_g, w_mix, w_branch[l].astype(BF16), w_out[l].astype(BF16),
                     x2d, final_g if last else norm_g[l + 1], last, seq, n_mem, tm_merge)
        if not last:
            x2d, h = res

    return res.reshape(batch, seq, d)
```

```python
import functools

import jax
import jax.numpy as jnp
import numpy as np
from jax import lax
from jax.experimental import pallas as pl
from jax.experimental.pallas import tpu as pltpu

F32 = jnp.float32
BF16 = jnp.bfloat16

HEAD_DIM = 64
N_HEADS = 8
BRANCH_WIDTH = 512
M_HEADS = 4
M_HEAD_DIM = 128
MOBA_BLOCK = 256
MOBA_TOPK = 3
IDX_TOPK_MAX = 256
ROPE_THETA = 500000.0
ROT_HALF = HEAD_DIM // 4 // 2
RMS_EPS = 1e-6

LANES = 128
SUBLANES = 8
VMEM_LIMIT = 56 * 1024 * 1024
NEG_BIG = -1e30
FLT_MAX = float(np.finfo(np.float32).max)
INT_MIN = -(2 ** 31)
I16_MIN = -(2 ** 15)
KEY_NEG_INF = -0x7F800000


def _cparams(n_axes):
    return pltpu.CompilerParams(dimension_semantics=("arbitrary",) * n_axes,
                                vmem_limit_bytes=VMEM_LIMIT)


def _rmsnorm_kernel(x_ref, g_ref, o_ref):
    xf = x_ref[...]
    y = xf * lax.rsqrt(jnp.mean(xf * xf, axis=-1, keepdims=True) + RMS_EPS)
    o_ref[...] = (y * g_ref[...]).astype(o_ref.dtype)


def _rmsnorm(x2d, g, out_dtype, tm):
    m, d = x2d.shape
    return pl.pallas_call(
        _rmsnorm_kernel,
        grid=(m // tm,),
        in_specs=[pl.BlockSpec((tm, d), lambda i: (i, 0)),
                  pl.BlockSpec((1, d), lambda i: (0, 0))],
        out_specs=pl.BlockSpec((tm, d), lambda i: (i, 0)),
        out_shape=jax.ShapeDtypeStruct((m, d), out_dtype),
        compiler_params=_cparams(1),
        name="rmsnorm",
    )(x2d, g.reshape(1, d))


def _mm_tok_kernel(h_ref, w_ref, o_ref):
    o_ref[...] = jnp.dot(h_ref[...], w_ref[...], preferred_element_type=F32).astype(o_ref.dtype)


def _mm_tok(h, w, out_dtype, tm, tn):
    m, k = h.shape
    n = w.shape[1]
    return pl.pallas_call(
        _mm_tok_kernel,
        grid=(m // tm, n // tn),
        in_specs=[pl.BlockSpec((tm, k), lambda i, j: (i, 0)),
                  pl.BlockSpec((k, tn), lambda i, j: (0, j))],
        out_specs=pl.BlockSpec((tm, tn), lambda i, j: (i, j)),
        out_shape=jax.ShapeDtypeStruct((m, n), out_dtype),
        compiler_params=_cparams(2),
        name="proj_tok",
    )(h, w)


def _in_proj_kernel(h_ref, wk_ref, wqt_ref, wvt_ref, wiw_ref, c_ref, s1_ref, s2_ref, cos_ref, sin_ref,
                    k_ref, qt_ref, vt_ref, iw_ref):
    h = h_ref[...]
    nt_dims = (((1,), (1,)), ((), ()))

    k_acc = jnp.dot(h, wk_ref[...], preferred_element_type=F32)
    c, s1, s2 = c_ref[...], s1_ref[...], s2_ref[...]
    for j in range(k_acc.shape[1] // LANES):
        piece = k_acc[:, j * LANES:(j + 1) * LANES]
        up = pltpu.roll(piece, LANES - ROT_HALF, 1)
        down = pltpu.roll(piece, ROT_HALF, 1)
        k_ref[:, j * LANES:(j + 1) * LANES] = (piece * c + up * s1 + down * s2).astype(k_ref.dtype)

    cos, sin = cos_ref[...], sin_ref[...]
    w = BRANCH_WIDTH
    for blk in range(wqt_ref.shape[0] // w):
        acc = lax.dot_general(wqt_ref[blk * w:(blk + 1) * w, :], h, nt_dims, preferred_element_type=F32)
        pieces = []
        for hh in range(w // HEAD_DIM):
            base = hh * HEAD_DIM
            x1 = acc[base:base + ROT_HALF]
            x2 = acc[base + ROT_HALF:base + 2 * ROT_HALF]
            pieces += [x1 * cos - x2 * sin, x2 * cos + x1 * sin, acc[base + 2 * ROT_HALF:base + HEAD_DIM]]
        qt_ref[blk * w:(blk + 1) * w, :] = jnp.concatenate(pieces, axis=0).astype(qt_ref.dtype)

    vt_ref[...] = lax.dot_general(wvt_ref[...], h, nt_dims, preferred_element_type=F32).astype(vt_ref.dtype)
    iw_ref[...] = lax.dot_general(wiw_ref[...], h, nt_dims, preferred_element_type=F32)


def _in_proj(h, w_k, w_qt, w_vt, w_iwt, tok_tabs, feat_tabs, seq, tm):
    m, k = h.shape
    nt = seq // tm

    def resident(shape):
        return pl.BlockSpec(shape, lambda i: (0,) * len(shape), pipeline_mode=pl.Buffered(1))

    tok_tab = pl.BlockSpec((tm, LANES), lambda i: (i % nt, 0))
    feat_tab = pl.BlockSpec((SUBLANES, tm), lambda i: (0, i % nt))
    n_k, n_q, n_v, n_w = w_k.shape[1], w_qt.shape[0], w_vt.shape[0], w_iwt.shape[0]
    return pl.pallas_call(
        _in_proj_kernel,
        grid=(m // tm,),
        in_specs=[pl.BlockSpec((tm, k), lambda i: (i, 0)),
                  resident(w_k.shape), resident(w_qt.shape), resident(w_vt.shape), resident(w_iwt.shape),
                  tok_tab, tok_tab, tok_tab, feat_tab, feat_tab],
        out_specs=(pl.BlockSpec((tm, n_k), lambda i: (i, 0)),
                   pl.BlockSpec((n_q, tm), lambda i: (0, i)),
                   pl.BlockSpec((n_v, tm), lambda i: (0, i)),
                   pl.BlockSpec((n_w, tm), lambda i: (0, i))),
        out_shape=(jax.ShapeDtypeStruct((m, n_k), BF16),
                   jax.ShapeDtypeStruct((n_q, m), BF16),
                   jax.ShapeDtypeStruct((n_v, m), BF16),
                   jax.ShapeDtypeStruct((n_w, m), F32)),
        compiler_params=_cparams(1),
        name="in_proj",
    )(h, w_k, w_qt, w_vt, w_iwt, *tok_tabs, *feat_tabs)


N_PAIRS = N_HEADS // 2


def _pair_operand(qt_ref, pair):
    qp = qt_ref[pair * LANES:(pair + 1) * LANES, :].astype(F32)
    row = lax.broadcasted_iota(jnp.int32, qp.shape, 0)
    lo = jnp.where(row < HEAD_DIM, qp, 0.0)
    hi = jnp.where(row >= HEAD_DIM, qp, 0.0)
    return jnp.concatenate([lo, hi], axis=1).astype(BF16)


ONES_ROWS = 16
QK_LOOKAHEAD = 4


OFFSET_SLACK = 1.0 + 2.0 ** -5
L_FLOOR = 2.0 ** -80


def _pv_accumulate(pm, vt_tile, acc_ref, p, scale=None):
    keys = vt_tile.shape[1]
    tq = pm.shape[1] // 2
    ones = jnp.ones((ONES_ROWS, keys), BF16)
    for e in range(2):
        vt_ext = jnp.concatenate([vt_tile[e * HEAD_DIM:(e + 1) * HEAD_DIM], ones], axis=0)
        cols = slice(e * tq, (e + 1) * tq)
        upd = jnp.dot(vt_ext, pm[:, cols], preferred_element_type=F32)
        old = acc_ref[p, e] if scale is None else scale[:, cols] * acc_ref[p, e]
        acc_ref[p, e] = old + upd


def _online_update(s, vt_tile, m_ref, acc_ref, p):
    m = m_ref[p]
    m_new = jnp.maximum(m, jnp.max(s, axis=0, keepdims=True))
    _pv_accumulate(jnp.exp2(s - m_new).astype(BF16), vt_tile, acc_ref, p, scale=jnp.exp2(m - m_new))
    m_ref[p] = m_new


def _offset_update(s, vt_tile, acc_ref, p):
    _pv_accumulate(jnp.exp2(s).astype(BF16), vt_tile, acc_ref, p)


def _flash_tiles(tiles, qk, mask, update):
    ss = [qk(*t) for t in tiles[:QK_LOOKAHEAD]]
    for n, (u, p) in enumerate(tiles):
        if n + QK_LOOKAHEAD < len(tiles):
            ss.append(qk(*tiles[n + QK_LOOKAHEAD]))
        update(mask(ss[n], u, p), u, p)
        ss[n] = None


def _key_norm_bound(k_ref, kb_ref, seq, tq):
    chunk = min(512, seq)
    width = k_ref.shape[1]
    lane = lax.broadcasted_iota(jnp.int32, (width, LANES), 0)
    head = lax.broadcasted_iota(jnp.int32, (width, LANES), 1)
    group = jnp.where((lane >= head * HEAD_DIM) & (lane < (head + 1) * HEAD_DIM), 1.0, 0.0).astype(BF16)

    def body(c, mx):
        kk = k_ref[pl.ds(pl.multiple_of(c * chunk, chunk), chunk), :].astype(F32)
        n2 = jnp.dot((kk * kk).astype(BF16), group, preferred_element_type=F32)
        return jnp.maximum(mx, jnp.max(n2, axis=0, keepdims=True))
    mx = lax.fori_loop(0, seq // chunk, body, jnp.zeros((1, LANES), F32))
    for p in range(N_PAIRS):
        kb_ref[p] = jnp.concatenate([jnp.broadcast_to(mx[:, 2 * p:2 * p + 1], (1, tq)),
                                     jnp.broadcast_to(mx[:, 2 * p + 1:2 * p + 2], (1, tq))], axis=1)


def _offset_rows(qop, kb2):
    qf = qop.astype(F32)
    bound = jnp.sqrt(jnp.sum(qf * qf, axis=0, keepdims=True) * kb2) * OFFSET_SLACK
    row = lax.broadcasted_iota(jnp.int32, (ONES_ROWS, qop.shape[1]), 0)
    return jnp.where(row == 0, -bound, 0.0).astype(BF16)


def _denominator_ok(acc_ref):
    return jnp.min(acc_ref[:, :, HEAD_DIM:HEAD_DIM + 1, :]) > L_FLOOR


def _flash_finish(o_ref, ot_ref, acc_ref, tq):
    for p in range(N_PAIRS):
        for e in range(2):
            lo = p * LANES + e * HEAD_DIM
            ot_ref[lo:lo + HEAD_DIM, :] = acc_ref[p, e, :HEAD_DIM] / acc_ref[p, e, HEAD_DIM:HEAD_DIM + 1]
    o_ref[...] = ot_ref[...].T


MOBA_UNROLL = 2
DSA_UNROLL = 2
LONG_UNROLLS = (8, 4)
SCORE_UNROLLS = (8, 4, 2)


def _moba_kernel(qt_ref, k_ref, vt_ref, o_ref, kmean_ref, kb_ref, qop_ref, sel_ref, m_ref, acc_ref,
                 ot_ref, *, seq, topk):
    i = pl.program_id(1)
    blk = MOBA_BLOCK
    nb = seq // blk
    tq = blk

    @pl.when(i == 0)
    def _():
        r = lax.broadcasted_iota(jnp.int32, (nb, seq), 0)
        c = lax.broadcasted_iota(jnp.int32, (nb, seq), 1)
        member = jnp.where((c >= r * blk) & (c < (r + 1) * blk), 1.0, 0.0).astype(BF16)
        ksum = jnp.dot(member, k_ref[...], preferred_element_type=F32)
        kmean_ref[...] = (ksum * (1.0 / blk)).astype(BF16)
        _key_norm_bound(k_ref, kb_ref, seq, tq)

    blk_id = lax.broadcasted_iota(jnp.int32, (nb, 2 * tq), 0)
    past = blk_id < i
    own_start = pl.multiple_of(i * blk, blk)

    for p in range(N_PAIRS):
        lanes = slice(p * LANES, (p + 1) * LANES)
        qop = _pair_operand(qt_ref, p)
        qop_ref[p, :LANES] = qop
        qop_ref[p, LANES:LANES + ONES_ROWS] = _offset_rows(qop, kb_ref[p])
        gate = jnp.dot(kmean_ref[:, lanes], qop, preferred_element_type=F32)
        gate = jnp.where(past, gate, -jnp.inf)
        sel = jnp.zeros((nb, 2 * tq), jnp.bool_)
        for _ in range(topk):
            mx = jnp.max(gate, axis=0, keepdims=True)
            first = jnp.min(jnp.where(gate == mx, blk_id, nb), axis=0, keepdims=True)
            hit = blk_id == first
            sel = sel | hit
            gate = jnp.where(hit, -jnp.inf, gate)
        sel_ref[p] = jnp.where(sel & past, 1.0, 0.0)
        qop_ref[p, LANES + ONES_ROWS:LANES + ONES_ROWS + nb] = jnp.where(sel & past, 0.0, NEG_BIG).astype(BF16)
        qop_ref[p, LANES + ONES_ROWS + nb:] = jnp.zeros((LANES - ONES_ROWS - nb, 2 * tq), BF16)

    def lanes(p):
        return slice(p * LANES, (p + 1) * LANES)

    krow = lax.broadcasted_iota(jnp.int32, (blk, 2 * tq), 0)
    qcol = lax.broadcasted_iota(jnp.int32, (blk, 2 * tq), 1)
    causal = krow <= jnp.where(qcol < tq, qcol, qcol - tq)
    lane = lax.broadcasted_iota(jnp.int32, (blk, LANES), 1)

    def attend(fast):
        acc_ref[...] = jnp.zeros(acc_ref.shape, F32)
        if not fast:
            m_ref[...] = jnp.full(m_ref.shape, NEG_BIG, F32)

        def key_operand(start, extra_lane):
            k_tile = k_ref[pl.ds(start, blk), :]
            if not fast:
                return lambda p: k_tile[:, lanes(p)]
            sel_lanes = (lane == 0) if extra_lane is None else ((lane == 0) | (lane == extra_lane))
            extra = jnp.where(sel_lanes, 1.0, 0.0).astype(BF16)
            return lambda p: jnp.concatenate([k_tile[:, lanes(p)], extra], axis=1)

        def query_operand(p):
            return qop_ref[p] if fast else qop_ref[p, :LANES]

        def update(vt_tile):
            if fast:
                return lambda s, u, p: _offset_update(s, vt_tile(u, p), acc_ref, p)
            return lambda s, u, p: _online_update(s, vt_tile(u, p), m_ref, acc_ref, p)

        own_keys = key_operand(own_start, None)
        _flash_tiles(
            [(0, p) for p in range(N_PAIRS)],
            lambda u, p: jnp.dot(own_keys(p), query_operand(p), preferred_element_type=F32),
            lambda s, u, p: jnp.where(causal, s, -jnp.inf),
            update(lambda u, p: vt_ref[lanes(p), pl.ds(own_start, blk)]))

        def body(c, carry, unroll, first):
            js = [first + c * unroll + u for u in range(unroll)]
            starts = [pl.multiple_of(j * blk, blk) for j in js]
            keys = [key_operand(starts[u], ONES_ROWS + js[u]) for u in range(unroll)]
            _flash_tiles(
                [(u, p) for u in range(unroll) for p in range(N_PAIRS)],
                lambda u, p: jnp.dot(keys[u](p), query_operand(p), preferred_element_type=F32),
                (lambda s, u, p: s) if fast else
                (lambda s, u, p: jnp.where(sel_ref[p, pl.ds(js[u], 1), :] > 0.5, s, -jnp.inf)),
                update(lambda u, p: vt_ref[lanes(p), pl.ds(starts[u], blk)]))
            return carry

        done = 0
        for unroll in (LONG_UNROLLS if fast else ()):
            n_long = (i - done) // unroll
            lax.fori_loop(0, n_long, functools.partial(body, unroll=unroll, first=done), 0)
            done = done + n_long * unroll
        lax.fori_loop(0, (i - done + MOBA_UNROLL - 1) // MOBA_UNROLL,
                      functools.partial(body, unroll=MOBA_UNROLL, first=done), 0)

    attend(True)

    @pl.when(jnp.logical_not(_denominator_ok(acc_ref)))
    def _():
        attend(False)

    _flash_finish(o_ref, ot_ref, acc_ref, tq)


def _moba(qt_all, k_all, vt_all, batch, seq, q_row_blk, k_col_blk, v_row_blk):
    tq = MOBA_BLOCK
    nq = seq // tq
    nb = seq // MOBA_BLOCK
    topk = min(MOBA_TOPK, nb - 1)
    w = BRANCH_WIDTH
    return pl.pallas_call(
        functools.partial(_moba_kernel, seq=seq, topk=topk),
        grid=(batch, nq),
        in_specs=[
            pl.BlockSpec((w, tq), lambda b, i: (q_row_blk, b * nq + i)),
            pl.BlockSpec((seq, w), lambda b, i: (b, k_col_blk), pipeline_mode=pl.Buffered(1)),
            pl.BlockSpec((w, seq), lambda b, i: (v_row_blk, b), pipeline_mode=pl.Buffered(1)),
        ],
        out_specs=pl.BlockSpec((tq, w), lambda b, i: (b * nq + i, 0)),
        out_shape=jax.ShapeDtypeStruct((batch * seq, w), F32),
        scratch_shapes=[pltpu.VMEM((nb, w), BF16),
                        pltpu.VMEM((N_PAIRS, 1, 2 * tq), F32),
                        pltpu.VMEM((N_PAIRS, 2 * LANES, 2 * tq), BF16),
                        pltpu.VMEM((N_PAIRS, nb, 2 * tq), F32),
                        pltpu.VMEM((N_PAIRS, 1, 2 * tq), F32),
                        pltpu.VMEM((N_PAIRS, 2, HEAD_DIM + ONES_ROWS, tq), F32),
                        pltpu.VMEM((w, tq), F32)],
        compiler_params=_cparams(2),
        name="moba",
    )(qt_all, k_all, vt_all)


def _ordered_key(bits):
    return jnp.where(bits < 0, jnp.int32(INT_MIN) - bits, bits)


def _tree_sum(parts):
    while len(parts) > 1:
        parts = [parts[a] + parts[a + 1] for a in range(0, len(parts) - 1, 2)] + (
            [parts[-1]] if len(parts) % 2 else [])
    return parts[0]


def _dsa_kernel(qit_ref, ki_ref, wt_ref, qt_ref, k_ref, vt_ref, o_ref,
                score_ref, hi_ref, lo_ref, kb_ref, qop_ref, m_ref, acc_ref, ot_ref,
                *, tq, kc, kcb, n_sel):
    i = pl.program_id(1)
    nch = (i * tq + tq) // kc
    nchb = (i * tq + tq + kcb - 1) // kcb
    krow = lax.broadcasted_iota(jnp.int32, (kc, tq), 0)
    qpos = i * tq + lax.broadcasted_iota(jnp.int32, (kc, tq), 1)
    n_sel_f = float(n_sel)

    @pl.when(i == 0)
    def _():
        _key_norm_bound(k_ref, kb_ref, k_ref.shape[0], tq)

    for p in range(N_PAIRS):
        qop_ref[p, :LANES] = _pair_operand(qit_ref, p)
    w_rows = [wt_ref[h:h + 1, :] * (N_HEADS ** -0.5) for h in range(N_HEADS)]

    def score_body(c, carry, masked, unroll, first):
        starts = [pl.multiple_of((first + c * unroll + u) * kc, kc) for u in range(unroll)]
        accs = [jnp.zeros((kc, tq), F32) for _ in range(unroll)]

        def head_scores(u, p):
            return jnp.dot(ki_ref[pl.ds(starts[u], kc), :], qop_ref[p, :LANES], preferred_element_type=F32)

        def accumulate(r, u, p):
            accs[u] = accs[u] + jnp.maximum(r[:, :tq], 0.0) * w_rows[2 * p]
            accs[u] = accs[u] + jnp.maximum(r[:, tq:], 0.0) * w_rows[2 * p + 1]
            if p < N_PAIRS - 1:
                return
            sc = jnp.where(starts[u] + krow <= qpos, accs[u], -jnp.inf) if masked else accs[u]
            score_ref[pl.ds(starts[u], kc), :] = sc
            key = _ordered_key(lax.bitcast_convert_type(sc, jnp.int32))
            hi_ref[pl.ds(starts[u], kc), :] = jnp.right_shift(key, 16).astype(jnp.int16)
            lo_ref[pl.ds(starts[u], kc), :] = (key ^ jnp.int32(0x8000)).astype(jnp.int16)

        _flash_tiles([(u, p) for u in range(unroll) for p in range(N_PAIRS)],
                     head_scores, lambda r, u, p: r, accumulate)
        return carry

    n_full = (i * tq + 1) // kc
    done = 0
    for unroll in SCORE_UNROLLS + (1,):
        n_trips = (n_full - done) // unroll
        lax.fori_loop(0, n_trips, functools.partial(score_body, masked=False, unroll=unroll, first=done), 0)
        done = done + n_trips * unroll
    lax.fori_loop(0, nch - n_full, functools.partial(score_body, masked=True, unroll=1, first=n_full), 0)

    @pl.when(nch * kc < nchb * kcb)
    def _():
        pad_rows = pl.ds(pl.multiple_of(nch * kc, kc), kcb - kc)
        score_ref[pad_rows, :] = jnp.full((kcb - kc, tq), -jnp.inf, F32)
        hi_ref[pad_rows, :] = jnp.full((kcb - kc, tq), I16_MIN, jnp.int16)
        lo_ref[pad_rows, :] = jnp.full((kcb - kc, tq), I16_MIN, jnp.int16)

    def count16(ref, pred):
        n_acc = 4
        rows = 2 * SUBLANES

        def body(c, accs):
            x = ref[pl.ds(pl.multiple_of(c * kcb, kcb), kcb), :]
            accs = list(accs)
            for r in range(kcb // rows):
                hit = jnp.where(pred(x[r * rows:(r + 1) * rows]), jnp.int16(1), jnp.int16(0))
                accs[r % n_acc] = accs[r % n_acc] + hit
            return tuple(accs)
        accs = lax.fori_loop(0, nchb, body, tuple(jnp.zeros((rows, tq), jnp.int16) for _ in range(n_acc)))
        tot = _tree_sum(list(accs)).astype(jnp.int32).astype(F32)
        return jnp.sum(tot, axis=0, keepdims=True)

    def bisect16(ref, target, count_at_min):
        def body(it, carry):
            t, c_ge, c_gt = carry
            cand = t + jnp.left_shift(jnp.int32(1), 15 - it)
            cand16 = cand.astype(jnp.int16)
            cnt = count16(ref, lambda x: x >= cand16)
            ok = cnt >= target
            return jnp.where(ok, cand, t), jnp.where(ok, cnt, c_ge), jnp.where(ok, c_gt, cnt)
        start = jnp.full((1, tq), I16_MIN, jnp.int32)
        return lax.fori_loop(0, 16, body, (start, count_at_min, jnp.zeros((1, tq), F32)))

    rows_seen = jnp.full((1, tq), (nchb * kcb).astype(F32))
    t_hi, c_ge_hi, c_gt_hi = bisect16(hi_ref, n_sel_f, rows_seen)
    t_hi16 = t_hi.astype(jnp.int16)

    def low_body(c, carry):
        rows = pl.ds(pl.multiple_of(c * kcb, kcb), kcb)
        lo_ref[rows, :] = jnp.where(hi_ref[rows, :] == t_hi16, lo_ref[rows, :], jnp.int16(I16_MIN))
        return carry

    lax.fori_loop(0, nchb, low_body, 0)
    t_lo, c_ge_lo, _ = bisect16(lo_ref, n_sel_f - c_gt_hi, c_ge_hi - c_gt_hi)
    n_ge = c_gt_hi + c_ge_lo
    key_t = t_hi * 65536 + (t_lo + 32768)
    short = key_t <= KEY_NEG_INF
    thr = jnp.where(short, -FLT_MAX, lax.bitcast_convert_type(_ordered_key(key_t), F32))
    has_tie = jnp.max(jnp.where(short, 0.0, n_ge - n_sel_f)) > 0.0

    @pl.when(has_tie)
    def _():
        lower = (lax.broadcasted_iota(jnp.int32, (kc, kc), 1)
                 < lax.broadcasted_iota(jnp.int32, (kc, kc), 0))
        lower = jnp.where(lower, 1.0, 0.0).astype(BF16)

        def gt_body(c, cnt):
            sc = score_ref[pl.ds(pl.multiple_of(c * kc, kc), kc), :]
            return cnt + jnp.sum(jnp.where(sc > thr, 1.0, 0.0), axis=0, keepdims=True)
        n_gt = lax.fori_loop(0, nch, gt_body, jnp.zeros((1, tq), F32))
        need = n_sel_f - n_gt

        def body(c, seen):
            start = pl.multiple_of(c * kc, kc)
            sc = score_ref[pl.ds(start, kc), :]
            eq = jnp.where(sc == thr, 1.0, 0.0)
            rank = seen + jnp.dot(lower, eq.astype(BF16), preferred_element_type=F32)
            keep = (sc > thr) | ((sc == thr) & (rank < need))
            keep = keep & (start + krow <= qpos)
            score_ref[pl.ds(start, kc), :] = jnp.where(keep, jnp.inf, -jnp.inf)
            return seen + jnp.sum(eq, axis=0, keepdims=True)
        lax.fori_loop(0, nch, body, jnp.zeros((1, tq), F32))

    for p in range(N_PAIRS):
        qop = _pair_operand(qt_ref, p)
        qop_ref[p, :LANES] = qop
        qop_ref[p, LANES:LANES + ONES_ROWS] = _offset_rows(qop, kb_ref[p])
        qop_ref[p, LANES + ONES_ROWS:] = jnp.zeros((LANES - ONES_ROWS, 2 * tq), BF16)

    def lanes(p):
        return slice(p * LANES, (p + 1) * LANES)

    def attend(fast):
        acc_ref[...] = jnp.zeros(acc_ref.shape, F32)
        if not fast:
            m_ref[...] = jnp.full(m_ref.shape, NEG_BIG, F32)
        ones = jnp.ones((kc, LANES), BF16)

        def att_body(c, carry, unroll, first):
            starts = [pl.multiple_of((first + c * unroll + u) * kc, kc) for u in range(unroll)]

            def qk(u, p):
                k_tile = k_ref[pl.ds(starts[u], kc), lanes(p)]
                if fast:
                    return jnp.dot(jnp.concatenate([k_tile, ones], axis=1), qop_ref[p],
                                   preferred_element_type=F32)
                return jnp.dot(k_tile, qop_ref[p, :LANES], preferred_element_type=F32)

            def add_bias(s, u, p):
                bias = jnp.where(score_ref[pl.ds(starts[u], kc), :] >= thr, 0.0, NEG_BIG)
                return jnp.concatenate([s[:, :tq] + bias, s[:, tq:] + bias], axis=1)

            def update(s, u, p):
                vt_tile = vt_ref[lanes(p), pl.ds(starts[u], kc)]
                if fast:
                    _offset_update(s, vt_tile, acc_ref, p)
                else:
                    _online_update(s, vt_tile, m_ref, acc_ref, p)

            _flash_tiles([(u, p) for u in range(unroll) for p in range(N_PAIRS)], qk, add_bias, update)
            return carry

        done = 0
        for unroll in (LONG_UNROLLS if fast else ()):
            n_long = (nch - done) // unroll
            lax.fori_loop(0, n_long, functools.partial(att_body, unroll=unroll, first=done), 0)
            done = done + n_long * unroll
        lax.fori_loop(0, (nch - done + DSA_UNROLL - 1) // DSA_UNROLL,
                      functools.partial(att_body, unroll=DSA_UNROLL, first=done), 0)

    attend(True)

    @pl.when(jnp.logical_not(_denominator_ok(acc_ref)))
    def _():
        attend(False)

    _flash_finish(o_ref, ot_ref, acc_ref, tq)


def _dsa(qt_all, k_all, vt_all, wt, batch, seq, tq, kc, kcb):
    nq = seq // tq
    w = BRANCH_WIDTH
    n_sel = min(IDX_TOPK_MAX, seq // 4)
    assert tq % kc == 0 and kcb == DSA_UNROLL * kc and seq % kcb == 0
    ki_col_blk = (2 * w) // LANES
    return pl.pallas_call(
        functools.partial(_dsa_kernel, tq=tq, kc=kc, kcb=kcb, n_sel=n_sel),
        grid=(batch, nq),
        in_specs=[
            pl.BlockSpec((w, tq), lambda b, i: (2, b * nq + i)),
            pl.BlockSpec((seq, LANES), lambda b, i: (b, ki_col_blk), pipeline_mode=pl.Buffered(1)),
            pl.BlockSpec((2 * SUBLANES, tq), lambda b, i: (0, b * nq + i)),
            pl.BlockSpec((w, tq), lambda b, i: (1, b * nq + i)),
            pl.BlockSpec((seq, w), lambda b, i: (b, 1), pipeline_mode=pl.Buffered(1)),
            pl.BlockSpec((w, seq), lambda b, i: (1, b), pipeline_mode=pl.Buffered(1)),
        ],
        out_specs=pl.BlockSpec((tq, w), lambda b, i: (b * nq + i, 0)),
        out_shape=jax.ShapeDtypeStruct((batch * seq, w), F32),
        scratch_shapes=[pltpu.VMEM((seq, tq), F32),
                        pltpu.VMEM((seq, tq), jnp.int16),
                        pltpu.VMEM((seq, tq), jnp.int16),
                        pltpu.VMEM((N_PAIRS, 1, 2 * tq), F32),
                        pltpu.VMEM((N_PAIRS, 2 * LANES, 2 * tq), BF16),
                        pltpu.VMEM((N_PAIRS, 1, 2 * tq), F32),
                        pltpu.VMEM((N_PAIRS, 2, HEAD_DIM + ONES_ROWS, tq), F32),
                        pltpu.VMEM((w, tq), F32)],
        compiler_params=_cparams(2),
        name="dsa",
    )(qt_all, k_all, wt, qt_all, k_all, vt_all)


def _merge_kernel(ya_ref, yb_ref, h_ref, kv_ref, wmq_ref, wg_ref, wmix_ref, wb_ref, wo_ref, x_ref, gn_ref,
                  *out_refs, last):
    w = BRANCH_WIDTH
    d = x_ref.shape[1]
    h = h_ref[...]
    mq = jnp.dot(h, wmq_ref[...], preferred_element_type=F32).astype(BF16)
    ym_heads = []
    for hd in range(M_HEADS):
        cols = slice(hd * M_HEAD_DIM, (hd + 1) * M_HEAD_DIM)
        km = kv_ref[:, cols]
        vm = kv_ref[:, w + hd * M_HEAD_DIM:w + (hd + 1) * M_HEAD_DIM]
        sc = lax.dot_general(mq[:, cols], km, (((1,), (1,)), ((), ())),
                             preferred_element_type=F32) * (M_HEAD_DIM ** -0.5)
        sc = sc - jnp.max(sc, axis=-1, keepdims=True)
        e = jnp.exp(sc)
        p = e / jnp.sum(e, axis=-1, keepdims=True)
        ym_heads.append(jnp.dot(p.astype(BF16), vm, preferred_element_type=F32))
    ym = jnp.concatenate(ym_heads, axis=-1)
    ys = (ya_ref[...], yb_ref[...], ym)
    merged = None
    for n in range(3):
        g = jnp.dot(h, wg_ref[:, n * w:(n + 1) * w], preferred_element_type=F32)
        y = (ys[n] * (g * jax.nn.sigmoid(g))).astype(BF16)
        up = jnp.dot(y, wb_ref[n], preferred_element_type=F32)
        mix = jax.nn.sigmoid(jnp.dot(h, wmix_ref[:, n * d:(n + 1) * d], preferred_element_type=F32))
        term = mix * up
        merged = term if merged is None else merged + term
    x_new = x_ref[...] + jnp.dot(merged.astype(BF16), wo_ref[...], preferred_element_type=F32)
    y = x_new * lax.rsqrt(jnp.mean(x_new * x_new, axis=-1, keepdims=True) + RMS_EPS) * gn_ref[...]
    if last:
        out_refs[0][...] = y
    else:
        out_refs[0][...] = x_new
        out_refs[1][...] = y.astype(BF16)


def _merge(ya, yb, h, kv, w_mq, w_g, w_mix, wb, wo, x2d, g_next, last, seq, n_mem, tm):
    m, d = x2d.shape
    w = BRANCH_WIDTH
    nt = seq // tm

    def resident(shape):
        return pl.BlockSpec(shape, lambda i: (0,) * len(shape), pipeline_mode=pl.Buffered(1))

    row_tile = pl.BlockSpec((tm, d), lambda i: (i, 0))
    out_f32 = jax.ShapeDtypeStruct((m, d), F32)
    return pl.pallas_call(
        functools.partial(_merge_kernel, last=last),
        grid=(m // tm,),
        in_specs=[
            pl.BlockSpec((tm, w), lambda i: (i, 0)),
            pl.BlockSpec((tm, w), lambda i: (i, 0)),
            pl.BlockSpec((tm, d), lambda i: (i, 0)),
            pl.BlockSpec((n_mem, 2 * w), lambda i: (i // nt, 0)),
            resident((d, w)),
            resident((d, 3 * w)),
            resident((d, 3 * d)),
            resident((3, w, d)),
            resident((d, d)),
            row_tile,
            resident((1, d)),
        ],
        out_specs=row_tile if last else (row_tile, row_tile),
        out_shape=out_f32 if last else (out_f32, jax.ShapeDtypeStruct((m, d), BF16)),
        compiler_params=_cparams(1),
        name="merge",
    )(ya, yb, h, kv, w_mq, w_g, w_mix, wb, wo, x2d, g_next.reshape(1, d))


def _rope_tables(seq):
    inv_freq = ROPE_THETA ** (-jnp.arange(ROT_HALF, dtype=F32) / ROT_HALF)
    ang = jnp.arange(seq, dtype=jnp.int32).astype(F32)[:, None] * inv_freq[None, :]
    cos, sin = jnp.cos(ang), jnp.sin(ang)
    r = np.arange(LANES) % HEAD_DIM
    f = r % ROT_HALF
    first = jnp.asarray(r < ROT_HALF)
    second = jnp.asarray((r >= ROT_HALF) & (r < 2 * ROT_HALF))
    rot = jnp.asarray(r < 2 * ROT_HALF)
    c_tok = jnp.where(rot[None, :], cos[:, f], 1.0)
    s1_tok = jnp.where(first[None, :], -sin[:, f], 0.0)
    s2_tok = jnp.where(second[None, :], sin[:, f], 0.0)
    return (c_tok, s1_tok, s2_tok), (cos.T, sin.T)


def kernel(x, mem, norm_g, w_in, mem_norm_g, w_mem_kv, w_branch, w_out, final_g):
    batch, seq, d = x.shape
    n_mem = mem.shape[1]
    depth = norm_g.shape[0]
    w = BRANCH_WIDTH
    m = batch * seq
    assert seq % MOBA_BLOCK == 0 and d % LANES == 0

    tm = min(1024, seq)
    tm_merge = min(512, seq)
    dsa_tq = 256
    dsa_kc = 256
    dsa_kcb = min(512, seq)

    tok_tabs, feat_tabs = _rope_tables(seq)
    offs = np.cumsum([0, w, w, w, w, w, w, w, w, N_HEADS * HEAD_DIM, HEAD_DIM, N_HEADS, w, w, 3 * d])
    (o_aq, o_ak, o_av, o_ag, o_bq, o_bk, o_bv, o_bg, o_iq, o_ik, o_iw, o_mq, o_mg, o_mix, o_end) = offs
    attn_scale = HEAD_DIM ** -0.5 * float(np.log2(np.e))
    idx_scale = HEAD_DIM ** -0.5

    x2d = x.reshape(m, d)
    mem2d = mem.reshape(batch * n_mem, d)
    h = _rmsnorm(x2d, norm_g[0], BF16, tm)
    for l in range(depth):
        col = lambda a, b: lax.slice(w_in, (l, 0, a), (l + 1, d, b)).reshape(d, b - a)
        w_k = jnp.concatenate([col(o_ak, o_av), col(o_bk, o_bv), col(o_ik, o_iw), col(o_ik, o_iw)],
                              axis=1).astype(BF16)
        w_qt = jnp.concatenate([col(o_aq, o_ak) * attn_scale, col(o_bq, o_bk) * attn_scale,
                                col(o_iq, o_ik) * idx_scale], axis=1).T.astype(BF16)
        w_vt = jnp.concatenate([col(o_av, o_ag), col(o_bv, o_bg)], axis=1).T.astype(BF16)
        w_iwt = jnp.pad(col(o_iw, o_mq).T, ((0, 2 * SUBLANES - N_HEADS), (0, 0))).astype(BF16)
        w_mq = col(o_mq, o_mg).astype(BF16)
        w_g = jnp.concatenate([col(o_ag, o_bq), col(o_bg, o_iq), col(o_mg, o_mix)], axis=1).astype(BF16)
        w_mix = col(o_mix, o_end).astype(BF16)

        k_all, qt_all, vt_all, iwt = _in_proj(h, w_k, w_qt, w_vt, w_iwt, tok_tabs, feat_tabs, seq, tm)

        ya = _moba(qt_all, k_all, vt_all, batch, seq, 0, 0, 0)
        yb = _dsa(qt_all, k_all, vt_all, iwt, batch, seq, dsa_tq, dsa_kc, dsa_kcb)

        mn = _rmsnorm(mem2d, mem_norm_g[l], BF16, n_mem)
        kv = _mm_tok(mn, w_mem_kv[l].astype(BF16), BF16, n_mem, 2 * w)

        last = l == depth - 1
        res = _merge(ya, yb, h, kv, w_mq, w_g, w_mix, w_branch[l].astype(BF16), w_out[l].astype(BF16),
                     x2d, final_g if last else norm_g[l + 1], last, seq, n_mem, tm_merge)
        if not last:
            x2d, h = res

    return res.reshape(batch, seq, d)
```

```python
import functools

import jax
import jax.numpy as jnp
import numpy as np
from jax import lax
from jax.experimental import pallas as pl
from jax.experimental.pallas import tpu as pltpu

F32 = jnp.float32
BF16 = jnp.bfloat16

HEAD_DIM = 64
N_HEADS = 8
BRANCH_WIDTH = 512
M_HEADS = 4
M_HEAD_DIM = 128
MOBA_BLOCK = 256
MOBA_TOPK = 3
IDX_TOPK_MAX = 256
ROPE_THETA = 500000.0
ROT_HALF = HEAD_DIM // 4 // 2
RMS_EPS = 1e-6

LANES = 128
SUBLANES = 8
VMEM_LIMIT = 56 * 1024 * 1024
NEG_BIG = -1e30
FLT_MAX = float(np.finfo(np.float32).max)
INT_MIN = -(2 ** 31)
I16_MIN = -(2 ** 15)
KEY_NEG_INF = -0x7F800000


def _cparams(n_axes):
    return pltpu.CompilerParams(dimension_semantics=("arbitrary",) * n_axes,
                                vmem_limit_bytes=VMEM_LIMIT)


def _rmsnorm_kernel(x_ref, g_ref, o_ref):
    xf = x_ref[...]
    y = xf * lax.rsqrt(jnp.mean(xf * xf, axis=-1, keepdims=True) + RMS_EPS)
    o_ref[...] = (y * g_ref[...]).astype(o_ref.dtype)


def _rmsnorm(x2d, g, out_dtype, tm):
    m, d = x2d.shape
    return pl.pallas_call(
        _rmsnorm_kernel,
        grid=(m // tm,),
        in_specs=[pl.BlockSpec((tm, d), lambda i: (i, 0)),
                  pl.BlockSpec((1, d), lambda i: (0, 0))],
        out_specs=pl.BlockSpec((tm, d), lambda i: (i, 0)),
        out_shape=jax.ShapeDtypeStruct((m, d), out_dtype),
        compiler_params=_cparams(1),
        name="rmsnorm",
    )(x2d, g.reshape(1, d))


def _mm_tok_kernel(h_ref, w_ref, o_ref):
    o_ref[...] = jnp.dot(h_ref[...], w_ref[...], preferred_element_type=F32).astype(o_ref.dtype)


def _mm_tok(h, w, out_dtype, tm, tn):
    m, k = h.shape
    n = w.shape[1]
    return pl.pallas_call(
        _mm_tok_kernel,
        grid=(m // tm, n // tn),
        in_specs=[pl.BlockSpec((tm, k), lambda i, j: (i, 0)),
                  pl.BlockSpec((k, tn), lambda i, j: (0, j))],
        out_specs=pl.BlockSpec((tm, tn), lambda i, j: (i, j)),
        out_shape=jax.ShapeDtypeStruct((m, n), out_dtype),
        compiler_params=_cparams(2),
        name="proj_tok",
    )(h, w)


def _in_proj_kernel(h_ref, wk_ref, wqt_ref, wvt_ref, wiw_ref, c_ref, s1_ref, s2_ref, cos_ref, sin_ref,
                    k_ref, qt_ref, vt_ref, iw_ref):
    h = h_ref[...]
    nt_dims = (((1,), (1,)), ((), ()))

    k_acc = jnp.dot(h, wk_ref[...], preferred_element_type=F32)
    c, s1, s2 = c_ref[...], s1_ref[...], s2_ref[...]
    for j in range(k_acc.shape[1] // LANES):
        piece = k_acc[:, j * LANES:(j + 1) * LANES]
        up = pltpu.roll(piece, LANES - ROT_HALF, 1)
        down = pltpu.roll(piece, ROT_HALF, 1)
        k_ref[:, j * LANES:(j + 1) * LANES] = (piece * c + up * s1 + down * s2).astype(k_ref.dtype)

    cos, sin = cos_ref[...], sin_ref[...]
    w = BRANCH_WIDTH
    for blk in range(wqt_ref.shape[0] // w):
        acc = lax.dot_general(wqt_ref[blk * w:(blk + 1) * w, :], h, nt_dims, preferred_element_type=F32)
        pieces = []
        for hh in range(w // HEAD_DIM):
            base = hh * HEAD_DIM
            x1 = acc[base:base + ROT_HALF]
            x2 = acc[base + ROT_HALF:base + 2 * ROT_HALF]
            pieces += [x1 * cos - x2 * sin, x2 * cos + x1 * sin, acc[base + 2 * ROT_HALF:base + HEAD_DIM]]
        qt_ref[blk * w:(blk + 1) * w, :] = jnp.concatenate(pieces, axis=0).astype(qt_ref.dtype)

    vt_ref[...] = lax.dot_general(wvt_ref[...], h, nt_dims, preferred_element_type=F32).astype(vt_ref.dtype)
    iw_ref[...] = lax.dot_general(wiw_ref[...], h, nt_dims, preferred_element_type=F32)


def _in_proj(h, w_k, w_qt, w_vt, w_iwt, tok_tabs, feat_tabs, seq, tm):
    m, k = h.shape
    nt = seq // tm

    def resident(shape):
        return pl.BlockSpec(shape, lambda i: (0,) * len(shape), pipeline_mode=pl.Buffered(1))

    tok_tab = pl.BlockSpec((tm, LANES), lambda i: (i % nt, 0))
    feat_tab = pl.BlockSpec((SUBLANES, tm), lambda i: (0, i % nt))
    n_k, n_q, n_v, n_w = w_k.shape[1], w_qt.shape[0], w_vt.shape[0], w_iwt.shape[0]
    return pl.pallas_call(
        _in_proj_kernel,
        grid=(m // tm,),
        in_specs=[pl.BlockSpec((tm, k), lambda i: (i, 0)),
                  resident(w_k.shape), resident(w_qt.shape), resident(w_vt.shape), resident(w_iwt.shape),
                  tok_tab, tok_tab, tok_tab, feat_tab, feat_tab],
        out_specs=(pl.BlockSpec((tm, n_k), lambda i: (i, 0)),
                   pl.BlockSpec((n_q, tm), lambda i: (0, i)),
                   pl.BlockSpec((n_v, tm), lambda i: (0, i)),
                   pl.BlockSpec((n_w, tm), lambda i: (0, i))),
        out_shape=(jax.ShapeDtypeStruct((m, n_k), BF16),
                   jax.ShapeDtypeStruct((n_q, m), BF16),
                   jax.ShapeDtypeStruct((n_v, m), BF16),
                   jax.ShapeDtypeStruct((n_w, m), F32)),
        compiler_params=_cparams(1),
        name="in_proj",
    )(h, w_k, w_qt, w_vt, w_iwt, *tok_tabs, *feat_tabs)


N_PAIRS = N_HEADS // 2


def _pair_operand(qt_ref, pair):
    qp = qt_ref[pair * LANES:(pair + 1) * LANES, :].astype(F32)
    row = lax.broadcasted_iota(jnp.int32, qp.shape, 0)
    lo = jnp.where(row < HEAD_DIM, qp, 0.0)
    hi = jnp.where(row >= HEAD_DIM, qp, 0.0)
    return jnp.concatenate([lo, hi], axis=1).astype(BF16)


ONES_ROWS = 16
QK_LOOKAHEAD = 4


OFFSET_SLACK = 1.0 + 2.0 ** -5
L_FLOOR = 2.0 ** -80


def _pv_accumulate(pm, vt_tile, acc_ref, p, scale=None):
    keys = vt_tile.shape[1]
    tq = pm.shape[1] // 2
    ones = jnp.ones((ONES_ROWS, keys), BF16)
    for e in range(2):
        vt_ext = jnp.concatenate([vt_tile[e * HEAD_DIM:(e + 1) * HEAD_DIM], ones], axis=0)
        cols = slice(e * tq, (e + 1) * tq)
        upd = jnp.dot(vt_ext, pm[:, cols], preferred_element_type=F32)
        old = acc_ref[p, e] if scale is None else scale[:, cols] * acc_ref[p, e]
        acc_ref[p, e] = old + upd


def _online_update(s, vt_tile, m_ref, acc_ref, p):
    m = m_ref[p]
    m_new = jnp.maximum(m, jnp.max(s, axis=0, keepdims=True))
    _pv_accumulate(jnp.exp2(s - m_new).astype(BF16), vt_tile, acc_ref, p, scale=jnp.exp2(m - m_new))
    m_ref[p] = m_new


def _offset_update(s, vt_tile, acc_ref, p):
    _pv_accumulate(jnp.exp2(s).astype(BF16), vt_tile, acc_ref, p)


def _flash_tiles(tiles, qk, mask, update):
    ss = [qk(*t) for t in tiles[:QK_LOOKAHEAD]]
    for n, (u, p) in enumerate(tiles):
        if n + QK_LOOKAHEAD < len(tiles):
            ss.append(qk(*tiles[n + QK_LOOKAHEAD]))
        update(mask(ss[n], u, p), u, p)
        ss[n] = None


def _key_norm_bound(k_ref, kb_ref, seq, tq):
    chunk = min(512, seq)
    width = k_ref.shape[1]
    lane = lax.broadcasted_iota(jnp.int32, (width, LANES), 0)
    head = lax.broadcasted_iota(jnp.int32, (width, LANES), 1)
    group = jnp.where((lane >= head * HEAD_DIM) & (lane < (head + 1) * HEAD_DIM), 1.0, 0.0).astype(BF16)

    def body(c, mx):
        kk = k_ref[pl.ds(pl.multiple_of(c * chunk, chunk), chunk), :].astype(F32)
        n2 = jnp.dot((kk * kk).astype(BF16), group, preferred_element_type=F32)
        return jnp.maximum(mx, jnp.max(n2, axis=0, keepdims=True))
    mx = lax.fori_loop(0, seq // chunk, body, jnp.zeros((1, LANES), F32))
    for p in range(N_PAIRS):
        kb_ref[p] = jnp.concatenate([jnp.broadcast_to(mx[:, 2 * p:2 * p + 1], (1, tq)),
                                     jnp.broadcast_to(mx[:, 2 * p + 1:2 * p + 2], (1, tq))], axis=1)


def _offset_rows(qop, kb2):
    qf = qop.astype(F32)
    bound = jnp.sqrt(jnp.sum(qf * qf, axis=0, keepdims=True) * kb2) * OFFSET_SLACK
    row = lax.broadcasted_iota(jnp.int32, (ONES_ROWS, qop.shape[1]), 0)
    return jnp.where(row == 0, -bound, 0.0).astype(BF16)


def _denominator_ok(acc_ref):
    return jnp.min(acc_ref[:, :, HEAD_DIM:HEAD_DIM + 1, :]) > L_FLOOR


def _flash_finish(o_ref, ot_ref, acc_ref, tq):
    for p in range(N_PAIRS):
        for e in range(2):
            lo = p * LANES + e * HEAD_DIM
            ot_ref[lo:lo + HEAD_DIM, :] = acc_ref[p, e, :HEAD_DIM] / acc_ref[p, e, HEAD_DIM:HEAD_DIM + 1]
    o_ref[...] = ot_ref[...].T


MOBA_UNROLL = 1
DSA_UNROLL = 1
LONG_UNROLLS = (8, 4, 2)
SCORE_UNROLLS = (8, 4, 2)


def _moba_kernel(qt_ref, k_ref, vt_ref, o_ref, kmean_ref, kb_ref, qop_ref, sel_ref, m_ref, acc_ref,
                 ot_ref, *, seq, topk):
    i = pl.program_id(1)
    blk = MOBA_BLOCK
    nb = seq // blk
    tq = blk

    @pl.when(i == 0)
    def _():
        r = lax.broadcasted_iota(jnp.int32, (nb, seq), 0)
        c = lax.broadcasted_iota(jnp.int32, (nb, seq), 1)
        member = jnp.where((c >= r * blk) & (c < (r + 1) * blk), 1.0, 0.0).astype(BF16)
        ksum = jnp.dot(member, k_ref[...], preferred_element_type=F32)
        kmean_ref[...] = (ksum * (1.0 / blk)).astype(BF16)
        _key_norm_bound(k_ref, kb_ref, seq, tq)

    blk_id = lax.broadcasted_iota(jnp.int32, (nb, 2 * tq), 0)
    past = blk_id < i
    own_start = pl.multiple_of(i * blk, blk)

    for p in range(N_PAIRS):
        lanes = slice(p * LANES, (p + 1) * LANES)
        qop = _pair_operand(qt_ref, p)
        qop_ref[p, :LANES] = qop
        qop_ref[p, LANES:LANES + ONES_ROWS] = _offset_rows(qop, kb_ref[p])
        gate = jnp.dot(kmean_ref[:, lanes], qop, preferred_element_type=F32)
        gate = jnp.where(past, gate, -jnp.inf)
        sel = jnp.zeros((nb, 2 * tq), jnp.bool_)
        for _ in range(topk):
            mx = jnp.max(gate, axis=0, keepdims=True)
            first = jnp.min(jnp.where(gate == mx, blk_id, nb), axis=0, keepdims=True)
            hit = blk_id == first
            sel = sel | hit
            gate = jnp.where(hit, -jnp.inf, gate)
        sel_ref[p] = jnp.where(sel & past, 1.0, 0.0)
        qop_ref[p, LANES + ONES_ROWS:LANES + ONES_ROWS + nb] = jnp.where(sel & past, 0.0, NEG_BIG).astype(BF16)
        qop_ref[p, LANES + ONES_ROWS + nb:] = jnp.zeros((LANES - ONES_ROWS - nb, 2 * tq), BF16)

    def lanes(p):
        return slice(p * LANES, (p + 1) * LANES)

    krow = lax.broadcasted_iota(jnp.int32, (blk, 2 * tq), 0)
    qcol = lax.broadcasted_iota(jnp.int32, (blk, 2 * tq), 1)
    causal = krow <= jnp.where(qcol < tq, qcol, qcol - tq)
    lane = lax.broadcasted_iota(jnp.int32, (blk, LANES), 1)

    def attend(fast):
        acc_ref[...] = jnp.zeros(acc_ref.shape, F32)
        if not fast:
            m_ref[...] = jnp.full(m_ref.shape, NEG_BIG, F32)

        def key_operand(start, extra_lane):
            k_tile = k_ref[pl.ds(start, blk), :]
            if not fast:
                return lambda p: k_tile[:, lanes(p)]
            sel_lanes = (lane == 0) if extra_lane is None else ((lane == 0) | (lane == extra_lane))
            extra = jnp.where(sel_lanes, 1.0, 0.0).astype(BF16)
            return lambda p: jnp.concatenate([k_tile[:, lanes(p)], extra], axis=1)

        def query_operand(p):
            return qop_ref[p] if fast else qop_ref[p, :LANES]

        def update(vt_tile):
            if fast:
                return lambda s, u, p: _offset_update(s, vt_tile(u, p), acc_ref, p)
            return lambda s, u, p: _online_update(s, vt_tile(u, p), m_ref, acc_ref, p)

        own_keys = key_operand(own_start, None)
        _flash_tiles(
            [(0, p) for p in range(N_PAIRS)],
            lambda u, p: jnp.dot(own_keys(p), query_operand(p), preferred_element_type=F32),
            lambda s, u, p: jnp.where(causal, s, -jnp.inf),
            update(lambda u, p: vt_ref[lanes(p), pl.ds(own_start, blk)]))

        def body(c, carry, unroll, first):
            js = [first + c * unroll + u for u in range(unroll)]
            starts = [pl.multiple_of(j * blk, blk) for j in js]
            keys = [key_operand(starts[u], ONES_ROWS + js[u]) for u in range(unroll)]
            _flash_tiles(
                [(u, p) for u in range(unroll) for p in range(N_PAIRS)],
                lambda u, p: jnp.dot(keys[u](p), query_operand(p), preferred_element_type=F32),
                (lambda s, u, p: s) if fast else
                (lambda s, u, p: jnp.where(sel_ref[p, pl.ds(js[u], 1), :] > 0.5, s, -jnp.inf)),
                update(lambda u, p: vt_ref[lanes(p), pl.ds(starts[u], blk)]))
            return carry

        done = 0
        for unroll in (LONG_UNROLLS if fast else ()):
            n_long = (i - done) // unroll
            lax.fori_loop(0, n_long, functools.partial(body, unroll=unroll, first=done), 0)
            done = done + n_long * unroll
        lax.fori_loop(0, (i - done + MOBA_UNROLL - 1) // MOBA_UNROLL,
                      functools.partial(body, unroll=MOBA_UNROLL, first=done), 0)

    attend(True)

    @pl.when(jnp.logical_not(_denominator_ok(acc_ref)))
    def _():
        attend(False)

    _flash_finish(o_ref, ot_ref, acc_ref, tq)


def _moba(qt_all, k_all, vt_all, batch, seq, q_row_blk, k_col_blk, v_row_blk):
    tq = MOBA_BLOCK
    nq = seq // tq
    nb = seq // MOBA_BLOCK
    topk = min(MOBA_TOPK, nb - 1)
    w = BRANCH_WIDTH
    return pl.pallas_call(
        functools.partial(_moba_kernel, seq=seq, topk=topk),
        grid=(batch, nq),
        in_specs=[
            pl.BlockSpec((w, tq), lambda b, i: (q_row_blk, b * nq + i)),
            pl.BlockSpec((seq, w), lambda b, i: (b, k_col_blk), pipeline_mode=pl.Buffered(1)),
            pl.BlockSpec((w, seq), lambda b, i: (v_row_blk, b), pipeline_mode=pl.Buffered(1)),
        ],
        out_specs=pl.BlockSpec((tq, w), lambda b, i: (b * nq + i, 0)),
        out_shape=jax.ShapeDtypeStruct((batch * seq, w), F32),
        scratch_shapes=[pltpu.VMEM((nb, w), BF16),
                        pltpu.VMEM((N_PAIRS, 1, 2 * tq), F32),
                        pltpu.VMEM((N_PAIRS, 2 * LANES, 2 * tq), BF16),
                        pltpu.VMEM((N_PAIRS, nb, 2 * tq), F32),
                        pltpu.VMEM((N_PAIRS, 1, 2 * tq), F32),
                        pltpu.VMEM((N_PAIRS, 2, HEAD_DIM + ONES_ROWS, tq), F32),
                        pltpu.VMEM((w, tq), F32)],
        compiler_params=_cparams(2),
        name="moba",
    )(qt_all, k_all, vt_all)


def _ordered_key(bits):
    return jnp.where(bits < 0, jnp.int32(INT_MIN) - bits, bits)


def _tree_sum(parts):
    while len(parts) > 1:
        parts = [parts[a] + parts[a + 1] for a in range(0, len(parts) - 1, 2)] + (
            [parts[-1]] if len(parts) % 2 else [])
    return parts[0]


def _dsa_kernel(qit_ref, ki_ref, wt_ref, qt_ref, k_ref, vt_ref, o_ref,
                score_ref, hi_ref, lo_ref, kb_ref, qop_ref, m_ref, acc_ref, ot_ref,
                *, tq, kc, kcb, n_sel):
    i = pl.program_id(1)
    nch = (i * tq + tq) // kc
    nchb = (i * tq + tq + kcb - 1) // kcb
    krow = lax.broadcasted_iota(jnp.int32, (kc, tq), 0)
    qpos = i * tq + lax.broadcasted_iota(jnp.int32, (kc, tq), 1)
    n_sel_f = float(n_sel)

    @pl.when(i == 0)
    def _():
        _key_norm_bound(k_ref, kb_ref, k_ref.shape[0], tq)

    for p in range(N_PAIRS):
        qop_ref[p, :LANES] = _pair_operand(qit_ref, p)
    w_rows = [wt_ref[h:h + 1, :] * (N_HEADS ** -0.5) for h in range(N_HEADS)]

    def score_body(c, carry, masked, unroll, first):
        starts = [pl.multiple_of((first + c * unroll + u) * kc, kc) for u in range(unroll)]
        accs = [jnp.zeros((kc, tq), F32) for _ in range(unroll)]

        def head_scores(u, p):
            return jnp.dot(ki_ref[pl.ds(starts[u], kc), :], qop_ref[p, :LANES], preferred_element_type=F32)

        def accumulate(r, u, p):
            accs[u] = accs[u] + jnp.maximum(r[:, :tq], 0.0) * w_rows[2 * p]
            accs[u] = accs[u] + jnp.maximum(r[:, tq:], 0.0) * w_rows[2 * p + 1]
            if p < N_PAIRS - 1:
                return
            sc = jnp.where(starts[u] + krow <= qpos, accs[u], -jnp.inf) if masked else accs[u]
            score_ref[pl.ds(starts[u], kc), :] = sc
            key = _ordered_key(lax.bitcast_convert_type(sc, jnp.int32))
            hi_ref[pl.ds(starts[u], kc), :] = jnp.right_shift(key, 16).astype(jnp.int16)
            lo_ref[pl.ds(starts[u], kc), :] = (key ^ jnp.int32(0x8000)).astype(jnp.int16)

        _flash_tiles([(u, p) for u in range(unroll) for p in range(N_PAIRS)],
                     head_scores, lambda r, u, p: r, accumulate)
        return carry

    n_full = (i * tq + 1) // kc
    done = 0
    for unroll in SCORE_UNROLLS + (1,):
        n_trips = (n_full - done) // unroll
        lax.fori_loop(0, n_trips, functools.partial(score_body, masked=False, unroll=unroll, first=done), 0)
        done = done + n_trips * unroll
    lax.fori_loop(0, nch - n_full, functools.partial(score_body, masked=True, unroll=1, first=n_full), 0)

    @pl.when(nch * kc < nchb * kcb)
    def _():
        pad_rows = pl.ds(pl.multiple_of(nch * kc, kc), kcb - kc)
        score_ref[pad_rows, :] = jnp.full((kcb - kc, tq), -jnp.inf, F32)
        hi_ref[pad_rows, :] = jnp.full((kcb - kc, tq), I16_MIN, jnp.int16)
        lo_ref[pad_rows, :] = jnp.full((kcb - kc, tq), I16_MIN, jnp.int16)

    def count16(ref, pred):
        n_acc = 4
        rows = 2 * SUBLANES

        def body(c, accs):
            x = ref[pl.ds(pl.multiple_of(c * kcb, kcb), kcb), :]
            accs = list(accs)
            for r in range(kcb // rows):
                hit = jnp.where(pred(x[r * rows:(r + 1) * rows]), jnp.int16(1), jnp.int16(0))
                accs[r % n_acc] = accs[r % n_acc] + hit
            return tuple(accs)
        accs = lax.fori_loop(0, nchb, body, tuple(jnp.zeros((rows, tq), jnp.int16) for _ in range(n_acc)))
        tot = _tree_sum(list(accs)).astype(jnp.int32).astype(F32)
        return jnp.sum(tot, axis=0, keepdims=True)

    def bisect16(ref, target, count_at_min):
        def body(it, carry):
            t, c_ge, c_gt = carry
            cand = t + jnp.left_shift(jnp.int32(1), 15 - it)
            cand16 = cand.astype(jnp.int16)
            cnt = count16(ref, lambda x: x >= cand16)
            ok = cnt >= target
            return jnp.where(ok, cand, t), jnp.where(ok, cnt, c_ge), jnp.where(ok, c_gt, cnt)
        start = jnp.full((1, tq), I16_MIN, jnp.int32)
        return lax.fori_loop(0, 16, body, (start, count_at_min, jnp.zeros((1, tq), F32)))

    rows_seen = jnp.full((1, tq), (nchb * kcb).astype(F32))
    t_hi, c_ge_hi, c_gt_hi = bisect16(hi_ref, n_sel_f, rows_seen)
    t_hi16 = t_hi.astype(jnp.int16)

    def low_body(c, carry):
        rows = pl.ds(pl.multiple_of(c * kcb, kcb), kcb)
        lo_ref[rows, :] = jnp.where(hi_ref[rows, :] == t_hi16, lo_ref[rows, :], jnp.int16(I16_MIN))
        return carry

    lax.fori_loop(0, nchb, low_body, 0)
    t_lo, c_ge_lo, _ = bisect16(lo_ref, n_sel_f - c_gt_hi, c_ge_hi - c_gt_hi)
    n_ge = c_gt_hi + c_ge_lo
    key_t = t_hi * 65536 + (t_lo + 32768)
    short = key_t <= KEY_NEG_INF
    thr = jnp.where(short, -FLT_MAX, lax.bitcast_convert_type(_ordered_key(key_t), F32))
    has_tie = jnp.max(jnp.where(short, 0.0, n_ge - n_sel_f)) > 0.0

    @pl.when(has_tie)
    def _():
        lower = (lax.broadcasted_iota(jnp.int32, (kc, kc), 1)
                 < lax.broadcasted_iota(jnp.int32, (kc, kc), 0))
        lower = jnp.where(lower, 1.0, 0.0).astype(BF16)

        def gt_body(c, cnt):
            sc = score_ref[pl.ds(pl.multiple_of(c * kc, kc), kc), :]
            return cnt + jnp.sum(jnp.where(sc > thr, 1.0, 0.0), axis=0, keepdims=True)
        n_gt = lax.fori_loop(0, nch, gt_body, jnp.zeros((1, tq), F32))
        need = n_sel_f - n_gt

        def body(c, seen):
            start = pl.multiple_of(c * kc, kc)
            sc = score_ref[pl.ds(start, kc), :]
            eq = jnp.where(sc == thr, 1.0, 0.0)
            rank = seen + jnp.dot(lower, eq.astype(BF16), preferred_element_type=F32)
            keep = (sc > thr) | ((sc == thr) & (rank < need))
            keep = keep & (start + krow <= qpos)
            score_ref[pl.ds(start, kc), :] = jnp.where(keep, jnp.inf, -jnp.inf)
            return seen + jnp.sum(eq, axis=0, keepdims=True)
        lax.fori_loop(0, nch, body, jnp.zeros((1, tq), F32))

    for p in range(N_PAIRS):
        qop = _pair_operand(qt_ref, p)
        qop_ref[p, :LANES] = qop
        qop_ref[p, LANES:LANES + ONES_ROWS] = _offset_rows(qop, kb_ref[p])
        qop_ref[p, LANES + ONES_ROWS:] = jnp.zeros((LANES - ONES_ROWS, 2 * tq), BF16)

    def lanes(p):
        return slice(p * LANES, (p + 1) * LANES)

    def attend(fast):
        acc_ref[...] = jnp.zeros(acc_ref.shape, F32)
        if not fast:
            m_ref[...] = jnp.full(m_ref.shape, NEG_BIG, F32)
        ones = jnp.ones((kc, LANES), BF16)

        def att_body(c, carry, unroll, first):
            starts = [pl.multiple_of((first + c * unroll + u) * kc, kc) for u in range(unroll)]

            def qk(u, p):
                k_tile = k_ref[pl.ds(starts[u], kc), lanes(p)]
                if fast:
                    return jnp.dot(jnp.concatenate([k_tile, ones], axis=1), qop_ref[p],
                                   preferred_element_type=F32)
                return jnp.dot(k_tile, qop_ref[p, :LANES], preferred_element_type=F32)

            def add_bias(s, u, p):
                bias = jnp.where(score_ref[pl.ds(starts[u], kc), :] >= thr, 0.0, NEG_BIG)
                return jnp.concatenate([s[:, :tq] + bias, s[:, tq:] + bias], axis=1)

            def update(s, u, p):
                vt_tile = vt_ref[lanes(p), pl.ds(starts[u], kc)]
                if fast:
                    _offset_update(s, vt_tile, acc_ref, p)
                else:
                    _online_update(s, vt_tile, m_ref, acc_ref, p)

            _flash_tiles([(u, p) for u in range(unroll) for p in range(N_PAIRS)], qk, add_bias, update)
            return carry

        done = 0
        for unroll in (LONG_UNROLLS if fast else ()):
            n_long = (nch - done) // unroll
            lax.fori_loop(0, n_long, functools.partial(att_body, unroll=unroll, first=done), 0)
            done = done + n_long * unroll
        lax.fori_loop(0, (nch - done + DSA_UNROLL - 1) // DSA_UNROLL,
                      functools.partial(att_body, unroll=DSA_UNROLL, first=done), 0)

    attend(True)

    @pl.when(jnp.logical_not(_denominator_ok(acc_ref)))
    def _():
        attend(False)

    _flash_finish(o_ref, ot_ref, acc_ref, tq)


def _dsa(qt_all, k_all, vt_all, wt, batch, seq, tq, kc, kcb):
    nq = seq // tq
    w = BRANCH_WIDTH
    n_sel = min(IDX_TOPK_MAX, seq // 4)
    assert tq % kc == 0 and kcb % kc == 0 and kcb > kc and seq % kcb == 0
    ki_col_blk = (2 * w) // LANES
    return pl.pallas_call(
        functools.partial(_dsa_kernel, tq=tq, kc=kc, kcb=kcb, n_sel=n_sel),
        grid=(batch, nq),
        in_specs=[
            pl.BlockSpec((w, tq), lambda b, i: (2, b * nq + i)),
            pl.BlockSpec((seq, LANES), lambda b, i: (b, ki_col_blk), pipeline_mode=pl.Buffered(1)),
            pl.BlockSpec((2 * SUBLANES, tq), lambda b, i: (0, b * nq + i)),
            pl.BlockSpec((w, tq), lambda b, i: (1, b * nq + i)),
            pl.BlockSpec((seq, w), lambda b, i: (b, 1), pipeline_mode=pl.Buffered(1)),
            pl.BlockSpec((w, seq), lambda b, i: (1, b), pipeline_mode=pl.Buffered(1)),
        ],
        out_specs=pl.BlockSpec((tq, w), lambda b, i: (b * nq + i, 0)),
        out_shape=jax.ShapeDtypeStruct((batch * seq, w), F32),
        scratch_shapes=[pltpu.VMEM((seq, tq), F32),
                        pltpu.VMEM((seq, tq), jnp.int16),
                        pltpu.VMEM((seq, tq), jnp.int16),
                        pltpu.VMEM((N_PAIRS, 1, 2 * tq), F32),
                        pltpu.VMEM((N_PAIRS, 2 * LANES, 2 * tq), BF16),
                        pltpu.VMEM((N_PAIRS, 1, 2 * tq), F32),
                        pltpu.VMEM((N_PAIRS, 2, HEAD_DIM + ONES_ROWS, tq), F32),
                        pltpu.VMEM((w, tq), F32)],
        compiler_params=_cparams(2),
        name="dsa",
    )(qt_all, k_all, wt, qt_all, k_all, vt_all)


def _merge_kernel(ya_ref, yb_ref, h_ref, kv_ref, wmq_ref, wg_ref, wmix_ref, wb_ref, wo_ref, x_ref, gn_ref,
                  *out_refs, last):
    w = BRANCH_WIDTH
    d = x_ref.shape[1]
    h = h_ref[...]
    mq = jnp.dot(h, wmq_ref[...], preferred_element_type=F32).astype(BF16)
    ym_heads = []
    for hd in range(M_HEADS):
        cols = slice(hd * M_HEAD_DIM, (hd + 1) * M_HEAD_DIM)
        km = kv_ref[:, cols]
        vm = kv_ref[:, w + hd * M_HEAD_DIM:w + (hd + 1) * M_HEAD_DIM]
        sc = lax.dot_general(mq[:, cols], km, (((1,), (1,)), ((), ())),
                             preferred_element_type=F32) * (M_HEAD_DIM ** -0.5)
        sc = sc - jnp.max(sc, axis=-1, keepdims=True)
        e = jnp.exp(sc)
        p = e / jnp.sum(e, axis=-1, keepdims=True)
        ym_heads.append(jnp.dot(p.astype(BF16), vm, preferred_element_type=F32))
    ym = jnp.concatenate(ym_heads, axis=-1)
    ys = (ya_ref[...], yb_ref[...], ym)
    merged = None
    for n in range(3):
        g = jnp.dot(h, wg_ref[:, n * w:(n + 1) * w], preferred_element_type=F32)
        y = (ys[n] * (g * jax.nn.sigmoid(g))).astype(BF16)
        up = jnp.dot(y, wb_ref[n], preferred_element_type=F32)
        mix = jax.nn.sigmoid(jnp.dot(h, wmix_ref[:, n * d:(n + 1) * d], preferred_element_type=F32))
        term = mix * up
        merged = term if merged is None else merged + term
    x_new = x_ref[...] + jnp.dot(merged.astype(BF16), wo_ref[...], preferred_element_type=F32)
    y = x_new * lax.rsqrt(jnp.mean(x_new * x_new, axis=-1, keepdims=True) + RMS_EPS) * gn_ref[...]
    if last:
        out_refs[0][...] = y
    else:
        out_refs[0][...] = x_new
        out_refs[1][...] = y.astype(BF16)


def _merge(ya, yb, h, kv, w_mq, w_g, w_mix, wb, wo, x2d, g_next, last, seq, n_mem, tm):
    m, d = x2d.shape
    w = BRANCH_WIDTH
    nt = seq // tm

    def resident(shape):
        return pl.BlockSpec(shape, lambda i: (0,) * len(shape), pipeline_mode=pl.Buffered(1))

    row_tile = pl.BlockSpec((tm, d), lambda i: (i, 0))
    out_f32 = jax.ShapeDtypeStruct((m, d), F32)
    return pl.pallas_call(
        functools.partial(_merge_kernel, last=last),
        grid=(m // tm,),
        in_specs=[
            pl.BlockSpec((tm, w), lambda i: (i, 0)),
            pl.BlockSpec((tm, w), lambda i: (i, 0)),
            pl.BlockSpec((tm, d), lambda i: (i, 0)),
            pl.BlockSpec((n_mem, 2 * w), lambda i: (i // nt, 0)),
            resident((d, w)),
            resident((d, 3 * w)),
            resident((d, 3 * d)),
            resident((3, w, d)),
            resident((d, d)),
            row_tile,
            resident((1, d)),
        ],
        out_specs=row_tile if last else (row_tile, row_tile),
        out_shape=out_f32 if last else (out_f32, jax.ShapeDtypeStruct((m, d), BF16)),
        compiler_params=_cparams(1),
        name="merge",
    )(ya, yb, h, kv, w_mq, w_g, w_mix, wb, wo, x2d, g_next.reshape(1, d))


def _rope_tables(seq):
    inv_freq = ROPE_THETA ** (-jnp.arange(ROT_HALF, dtype=F32) / ROT_HALF)
    ang = jnp.arange(seq, dtype=jnp.int32).astype(F32)[:, None] * inv_freq[None, :]
    cos, sin = jnp.cos(ang), jnp.sin(ang)
    r = np.arange(LANES) % HEAD_DIM
    f = r % ROT_HALF
    first = jnp.asarray(r < ROT_HALF)
    second = jnp.asarray((r >= ROT_HALF) & (r < 2 * ROT_HALF))
    rot = jnp.asarray(r < 2 * ROT_HALF)
    c_tok = jnp.where(rot[None, :], cos[:, f], 1.0)
    s1_tok = jnp.where(first[None, :], -sin[:, f], 0.0)
    s2_tok = jnp.where(second[None, :], sin[:, f], 0.0)
    return (c_tok, s1_tok, s2_tok), (cos.T, sin.T)


def kernel(x, mem, norm_g, w_in, mem_norm_g, w_mem_kv, w_branch, w_out, final_g):
    batch, seq, d = x.shape
    n_mem = mem.shape[1]
    depth = norm_g.shape[0]
    w = BRANCH_WIDTH
    m = batch * seq
    assert seq % MOBA_BLOCK == 0 and d % LANES == 0

    tm = min(1024, seq)
    tm_merge = min(512, seq)
    dsa_tq = 256
    dsa_kc = 256
    dsa_kcb = min(512, seq)

    tok_tabs, feat_tabs = _rope_tables(seq)
    offs = np.cumsum([0, w, w, w, w, w, w, w, w, N_HEADS * HEAD_DIM, HEAD_DIM, N_HEADS, w, w, 3 * d])
    (o_aq, o_ak, o_av, o_ag, o_bq, o_bk, o_bv, o_bg, o_iq, o_ik, o_iw, o_mq, o_mg, o_mix, o_end) = offs
    attn_scale = HEAD_DIM ** -0.5 * float(np.log2(np.e))
    idx_scale = HEAD_DIM ** -0.5

    x2d = x.reshape(m, d)
    mem2d = mem.reshape(batch * n_mem, d)
    h = _rmsnorm(x2d, norm_g[0], BF16, tm)
    for l in range(depth):
        col = lambda a, b: lax.slice(w_in, (l, 0, a), (l + 1, d, b)).reshape(d, b - a)
        w_k = jnp.concatenate([col(o_ak, o_av), col(o_bk, o_bv), col(o_ik, o_iw), col(o_ik, o_iw)],
                              axis=1).astype(BF16)
        w_qt = jnp.concatenate([col(o_aq, o_ak) * attn_scale, col(o_bq, o_bk) * attn_scale,
                                col(o_iq, o_ik) * idx_scale], axis=1).T.astype(BF16)
        w_vt = jnp.concatenate([col(o_av, o_ag), col(o_bv, o_bg)], axis=1).T.astype(BF16)
        w_iwt = jnp.pad(col(o_iw, o_mq).T, ((0, 2 * SUBLANES - N_HEADS), (0, 0))).astype(BF16)
        w_mq = col(o_mq, o_mg).astype(BF16)
        w_g = jnp.concatenate([col(o_ag, o_bq), col(o_bg, o_iq), col(o_mg, o_mix)], axis=1).astype(BF16)
        w_mix = col(o_mix, o_end).astype(BF16)

        k_all, qt_all, vt_all, iwt = _in_proj(h, w_k, w_qt, w_vt, w_iwt, tok_tabs, feat_tabs, seq, tm)

        ya = _moba(qt_all, k_all, vt_all, batch, seq, 0, 0, 0)
        yb = _dsa(qt_all, k_all, vt_all, iwt, batch, seq, dsa_tq, dsa_kc, dsa_kcb)

        mn = _rmsnorm(mem2d, mem_norm_g[l], BF16, n_mem)
        kv = _mm_tok(mn, w_mem_kv[l].astype(BF16), BF16, n_mem, 2 * w)

        last = l == depth - 1
        res = _merge(ya, yb, h, kv, w_mq, w_g, w_mix, w_branch[l].astype(BF16), w_out[l].astype(BF16),
                     x2d, final_g if last else norm_g[l + 1], last, seq, n_mem, tm_merge)
        if not last:
            x2d, h = res

    return res.reshape(batch, seq, d)
```

```python
import functools

import jax
import jax.numpy as jnp
import numpy as np
from jax import lax
from jax.experimental import pallas as pl
from jax.experimental.pallas import tpu as pltpu

F32 = jnp.float32
BF16 = jnp.bfloat16

HEAD_DIM = 64
N_HEADS = 8
BRANCH_WIDTH = 512
M_HEADS = 4
M_HEAD_DIM = 128
MOBA_BLOCK = 256
MOBA_TOPK = 3
IDX_TOPK_MAX = 256
ROPE_THETA = 500000.0
ROT_HALF = HEAD_DIM // 4 // 2
RMS_EPS = 1e-6

LANES = 128
SUBLANES = 8
VMEM_LIMIT = 56 * 1024 * 1024
NEG_BIG = -1e30
FLT_MAX = float(np.finfo(np.float32).max)
INT_MIN = -(2 ** 31)
I16_MIN = -(2 ** 15)
KEY_NEG_INF = -0x7F800000


def _cparams(n_axes):
    return pltpu.CompilerParams(dimension_semantics=("arbitrary",) * n_axes,
                                vmem_limit_bytes=VMEM_LIMIT)


def _rmsnorm_kernel(x_ref, g_ref, o_ref):
    xf = x_ref[...]
    y = xf * lax.rsqrt(jnp.mean(xf * xf, axis=-1, keepdims=True) + RMS_EPS)
    o_ref[...] = (y * g_ref[...]).astype(o_ref.dtype)


def _rmsnorm(x2d, g, out_dtype, tm):
    m, d = x2d.shape
    return pl.pallas_call(
        _rmsnorm_kernel,
        grid=(m // tm,),
        in_specs=[pl.BlockSpec((tm, d), lambda i: (i, 0)),
                  pl.BlockSpec((1, d), lambda i: (0, 0))],
        out_specs=pl.BlockSpec((tm, d), lambda i: (i, 0)),
        out_shape=jax.ShapeDtypeStruct((m, d), out_dtype),
        compiler_params=_cparams(1),
        name="rmsnorm",
    )(x2d, g.reshape(1, d))


def _mem_kv_kernel(mem_ref, g_ref, w_ref, o_ref):
    xf = mem_ref[...]
    y = xf * lax.rsqrt(jnp.mean(xf * xf, axis=-1, keepdims=True) + RMS_EPS)
    mn = (y * g_ref[0]).astype(BF16)
    o_ref[0] = jnp.dot(mn, w_ref[0], preferred_element_type=F32).astype(o_ref.dtype)


def _mem_kv(mem2d, g_all, w_all):
    depth, d, n = w_all.shape
    m = mem2d.shape[0]
    return pl.pallas_call(
        _mem_kv_kernel,
        grid=(depth,),
        in_specs=[pl.BlockSpec((m, d), lambda l: (0, 0)),
                  pl.BlockSpec((1, 1, d), lambda l: (l, 0, 0)),
                  pl.BlockSpec((1, d, n), lambda l: (l, 0, 0))],
        out_specs=pl.BlockSpec((1, m, n), lambda l: (l, 0, 0)),
        out_shape=jax.ShapeDtypeStruct((depth, m, n), BF16),
        compiler_params=_cparams(1),
        name="mem_kv",
    )(mem2d, g_all.reshape(depth, 1, d), w_all)


def _in_proj_kernel(h_ref, wk_ref, wqt_ref, wvt_ref, wiw_ref, c_ref, s1_ref, s2_ref, cos_ref, sin_ref,
                    k_ref, qt_ref, vt_ref, iw_ref):
    h = h_ref[...]
    nt_dims = (((1,), (1,)), ((), ()))

    k_acc = jnp.dot(h, wk_ref[...], preferred_element_type=F32)
    c, s1, s2 = c_ref[...], s1_ref[...], s2_ref[...]
    for j in range(k_acc.shape[1] // LANES):
        piece = k_acc[:, j * LANES:(j + 1) * LANES]
        up = pltpu.roll(piece, LANES - ROT_HALF, 1)
        down = pltpu.roll(piece, ROT_HALF, 1)
        k_ref[:, j * LANES:(j + 1) * LANES] = (piece * c + up * s1 + down * s2).astype(k_ref.dtype)

    cos, sin = cos_ref[...], sin_ref[...]
    w = BRANCH_WIDTH
    for blk in range(wqt_ref.shape[0] // w):
        acc = lax.dot_general(wqt_ref[blk * w:(blk + 1) * w, :], h, nt_dims, preferred_element_type=F32)
        pieces = []
        for hh in range(w // HEAD_DIM):
            base = hh * HEAD_DIM
            x1 = acc[base:base + ROT_HALF]
            x2 = acc[base + ROT_HALF:base + 2 * ROT_HALF]
            pieces += [x1 * cos - x2 * sin, x2 * cos + x1 * sin, acc[base + 2 * ROT_HALF:base + HEAD_DIM]]
        qt_ref[blk * w:(blk + 1) * w, :] = jnp.concatenate(pieces, axis=0).astype(qt_ref.dtype)

    vt_ref[...] = lax.dot_general(wvt_ref[...], h, nt_dims, preferred_element_type=F32).astype(vt_ref.dtype)
    iw_ref[...] = lax.dot_general(wiw_ref[...], h, nt_dims, preferred_element_type=F32)


def _in_proj(h, w_k, w_qt, w_vt, w_iwt, tok_tabs, feat_tabs, seq, tm):
    m, k = h.shape
    nt = seq // tm

    def resident(shape):
        return pl.BlockSpec(shape, lambda i: (0,) * len(shape), pipeline_mode=pl.Buffered(1))

    tok_tab = pl.BlockSpec((tm, LANES), lambda i: (i % nt, 0))
    feat_tab = pl.BlockSpec((SUBLANES, tm), lambda i: (0, i % nt))
    n_k, n_q, n_v, n_w = w_k.shape[1], w_qt.shape[0], w_vt.shape[0], w_iwt.shape[0]
    return pl.pallas_call(
        _in_proj_kernel,
        grid=(m // tm,),
        in_specs=[pl.BlockSpec((tm, k), lambda i: (i, 0)),
                  resident(w_k.shape), resident(w_qt.shape), resident(w_vt.shape), resident(w_iwt.shape),
                  tok_tab, tok_tab, tok_tab, feat_tab, feat_tab],
        out_specs=(pl.BlockSpec((tm, n_k), lambda i: (i, 0)),
                   pl.BlockSpec((n_q, tm), lambda i: (0, i)),
                   pl.BlockSpec((n_v, tm), lambda i: (0, i)),
                   pl.BlockSpec((n_w, tm), lambda i: (0, i))),
        out_shape=(jax.ShapeDtypeStruct((m, n_k), BF16),
                   jax.ShapeDtypeStruct((n_q, m), BF16),
                   jax.ShapeDtypeStruct((n_v, m), BF16),
                   jax.ShapeDtypeStruct((n_w, m), F32)),
        compiler_params=_cparams(1),
        name="in_proj",
    )(h, w_k, w_qt, w_vt, w_iwt, *tok_tabs, *feat_tabs)


N_PAIRS = N_HEADS // 2


def _pair_operand(qt_ref, pair):
    qp = qt_ref[pair * LANES:(pair + 1) * LANES, :].astype(F32)
    row = lax.broadcasted_iota(jnp.int32, qp.shape, 0)
    lo = jnp.where(row < HEAD_DIM, qp, 0.0)
    hi = jnp.where(row >= HEAD_DIM, qp, 0.0)
    return jnp.concatenate([lo, hi], axis=1).astype(BF16)


ONES_ROWS = 16
QK_LOOKAHEAD = 4


OFFSET_SLACK = 1.0 + 2.0 ** -5
L_FLOOR = 2.0 ** -80


def _pv_accumulate(pm, vt_tile, acc_ref, p, scale=None):
    keys = vt_tile.shape[1]
    tq = pm.shape[1] // 2
    ones = jnp.ones((ONES_ROWS, keys), BF16)
    for e in range(2):
        vt_ext = jnp.concatenate([vt_tile[e * HEAD_DIM:(e + 1) * HEAD_DIM], ones], axis=0)
        cols = slice(e * tq, (e + 1) * tq)
        upd = jnp.dot(vt_ext, pm[:, cols], preferred_element_type=F32)
        old = acc_ref[p, e] if scale is None else scale[:, cols] * acc_ref[p, e]
        acc_ref[p, e] = old + upd


def _online_update(s, vt_tile, m_ref, acc_ref, p):
    m = m_ref[p]
    m_new = jnp.maximum(m, jnp.max(s, axis=0, keepdims=True))
    _pv_accumulate(jnp.exp2(s - m_new).astype(BF16), vt_tile, acc_ref, p, scale=jnp.exp2(m - m_new))
    m_ref[p] = m_new


def _offset_update(s, vt_tile, acc_ref, p):
    _pv_accumulate(jnp.exp2(s).astype(BF16), vt_tile, acc_ref, p)


def _flash_tiles(tiles, qk, mask, update):
    ss = [qk(*t) for t in tiles[:QK_LOOKAHEAD]]
    for n, (u, p) in enumerate(tiles):
        if n + QK_LOOKAHEAD < len(tiles):
            ss.append(qk(*tiles[n + QK_LOOKAHEAD]))
        update(mask(ss[n], u, p), u, p)
        ss[n] = None


def _key_norm_bound(k_ref, kb_ref, seq, tq):
    chunk = min(512, seq)
    width = k_ref.shape[1]
    lane = lax.broadcasted_iota(jnp.int32, (width, LANES), 0)
    head = lax.broadcasted_iota(jnp.int32, (width, LANES), 1)
    group = jnp.where((lane >= head * HEAD_DIM) & (lane < (head + 1) * HEAD_DIM), 1.0, 0.0).astype(BF16)

    def body(c, mx):
        kk = k_ref[pl.ds(pl.multiple_of(c * chunk, chunk), chunk), :].astype(F32)
        n2 = jnp.dot((kk * kk).astype(BF16), group, preferred_element_type=F32)
        return jnp.maximum(mx, jnp.max(n2, axis=0, keepdims=True))
    mx = lax.fori_loop(0, seq // chunk, body, jnp.zeros((1, LANES), F32))
    for p in range(N_PAIRS):
        kb_ref[p] = jnp.concatenate([jnp.broadcast_to(mx[:, 2 * p:2 * p + 1], (1, tq)),
                                     jnp.broadcast_to(mx[:, 2 * p + 1:2 * p + 2], (1, tq))], axis=1)


def _offset_rows(qop, kb2):
    qf = qop.astype(F32)
    bound = jnp.sqrt(jnp.sum(qf * qf, axis=0, keepdims=True) * kb2) * OFFSET_SLACK
    row = lax.broadcasted_iota(jnp.int32, (ONES_ROWS, qop.shape[1]), 0)
    return jnp.where(row == 0, -bound, 0.0).astype(BF16)


def _denominator_ok(acc_ref):
    return jnp.min(acc_ref[:, :, HEAD_DIM:HEAD_DIM + 1, :]) > L_FLOOR


def _flash_finish(o_ref, ot_ref, acc_ref, tq):
    for p in range(N_PAIRS):
        for e in range(2):
            lo = p * LANES + e * HEAD_DIM
            ot_ref[lo:lo + HEAD_DIM, :] = acc_ref[p, e, :HEAD_DIM] / acc_ref[p, e, HEAD_DIM:HEAD_DIM + 1]
    o_ref[...] = ot_ref[...].T


MOBA_UNROLL = 1
DSA_UNROLL = 1
LONG_UNROLLS = (8, 4, 2)
SCORE_UNROLLS = (8, 4, 2)


def _moba_kernel(qt_ref, k_ref, vt_ref, o_ref, kmean_ref, kb_ref, qop_ref, sel_ref, m_ref, acc_ref,
                 ot_ref, *, seq, topk):
    i = pl.program_id(1)
    blk = MOBA_BLOCK
    nb = seq // blk
    tq = blk

    @pl.when(i == 0)
    def _():
        r = lax.broadcasted_iota(jnp.int32, (nb, seq), 0)
        c = lax.broadcasted_iota(jnp.int32, (nb, seq), 1)
        member = jnp.where((c >= r * blk) & (c < (r + 1) * blk), 1.0, 0.0).astype(BF16)
        ksum = jnp.dot(member, k_ref[...], preferred_element_type=F32)
        kmean_ref[...] = (ksum * (1.0 / blk)).astype(BF16)
        _key_norm_bound(k_ref, kb_ref, seq, tq)

    blk_id = lax.broadcasted_iota(jnp.int32, (nb, 2 * tq), 0)
    past = blk_id < i
    own_start = pl.multiple_of(i * blk, blk)

    for p in range(N_PAIRS):
        lanes = slice(p * LANES, (p + 1) * LANES)
        qop = _pair_operand(qt_ref, p)
        qop_ref[p, :LANES] = qop
        qop_ref[p, LANES:LANES + ONES_ROWS] = _offset_rows(qop, kb_ref[p])
        gate = jnp.dot(kmean_ref[:, lanes], qop, preferred_element_type=F32)
        gate = jnp.where(past, gate, -jnp.inf)
        sel = jnp.zeros((nb, 2 * tq), jnp.bool_)
        for _ in range(topk):
            mx = jnp.max(gate, axis=0, keepdims=True)
            first = jnp.min(jnp.where(gate == mx, blk_id, nb), axis=0, keepdims=True)
            hit = blk_id == first
            sel = sel | hit
            gate = jnp.where(hit, -jnp.inf, gate)
        sel_ref[p] = jnp.where(sel & past, 1.0, 0.0)
        qop_ref[p, LANES + ONES_ROWS:LANES + ONES_ROWS + nb] = jnp.where(sel & past, 0.0, NEG_BIG).astype(BF16)
        qop_ref[p, LANES + ONES_ROWS + nb:] = jnp.zeros((LANES - ONES_ROWS - nb, 2 * tq), BF16)

    def lanes(p):
        return slice(p * LANES, (p + 1) * LANES)

    krow = lax.broadcasted_iota(jnp.int32, (blk, 2 * tq), 0)
    qcol = lax.broadcasted_iota(jnp.int32, (blk, 2 * tq), 1)
    causal = krow <= jnp.where(qcol < tq, qcol, qcol - tq)
    lane = lax.broadcasted_iota(jnp.int32, (blk, LANES), 1)

    def attend(fast):
        acc_ref[...] = jnp.zeros(acc_ref.shape, F32)
        if not fast:
            m_ref[...] = jnp.full(m_ref.shape, NEG_BIG, F32)

        def key_operand(start, extra_lane):
            k_tile = k_ref[pl.ds(start, blk), :]
            if not fast:
                return lambda p: k_tile[:, lanes(p)]
            sel_lanes = (lane == 0) if extra_lane is None else ((lane == 0) | (lane == extra_lane))
            extra = jnp.where(sel_lanes, 1.0, 0.0).astype(BF16)
            return lambda p: jnp.concatenate([k_tile[:, lanes(p)], extra], axis=1)

        def query_operand(p):
            return qop_ref[p] if fast else qop_ref[p, :LANES]

        def update(vt_tile):
            if fast:
                return lambda s, u, p: _offset_update(s, vt_tile(u, p), acc_ref, p)
            return lambda s, u, p: _online_update(s, vt_tile(u, p), m_ref, acc_ref, p)

        own_keys = key_operand(own_start, None)
        _flash_tiles(
            [(0, p) for p in range(N_PAIRS)],
            lambda u, p: jnp.dot(own_keys(p), query_operand(p), preferred_element_type=F32),
            lambda s, u, p: jnp.where(causal, s, -jnp.inf),
            update(lambda u, p: vt_ref[lanes(p), pl.ds(own_start, blk)]))

        def body(c, carry, unroll, first):
            js = [first + c * unroll + u for u in range(unroll)]
            starts = [pl.multiple_of(j * blk, blk) for j in js]
            keys = [key_operand(starts[u], ONES_ROWS + js[u]) for u in range(unroll)]
            _flash_tiles(
                [(u, p) for u in range(unroll) for p in range(N_PAIRS)],
                lambda u, p: jnp.dot(keys[u](p), query_operand(p), preferred_element_type=F32),
                (lambda s, u, p: s) if fast else
                (lambda s, u, p: jnp.where(sel_ref[p, pl.ds(js[u], 1), :] > 0.5, s, -jnp.inf)),
                update(lambda u, p: vt_ref[lanes(p), pl.ds(starts[u], blk)]))
            return carry

        done = 0
        for unroll in (LONG_UNROLLS if fast else ()):
            n_long = (i - done) // unroll
            lax.fori_loop(0, n_long, functools.partial(body, unroll=unroll, first=done), 0)
            done = done + n_long * unroll
        lax.fori_loop(0, (i - done + MOBA_UNROLL - 1) // MOBA_UNROLL,
                      functools.partial(body, unroll=MOBA_UNROLL, first=done), 0)

    attend(True)

    @pl.when(jnp.logical_not(_denominator_ok(acc_ref)))
    def _():
        attend(False)

    _flash_finish(o_ref, ot_ref, acc_ref, tq)


def _moba(qt_all, k_all, vt_all, batch, seq, q_row_blk, k_col_blk, v_row_blk):
    tq = MOBA_BLOCK
    nq = seq // tq
    nb = seq // MOBA_BLOCK
    topk = min(MOBA_TOPK, nb - 1)
    w = BRANCH_WIDTH
    return pl.pallas_call(
        functools.partial(_moba_kernel, seq=seq, topk=topk),
        grid=(batch, nq),
        in_specs=[
            pl.BlockSpec((w, tq), lambda b, i: (q_row_blk, b * nq + i)),
            pl.BlockSpec((seq, w), lambda b, i: (b, k_col_blk), pipeline_mode=pl.Buffered(1)),
            pl.BlockSpec((w, seq), lambda b, i: (v_row_blk, b), pipeline_mode=pl.Buffered(1)),
        ],
        out_specs=pl.BlockSpec((tq, w), lambda b, i: (b * nq + i, 0)),
        out_shape=jax.ShapeDtypeStruct((batch * seq, w), F32),
        scratch_shapes=[pltpu.VMEM((nb, w), BF16),
                        pltpu.VMEM((N_PAIRS, 1, 2 * tq), F32),
                        pltpu.VMEM((N_PAIRS, 2 * LANES, 2 * tq), BF16),
                        pltpu.VMEM((N_PAIRS, nb, 2 * tq), F32),
                        pltpu.VMEM((N_PAIRS, 1, 2 * tq), F32),
                        pltpu.VMEM((N_PAIRS, 2, HEAD_DIM + ONES_ROWS, tq), F32),
                        pltpu.VMEM((w, tq), F32)],
        compiler_params=_cparams(2),
        name="moba",
    )(qt_all, k_all, vt_all)


def _ordered_key(bits):
    return jnp.where(bits < 0, jnp.int32(INT_MIN) - bits, bits)


def _tree_sum(parts):
    while len(parts) > 1:
        parts = [parts[a] + parts[a + 1] for a in range(0, len(parts) - 1, 2)] + (
            [parts[-1]] if len(parts) % 2 else [])
    return parts[0]


def _dsa_kernel(qit_ref, ki_ref, wt_ref, qt_ref, k_ref, vt_ref, o_ref,
                score_ref, hi_ref, lo_ref, kb_ref, qop_ref, m_ref, acc_ref, ot_ref,
                *, tq, kc, kcb, n_sel):
    i = pl.program_id(1)
    nch = (i * tq + tq) // kc
    nchb = (i * tq + tq + kcb - 1) // kcb
    krow = lax.broadcasted_iota(jnp.int32, (kc, tq), 0)
    qpos = i * tq + lax.broadcasted_iota(jnp.int32, (kc, tq), 1)
    n_sel_f = float(n_sel)

    @pl.when(i == 0)
    def _():
        _key_norm_bound(k_ref, kb_ref, k_ref.shape[0], tq)

    for p in range(N_PAIRS):
        qop_ref[p, :LANES] = _pair_operand(qit_ref, p)
    w_rows = [wt_ref[h:h + 1, :] * (N_HEADS ** -0.5) for h in range(N_HEADS)]

    def score_body(c, carry, masked, unroll, first):
        starts = [pl.multiple_of((first + c * unroll + u) * kc, kc) for u in range(unroll)]
        accs = [jnp.zeros((kc, tq), F32) for _ in range(unroll)]

        def head_scores(u, p):
            return jnp.dot(ki_ref[pl.ds(starts[u], kc), :], qop_ref[p, :LANES], preferred_element_type=F32)

        def accumulate(r, u, p):
            accs[u] = accs[u] + jnp.maximum(r[:, :tq], 0.0) * w_rows[2 * p]
            accs[u] = accs[u] + jnp.maximum(r[:, tq:], 0.0) * w_rows[2 * p + 1]
            if p < N_PAIRS - 1:
                return
            sc = jnp.where(starts[u] + krow <= qpos, accs[u], -jnp.inf) if masked else accs[u]
            score_ref[pl.ds(starts[u], kc), :] = sc
            key = _ordered_key(lax.bitcast_convert_type(sc, jnp.int32))
            hi_ref[pl.ds(starts[u], kc), :] = jnp.right_shift(key, 16).astype(jnp.int16)
            lo_ref[pl.ds(starts[u], kc), :] = (key ^ jnp.int32(0x8000)).astype(jnp.int16)

        _flash_tiles([(u, p) for u in range(unroll) for p in range(N_PAIRS)],
                     head_scores, lambda r, u, p: r, accumulate)
        return carry

    n_full = (i * tq + 1) // kc
    done = 0
    for unroll in SCORE_UNROLLS + (1,):
        n_trips = (n_full - done) // unroll
        lax.fori_loop(0, n_trips, functools.partial(score_body, masked=False, unroll=unroll, first=done), 0)
        done = done + n_trips * unroll
    lax.fori_loop(0, nch - n_full, functools.partial(score_body, masked=True, unroll=1, first=n_full), 0)

    @pl.when(nch * kc < nchb * kcb)
    def _():
        pad_rows = pl.ds(pl.multiple_of(nch * kc, kc), kcb - kc)
        score_ref[pad_rows, :] = jnp.full((kcb - kc, tq), -jnp.inf, F32)
        hi_ref[pad_rows, :] = jnp.full((kcb - kc, tq), I16_MIN, jnp.int16)
        lo_ref[pad_rows, :] = jnp.full((kcb - kc, tq), I16_MIN, jnp.int16)

    def count16(ref, pred):
        n_acc = 4
        rows = 2 * SUBLANES

        def body(c, accs):
            x = ref[pl.ds(pl.multiple_of(c * kcb, kcb), kcb), :]
            accs = list(accs)
            for r in range(kcb // rows):
                hit = jnp.where(pred(x[r * rows:(r + 1) * rows]), jnp.int16(1), jnp.int16(0))
                accs[r % n_acc] = accs[r % n_acc] + hit
            return tuple(accs)
        accs = lax.fori_loop(0, nchb, body, tuple(jnp.zeros((rows, tq), jnp.int16) for _ in range(n_acc)))
        tot = _tree_sum(list(accs)).astype(jnp.int32).astype(F32)
        return jnp.sum(tot, axis=0, keepdims=True)

    def bisect16(ref, target, count_at_min):
        def body(it, carry):
            t, c_ge, c_gt = carry
            cand = t + jnp.left_shift(jnp.int32(1), 15 - it)
            cand16 = cand.astype(jnp.int16)
            cnt = count16(ref, lambda x: x >= cand16)
            ok = cnt >= target
            return jnp.where(ok, cand, t), jnp.where(ok, cnt, c_ge), jnp.where(ok, c_gt, cnt)
        start = jnp.full((1, tq), I16_MIN, jnp.int32)
        return lax.fori_loop(0, 16, body, (start, count_at_min, jnp.zeros((1, tq), F32)))

    rows_seen = jnp.full((1, tq), (nchb * kcb).astype(F32))
    t_hi, c_ge_hi, c_gt_hi = bisect16(hi_ref, n_sel_f, rows_seen)
    t_hi16 = t_hi.astype(jnp.int16)

    def low_body(c, carry):
        rows = pl.ds(pl.multiple_of(c * kcb, kcb), kcb)
        lo_ref[rows, :] = jnp.where(hi_ref[rows, :] == t_hi16, lo_ref[rows, :], jnp.int16(I16_MIN))
        return carry

    lax.fori_loop(0, nchb, low_body, 0)
    t_lo, c_ge_lo, _ = bisect16(lo_ref, n_sel_f - c_gt_hi, c_ge_hi - c_gt_hi)
    n_ge = c_gt_hi + c_ge_lo
    key_t = t_hi * 65536 + (t_lo + 32768)
    short = key_t <= KEY_NEG_INF
    thr = jnp.where(short, -FLT_MAX, lax.bitcast_convert_type(_ordered_key(key_t), F32))
    has_tie = jnp.max(jnp.where(short, 0.0, n_ge - n_sel_f)) > 0.0

    @pl.when(has_tie)
    def _():
        lower = (lax.broadcasted_iota(jnp.int32, (kc, kc), 1)
                 < lax.broadcasted_iota(jnp.int32, (kc, kc), 0))
        lower = jnp.where(lower, 1.0, 0.0).astype(BF16)

        def gt_body(c, cnt):
            sc = score_ref[pl.ds(pl.multiple_of(c * kc, kc), kc), :]
            return cnt + jnp.sum(jnp.where(sc > thr, 1.0, 0.0), axis=0, keepdims=True)
        n_gt = lax.fori_loop(0, nch, gt_body, jnp.zeros((1, tq), F32))
        need = n_sel_f - n_gt

        def body(c, seen):
            start = pl.multiple_of(c * kc, kc)
            sc = score_ref[pl.ds(start, kc), :]
            eq = jnp.where(sc == thr, 1.0, 0.0)
            rank = seen + jnp.dot(lower, eq.astype(BF16), preferred_element_type=F32)
            keep = (sc > thr) | ((sc == thr) & (rank < need))
            keep = keep & (start + krow <= qpos)
            score_ref[pl.ds(start, kc), :] = jnp.where(keep, jnp.inf, -jnp.inf)
            return seen + jnp.sum(eq, axis=0, keepdims=True)
        lax.fori_loop(0, nch, body, jnp.zeros((1, tq), F32))

    for p in range(N_PAIRS):
        qop = _pair_operand(qt_ref, p)
        qop_ref[p, :LANES] = qop
        qop_ref[p, LANES:LANES + ONES_ROWS] = _offset_rows(qop, kb_ref[p])
        qop_ref[p, LANES + ONES_ROWS:] = jnp.zeros((LANES - ONES_ROWS, 2 * tq), BF16)

    def lanes(p):
        return slice(p * LANES, (p + 1) * LANES)

    def attend(fast):
        acc_ref[...] = jnp.zeros(acc_ref.shape, F32)
        if not fast:
            m_ref[...] = jnp.full(m_ref.shape, NEG_BIG, F32)
        ones = jnp.ones((kc, LANES), BF16)

        def att_body(c, carry, unroll, first):
            starts = [pl.multiple_of((first + c * unroll + u) * kc, kc) for u in range(unroll)]

            def qk(u, p):
                k_tile = k_ref[pl.ds(starts[u], kc), lanes(p)]
                if fast:
                    return jnp.dot(jnp.concatenate([k_tile, ones], axis=1), qop_ref[p],
                                   preferred_element_type=F32)
                return jnp.dot(k_tile, qop_ref[p, :LANES], preferred_element_type=F32)

            def add_bias(s, u, p):
                bias = jnp.where(score_ref[pl.ds(starts[u], kc), :] >= thr, 0.0, NEG_BIG)
                return jnp.concatenate([s[:, :tq] + bias, s[:, tq:] + bias], axis=1)

            def update(s, u, p):
                vt_tile = vt_ref[lanes(p), pl.ds(starts[u], kc)]
                if fast:
                    _offset_update(s, vt_tile, acc_ref, p)
                else:
                    _online_update(s, vt_tile, m_ref, acc_ref, p)

            _flash_tiles([(u, p) for u in range(unroll) for p in range(N_PAIRS)], qk, add_bias, update)
            return carry

        done = 0
        for unroll in (LONG_UNROLLS if fast else ()):
            n_long = (nch - done) // unroll
            lax.fori_loop(0, n_long, functools.partial(att_body, unroll=unroll, first=done), 0)
            done = done + n_long * unroll
        lax.fori_loop(0, (nch - done + DSA_UNROLL - 1) // DSA_UNROLL,
                      functools.partial(att_body, unroll=DSA_UNROLL, first=done), 0)

    attend(True)

    @pl.when(jnp.logical_not(_denominator_ok(acc_ref)))
    def _():
        attend(False)

    _flash_finish(o_ref, ot_ref, acc_ref, tq)


def _dsa(qt_all, k_all, vt_all, wt, batch, seq, tq, kc, kcb):
    nq = seq // tq
    w = BRANCH_WIDTH
    n_sel = min(IDX_TOPK_MAX, seq // 4)
    assert tq % kc == 0 and kcb % kc == 0 and kcb > kc and seq % kcb == 0
    ki_col_blk = (2 * w) // LANES
    return pl.pallas_call(
        functools.partial(_dsa_kernel, tq=tq, kc=kc, kcb=kcb, n_sel=n_sel),
        grid=(batch, nq),
        in_specs=[
            pl.BlockSpec((w, tq), lambda b, i: (2, b * nq + i)),
            pl.BlockSpec((seq, LANES), lambda b, i: (b, ki_col_blk), pipeline_mode=pl.Buffered(1)),
            pl.BlockSpec((2 * SUBLANES, tq), lambda b, i: (0, b * nq + i)),
            pl.BlockSpec((w, tq), lambda b, i: (1, b * nq + i)),
            pl.BlockSpec((seq, w), lambda b, i: (b, 1), pipeline_mode=pl.Buffered(1)),
            pl.BlockSpec((w, seq), lambda b, i: (1, b), pipeline_mode=pl.Buffered(1)),
        ],
        out_specs=pl.BlockSpec((tq, w), lambda b, i: (b * nq + i, 0)),
        out_shape=jax.ShapeDtypeStruct((batch * seq, w), F32),
        scratch_shapes=[pltpu.VMEM((seq, tq), F32),
                        pltpu.VMEM((seq, tq), jnp.int16),
                        pltpu.VMEM((seq, tq), jnp.int16),
                        pltpu.VMEM((N_PAIRS, 1, 2 * tq), F32),
                        pltpu.VMEM((N_PAIRS, 2 * LANES, 2 * tq), BF16),
                        pltpu.VMEM((N_PAIRS, 1, 2 * tq), F32),
                        pltpu.VMEM((N_PAIRS, 2, HEAD_DIM + ONES_ROWS, tq), F32),
                        pltpu.VMEM((w, tq), F32)],
        compiler_params=_cparams(2),
        name="dsa",
    )(qt_all, k_all, wt, qt_all, k_all, vt_all)


def _merge_kernel(ya_ref, yb_ref, h_ref, kv_ref, wmq_ref, wg_ref, wmix_ref, wb_ref, wo_ref, x_ref, gn_ref,
                  *out_refs, last):
    w = BRANCH_WIDTH
    d = x_ref.shape[1]
    h = h_ref[...]
    mq = jnp.dot(h, wmq_ref[...], preferred_element_type=F32).astype(BF16)
    ym_heads = []
    for hd in range(M_HEADS):
        cols = slice(hd * M_HEAD_DIM, (hd + 1) * M_HEAD_DIM)
        km = kv_ref[:, cols]
        vm = kv_ref[:, w + hd * M_HEAD_DIM:w + (hd + 1) * M_HEAD_DIM]
        sc = lax.dot_general(mq[:, cols], km, (((1,), (1,)), ((), ())),
                             preferred_element_type=F32) * (M_HEAD_DIM ** -0.5)
        sc = sc - jnp.max(sc, axis=-1, keepdims=True)
        e = jnp.exp(sc)
        p = e / jnp.sum(e, axis=-1, keepdims=True)
        ym_heads.append(jnp.dot(p.astype(BF16), vm, preferred_element_type=F32))
    ym = jnp.concatenate(ym_heads, axis=-1)
    ys = (ya_ref[...], yb_ref[...], ym)
    merged = None
    for n in range(3):
        g = jnp.dot(h, wg_ref[:, n * w:(n + 1) * w], preferred_element_type=F32)
        y = (ys[n] * (g * jax.nn.sigmoid(g))).astype(BF16)
        up = jnp.dot(y, wb_ref[n], preferred_element_type=F32)
        mix = jax.nn.sigmoid(jnp.dot(h, wmix_ref[:, n * d:(n + 1) * d], preferred_element_type=F32))
        term = mix * up
        merged = term if merged is None else merged + term
    x_new = x_ref[...] + jnp.dot(merged.astype(BF16), wo_ref[...], preferred_element_type=F32)
    y = x_new * lax.rsqrt(jnp.mean(x_new * x_new, axis=-1, keepdims=True) + RMS_EPS) * gn_ref[...]
    if last:
        out_refs[0][...] = y
    else:
        out_refs[0][...] = x_new
        out_refs[1][...] = y.astype(BF16)


def _merge(ya, yb, h, kv, w_mq, w_g, w_mix, wb, wo, x2d, g_next, last, seq, n_mem, tm):
    m, d = x2d.shape
    w = BRANCH_WIDTH
    nt = seq // tm

    def resident(shape):
        return pl.BlockSpec(shape, lambda i: (0,) * len(shape), pipeline_mode=pl.Buffered(1))

    row_tile = pl.BlockSpec((tm, d), lambda i: (i, 0))
    out_f32 = jax.ShapeDtypeStruct((m, d), F32)
    return pl.pallas_call(
        functools.partial(_merge_kernel, last=last),
        grid=(m // tm,),
        in_specs=[
            pl.BlockSpec((tm, w), lambda i: (i, 0)),
            pl.BlockSpec((tm, w), lambda i: (i, 0)),
            pl.BlockSpec((tm, d), lambda i: (i, 0)),
            pl.BlockSpec((n_mem, 2 * w), lambda i: (i // nt, 0)),
            resident((d, w)),
            resident((d, 3 * w)),
            resident((d, 3 * d)),
            resident((3, w, d)),
            resident((d, d)),
            row_tile,
            resident((1, d)),
        ],
        out_specs=row_tile if last else (row_tile, row_tile),
        out_shape=out_f32 if last else (out_f32, jax.ShapeDtypeStruct((m, d), BF16)),
        compiler_params=_cparams(1),
        name="merge",
    )(ya, yb, h, kv, w_mq, w_g, w_mix, wb, wo, x2d, g_next.reshape(1, d))


def _rope_tables(seq):
    inv_freq = ROPE_THETA ** (-jnp.arange(ROT_HALF, dtype=F32) / ROT_HALF)
    ang = jnp.arange(seq, dtype=jnp.int32).astype(F32)[:, None] * inv_freq[None, :]
    cos, sin = jnp.cos(ang), jnp.sin(ang)
    r = np.arange(LANES) % HEAD_DIM
    f = r % ROT_HALF
    first = jnp.asarray(r < ROT_HALF)
    second = jnp.asarray((r >= ROT_HALF) & (r < 2 * ROT_HALF))
    rot = jnp.asarray(r < 2 * ROT_HALF)
    c_tok = jnp.where(rot[None, :], cos[:, f], 1.0)
    s1_tok = jnp.where(first[None, :], -sin[:, f], 0.0)
    s2_tok = jnp.where(second[None, :], sin[:, f], 0.0)
    return (c_tok, s1_tok, s2_tok), (cos.T, sin.T)


def kernel(x, mem, norm_g, w_in, mem_norm_g, w_mem_kv, w_branch, w_out, final_g):
    batch, seq, d = x.shape
    n_mem = mem.shape[1]
    depth = norm_g.shape[0]
    w = BRANCH_WIDTH
    m = batch * seq
    assert seq % MOBA_BLOCK == 0 and d % LANES == 0

    tm = min(1024, seq)
    tm_merge = min(512, seq)
    dsa_tq = 256
    dsa_kc = 256
    dsa_kcb = min(512, seq)

    tok_tabs, feat_tabs = _rope_tables(seq)
    offs = np.cumsum([0, w, w, w, w, w, w, w, w, N_HEADS * HEAD_DIM, HEAD_DIM, N_HEADS, w, w, 3 * d])
    (o_aq, o_ak, o_av, o_ag, o_bq, o_bk, o_bv, o_bg, o_iq, o_ik, o_iw, o_mq, o_mg, o_mix, o_end) = offs
    attn_scale = HEAD_DIM ** -0.5 * float(np.log2(np.e))
    idx_scale = HEAD_DIM ** -0.5

    x2d = x.reshape(m, d)
    mem2d = mem.reshape(batch * n_mem, d)
    h = _rmsnorm(x2d, norm_g[0], BF16, tm)
    kv_all = _mem_kv(mem2d, mem_norm_g, w_mem_kv.astype(BF16))
    for l in range(depth):
        col = lambda a, b: lax.slice(w_in, (l, 0, a), (l + 1, d, b)).reshape(d, b - a)
        w_k = jnp.concatenate([col(o_ak, o_av), col(o_bk, o_bv), col(o_ik, o_iw), col(o_ik, o_iw)],
                              axis=1).astype(BF16)
        w_qt = jnp.concatenate([col(o_aq, o_ak) * attn_scale, col(o_bq, o_bk) * attn_scale,
                                col(o_iq, o_ik) * idx_scale], axis=1).T.astype(BF16)
        w_vt = jnp.concatenate([col(o_av, o_ag), col(o_bv, o_bg)], axis=1).T.astype(BF16)
        w_iwt = jnp.pad(col(o_iw, o_mq).T, ((0, 2 * SUBLANES - N_HEADS), (0, 0))).astype(BF16)
        w_mq = col(o_mq, o_mg).astype(BF16)
        w_g = jnp.concatenate([col(o_ag, o_bq), col(o_bg, o_iq), col(o_mg, o_mix)], axis=1).astype(BF16)
        w_mix = col(o_mix, o_end).astype(BF16)

        k_all, qt_all, vt_all, iwt = _in_proj(h, w_k, w_qt, w_vt, w_iwt, tok_tabs, feat_tabs, seq, tm)

        ya = _moba(qt_all, k_all, vt_all, batch, seq, 0, 0, 0)
        yb = _dsa(qt_all, k_all, vt_all, iwt, batch, seq, dsa_tq, dsa_kc, dsa_kcb)

        last = l == depth - 1
        res = _merge(ya, yb, h, kv_all[l], w_mq, w_g, w_mix, w_branch[l].astype(BF16), w_out[l].astype(BF16),
                     x2d, final_g if last else norm_g[l + 1], last, seq, n_mem, tm_merge)
        if not last:
            x2d, h = res

    return res.reshape(batch, seq, d)
```
